```python
import math
import jax, jax.numpy as jnp
from jax import lax
import numpy as np

D_MODEL = 1024
BATCH = 2
SEQ = 8192
DEPTH = 2

POOL_GROUPS = 4
POOL_GROUP_DIM = 128
POOL_WINDOWS = (2, 4, 8, 16)
POOL_DIM = POOL_GROUPS * POOL_GROUP_DIM
MLA_HEADS = 8
MLA_NOPE = 64
MLA_ROPE = 32
MLA_V = 64
MLA_Q_RANK = 384
MLA_KV_RANK = 256
ROPE_THETA = 10000.0
Q_BLOCK = 128
POS_OFFSET_MAX = 4096
GLA_HEADS = 4
GLA_DK = 64
GLA_DV = 128
GLA_GATE_RANK = 16
GLA_GATE_NORM = 16.0
GLA_CHUNK = 64
N_BRANCHES = 3
N_EXPERTS = 16
CAPACITY_FACTOR = 2
D_EXPERT = 2048
DN_ALPHA = (2 * DEPTH) ** 0.25
DN_BETA = (8 * DEPTH) ** (-0.25)
LN_EPS = 1e-5
RMS_EPS = 1e-6

IN_SPLITS = (
    POOL_DIM,
    MLA_Q_RANK,
    MLA_KV_RANK,
    MLA_ROPE,
    GLA_HEADS * GLA_DK,
    GLA_HEADS * GLA_DK,
    GLA_HEADS * GLA_DV,
    GLA_HEADS * GLA_DV,
    2 * GLA_GATE_RANK,
    N_BRANCHES * D_MODEL,
)
D_IN = sum(IN_SPLITS)

kernel_name = "hybrid_pool_mla_gla_ecmoe_deepnorm"


def layer_norm(x, g, b):
    xf = x.astype(jnp.float32)
    mu = jnp.mean(xf, axis=-1, keepdims=True)
    var = jnp.mean(jnp.square(xf - mu), axis=-1, keepdims=True)
    return ((xf - mu) * lax.rsqrt(var + LN_EPS) * g + b).astype(x.dtype)


def rms_norm(x, g):
    xf = x.astype(jnp.float32)
    ms = jnp.mean(jnp.square(xf), axis=-1, keepdims=True)
    return (xf * lax.rsqrt(ms + RMS_EPS) * g).astype(x.dtype)


def split_columns(proj):
    idx, acc = [], 0
    for s in IN_SPLITS[:-1]:
        acc += s
        idx.append(acc)
    return jnp.split(proj, idx, axis=-1)


def pool_mixer(u, w_group, scale):
    B, S, _ = u.shape
    uf = u.astype(jnp.float32).reshape(B, S, POOL_GROUPS, POOL_GROUP_DIM)
    csum = jnp.concatenate(
        [jnp.zeros((B, 1, POOL_GROUPS, POOL_GROUP_DIM), jnp.float32), jnp.cumsum(uf, axis=1)], axis=1)
    t = jnp.arange(S)
    outs = []
    for g, w in enumerate(POOL_WINDOWS):
        lo = jnp.clip(t - w // 2, 0, S)
        hi = jnp.clip(t + w // 2, 0, S)
        cg = csum[:, :, g]
        cnt = (hi - lo).astype(jnp.float32)[None, :, None]
        outs.append((cg[:, hi] - cg[:, lo]) / cnt - uf[:, :, g])
    pooled = jnp.stack(outs, axis=2)
    mixed = jnp.einsum('bsgc,gcd->bsgd', pooled, w_group.astype(jnp.float32)).reshape(B, S, POOL_DIM)
    return (mixed * scale).astype(u.dtype)


def rope_angles(positions, dim):
    half = dim // 2
    freqs = ROPE_THETA ** (-jnp.arange(half, dtype=jnp.float32) / half)
    ang = positions.astype(jnp.float32)[..., None] * freqs
    return jnp.cos(ang), jnp.sin(ang)


def apply_rope(x, cos, sin):
    half = x.shape[-1] // 2
    xf = x.astype(jnp.float32)
    x1, x2 = xf[..., :half], xf[..., half:]
    return jnp.concatenate([x1 * cos - x2 * sin, x1 * sin + x2 * cos], axis=-1).astype(x.dtype)


def mla(c_q, c_kv, k_r, positions, q_norm, w_uq, kv_norm, w_ukv):
    B, S, _ = c_q.shape
    q = jnp.einsum('bsr,rn->bsn', rms_norm(c_q, q_norm), w_uq).reshape(B, S, MLA_HEADS, MLA_NOPE + MLA_ROPE)
    kv = jnp.einsum('bsr,rn->bsn', rms_norm(c_kv, kv_norm), w_ukv).reshape(B, S, MLA_HEADS, MLA_NOPE + MLA_V)
    q_nope, q_rope = q[..., :MLA_NOPE], q[..., MLA_NOPE:]
    k_nope, v = kv[..., :MLA_NOPE], kv[..., MLA_NOPE:]
    cos, sin = rope_angles(positions, MLA_ROPE)
    q_rope = apply_rope(q_rope, cos[:, :, None, :], sin[:, :, None, :])
    k_rope = apply_rope(k_r, cos, sin)
    scale = (MLA_NOPE + MLA_ROPE) ** -0.5
    nb = S // Q_BLOCK
    qn_b = (q_nope * scale).reshape(B, nb, Q_BLOCK, MLA_HEADS, MLA_NOPE).transpose(1, 0, 2, 3, 4)
    qr_b = (q_rope * scale).reshape(B, nb, Q_BLOCK, MLA_HEADS, MLA_ROPE).transpose(1, 0, 2, 3, 4)

    def attend(blk):
        qn, qr = blk
        s = (jnp.einsum('bqhd,bkhd->bhqk', qn, k_nope).astype(jnp.float32)
             + jnp.einsum('bqhr,bkr->bhqk', qr, k_rope).astype(jnp.float32))
        p = jax.nn.softmax(s, axis=-1).astype(v.dtype)
        return jnp.einsum('bhqk,bkhd->bqhd', p, v)

    o = lax.map(attend, (qn_b, qr_b))
    return o.transpose(1, 0, 2, 3, 4).reshape(B, S, MLA_HEADS * MLA_V)


def gla_direction(q, k, v, g):
    B, S, H, DK = q.shape
    L = GLA_CHUNK
    N = S // L

    def resh(t):
        return t.reshape(B, N, L, H, t.shape[-1]).transpose(0, 3, 1, 2, 4)

    q, k, v, g = resh(q), resh(k), resh(v), resh(g)
    b = jnp.cumsum(g, axis=3)
    b_last = b[:, :, :, -1:, :]
    q_in = q * jnp.exp(b)
    k_in = k * jnp.exp(-b)
    k_st = k * jnp.exp(b_last - b)
    mask = jnp.tril(jnp.ones((L, L), dtype=bool))
    a = jnp.where(mask, jnp.einsum('bhnid,bhnjd->bhnij', q_in, k_in), 0.0)
    o_intra = jnp.einsum('bhnij,bhnjv->bhniv', a, v)
    dec = jnp.exp(b_last[:, :, :, 0, :])
    kv_chunk = jnp.einsum('bhnjd,bhnjv->bhndv', k_st, v)

    def step(state, inp):
        dec_n, kv_n = inp
        return dec_n[..., None] * state + kv_n, state

    init = jnp.zeros((B, H, DK, v.shape[-1]), jnp.float32)
    _, s_prev = lax.scan(step, init, (jnp.moveaxis(dec, 2, 0), jnp.moveaxis(kv_chunk, 2, 0)))
    s_prev = jnp.moveaxis(s_prev, 0, 2)
    o_inter = jnp.einsum('bhnid,bhndv->bhniv', q_in, s_prev)
    return (o_intra + o_inter).transpose(0, 2, 3, 1, 4).reshape(B, S, H, v.shape[-1])


def gla(q, k, v, r, dlow, w_dec2, b_dec, norm_g):
    B, S, _ = q.shape
    qf = q.astype(jnp.float32).reshape(B, S, GLA_HEADS, GLA_DK) * (GLA_DK ** -0.5)
    kf = k.astype(jnp.float32).reshape(B, S, GLA_HEADS, GLA_DK)
    vf = v.astype(jnp.float32).reshape(B, S, GLA_HEADS, GLA_DV)
    logits = jnp.einsum('bsir,irk->bsik', dlow.reshape(B, S, 2, GLA_GATE_RANK), w_dec2) + b_dec
    g = (jax.nn.log_sigmoid(logits.astype(jnp.float32)) / GLA_GATE_NORM).reshape(B, S, 2, GLA_HEADS, GLA_DK)
    o_f = gla_direction(qf, kf, vf, g[:, :, 0])
    flip = lambda t: jnp.flip(t, axis=1)
    o_b = flip(gla_direction(flip(qf), flip(kf), flip(vf), flip(g[:, :, 1])))
    o = rms_norm(o_f + o_b, norm_g).reshape(B, S, GLA_HEADS * GLA_DV)
    return (o * jax.nn.silu(r.astype(jnp.float32))).astype(q.dtype)


def hybrid_mixer(h, positions, w_in, b_gate, pool_w, pool_scale, w_up_a,
                 mla_q_norm, mla_w_uq, mla_kv_norm, mla_w_ukv, w_up_b,
                 gla_w_dec, gla_b_dec, gla_norm, w_up_c, w_out):
    B, S, D = h.shape
    proj = jnp.einsum('bsd,dn->bsn', h, w_in)
    u_pool, c_q, c_kv, k_r, g_q, g_k, g_v, g_r, g_dec, gate_logits = split_columns(proj)
    y_a = jnp.einsum('bsc,cd->bsd', pool_mixer(u_pool, pool_w, pool_scale), w_up_a)
    y_b = jnp.einsum('bsc,cd->bsd', mla(c_q, c_kv, k_r, positions, mla_q_norm, mla_w_uq, mla_kv_norm, mla_w_ukv), w_up_b)
    y_c = jnp.einsum('bsc,cd->bsd', gla(g_q, g_k, g_v, g_r, g_dec, gla_w_dec, gla_b_dec, gla_norm), w_up_c)
    gates = jax.nn.sigmoid((gate_logits + b_gate).astype(jnp.float32)).reshape(B, S, N_BRANCHES, D).astype(h.dtype)
    merged = gates[:, :, 0] * y_a + gates[:, :, 1] * y_b + gates[:, :, 2] * y_c
    return jnp.einsum('bsd,de->bse', merged, w_out)


def expert_choice_ffn(h, router_w, w_gate, w_up, w_down):
    B, S, D = h.shape
    cap = CAPACITY_FACTOR * S // N_EXPERTS
    affinity = jax.nn.softmax(jnp.einsum('bsd,de->bse', h, router_w).astype(jnp.float32), axis=-1)
    gate, idx = lax.top_k(jnp.transpose(affinity, (0, 2, 1)), cap)
    xe = jax.vmap(lambda hb, ib: hb[ib])(h, idx)
    hid = jax.nn.silu(jnp.einsum('becd,edf->becf', xe, w_gate)) * jnp.einsum('becd,edf->becf', xe, w_up)
    ye = jnp.einsum('becf,efd->becd', hid, w_down) * gate[..., None].astype(h.dtype)
    return jax.vmap(lambda ib, yb: jnp.zeros((S, D), yb.dtype).at[ib.reshape(-1)].add(yb.reshape(-1, D)))(idx, ye)


def setup_inputs(seed: int = 0) -> dict:
    key = jax.random.key(seed)
    ks = jax.random.split(key, 32)
    L = DEPTH
    f32 = jnp.float32

    def w(k, shape, fan_in, gain=1.0):
        return jax.random.normal(k, shape, f32) * (gain * fan_in ** -0.5)

    def gain(k, shape):
        return 1.0 + 0.02 * jax.random.normal(k, shape, f32)

    def bias(k, shape):
        return 0.02 * jax.random.normal(k, shape, f32)

    x = jax.random.normal(ks[0], (BATCH, SEQ, D_MODEL), f32)
    positions = (jnp.arange(SEQ, dtype=jnp.int32)[None, :]
                 + jax.random.randint(ks[1], (BATCH, 1), 0, POS_OFFSET_MAX, dtype=jnp.int32))
    return {
        "x": x,
        "positions": positions,
        "ln0_g": gain(ks[2], (D_MODEL,)),
        "ln0_b": bias(ks[3], (D_MODEL,)),
        "w_in": w(ks[4], (L, D_MODEL, D_IN), D_MODEL),
        "b_gate": bias(ks[5], (L, N_BRANCHES * D_MODEL)),
        "pool_w": w(ks[6], (L, POOL_GROUPS, POOL_GROUP_DIM, POOL_GROUP_DIM), POOL_GROUP_DIM),
        "pool_scale": gain(ks[7], (L, POOL_DIM)),
        "w_up_a": w(ks[8], (L, POOL_DIM, D_MODEL), POOL_DIM),
        "mla_q_norm": gain(ks[9], (L, MLA_Q_RANK)),
        "mla_w_uq": w(ks[10], (L, MLA_Q_RANK, MLA_HEADS * (MLA_NOPE + MLA_ROPE)), MLA_Q_RANK),
        "mla_kv_norm": gain(ks[11], (L, MLA_KV_RANK)),
        "mla_w_ukv": w(ks[12], (L, MLA_KV_RANK, MLA_HEADS * (MLA_NOPE + MLA_V)), MLA_KV_RANK),
        "w_up_b": w(ks[13], (L, MLA_HEADS * MLA_V, D_MODEL), MLA_HEADS * MLA_V),
        "gla_w_dec": w(ks[14], (L, 2, GLA_GATE_RANK, GLA_HEADS * GLA_DK), GLA_GATE_RANK),
        "gla_b_dec": bias(ks[15], (L, 2, GLA_HEADS * GLA_DK)),
        "gla_norm": gain(ks[16], (L, GLA_DV)),
        "w_up_c": w(ks[17], (L, GLA_HEADS * GLA_DV, D_MODEL), GLA_HEADS * GLA_DV),
        "w_out": w(ks[18], (L, D_MODEL, D_MODEL), D_MODEL, DN_BETA),
        "ln1_g": gain(ks[19], (L, D_MODEL)),
        "ln1_b": bias(ks[20], (L, D_MODEL)),
        "router_w": w(ks[21], (L, D_MODEL, N_EXPERTS), D_MODEL),
        "exp_w_gate": w(ks[22], (L, N_EXPERTS, D_MODEL, D_EXPERT), D_MODEL),
        "exp_w_up": w(ks[23], (L, N_EXPERTS, D_MODEL, D_EXPERT), D_MODEL),
        "exp_w_down": w(ks[24], (L, N_EXPERTS, D_EXPERT, D_MODEL), D_EXPERT, DN_BETA),
        "ln2_g": gain(ks[25], (L, D_MODEL)),
        "ln2_b": bias(ks[26], (L, D_MODEL)),
    }


def reference(x, positions, ln0_g, ln0_b, w_in, b_gate, pool_w, pool_scale, w_up_a,
              mla_q_norm, mla_w_uq, mla_kv_norm, mla_w_ukv, w_up_b,
              gla_w_dec, gla_b_dec, gla_norm, w_up_c, w_out, ln1_g, ln1_b,
              router_w, exp_w_gate, exp_w_up, exp_w_down, ln2_g, ln2_b):
    h = layer_norm(x, ln0_g, ln0_b)
    for l in range(DEPTH):
        mix = hybrid_mixer(h, positions, w_in[l], b_gate[l], pool_w[l], pool_scale[l], w_up_a[l],
                           mla_q_norm[l], mla_w_uq[l], mla_kv_norm[l], mla_w_ukv[l], w_up_b[l],
                           gla_w_dec[l], gla_b_dec[l], gla_norm[l], w_up_c[l], w_out[l])
        h = layer_norm(DN_ALPHA * h + mix, ln1_g[l], ln1_b[l])
        ffn = expert_choice_ffn(h, router_w[l], exp_w_gate[l], exp_w_up[l], exp_w_down[l])
        h = layer_norm(DN_ALPHA * h + ffn, ln2_g[l], ln2_b[l])
    return h
```

```python
import functools

import jax
import jax.numpy as jnp
from jax import lax
from jax.experimental import pallas as pl
from jax.experimental.pallas import tpu as pltpu

F32 = jnp.float32
BF16 = jnp.bfloat16

LANES = 128
VMEM_LIMIT = 56 * 1024 * 1024

D_MODEL = 1024
DEPTH = 2
POOL_WINDOWS = (2, 4, 8, 16)
POOL_DIM = 512
POOL_HALO = 16
MLA_HEADS = 8
MLA_NOPE = 64
MLA_ROPE = 32
MLA_V = 64
MLA_Q_RANK = 384
MLA_KV_RANK = 256
ROPE_THETA = 10000.0
GLA_HEADS = 4
GLA_DK = 64
GLA_DV = 128
GLA_GATE_RANK = 16
GLA_GATE_NORM = 16.0
GLA_CHUNK = 64
N_EXPERTS = 16
CAPACITY_FACTOR = 2
D_EXPERT = 2048
DN_ALPHA = (2 * DEPTH) ** 0.25
LN_EPS = 1e-5
RMS_EPS = 1e-6
LOG2_E = 1.4426950408889634

HEAD_PAD = LANES
GLA_W = GLA_HEADS * HEAD_PAD
HA_W = 2 * D_MODEL + LANES

_SEG = (("pool", POOL_DIM), ("cq", MLA_Q_RANK), ("ckv", MLA_KV_RANK), ("kr2", 2 * HEAD_PAD),
        ("gq", GLA_W), ("gk", GLA_W), ("gv", GLA_W), ("gr", GLA_W), ("gd", LANES),
        ("gates", 3 * D_MODEL))
_SEG_OFF = {}
_off = 0
for _n, _w in _SEG:
    _SEG_OFF[_n] = (_off, _w)
    _off += _w
W_ALL_COLS = _off


def _params(*sem):
    return pltpu.CompilerParams(dimension_semantics=sem, vmem_limit_bytes=VMEM_LIMIT)


def _dot(a, b):
    return jnp.dot(a, b, preferred_element_type=F32)


def _dot_nt(a, b):
    return lax.dot_general(a, b, (((1,), (1,)), ((), ())), preferred_element_type=F32)


def _split(x):
    hi = x.astype(BF16)
    lo = (x - hi.astype(F32)).astype(BF16)
    return hi, lo


def _dot3(a, b, nt=False):
    d = _dot_nt if nt else _dot
    ah, al = _split(a)
    bh, bl = _split(b)
    return d(ah, bh) + d(ah, bl) + d(al, bh)


def _layer_norm(x, g, b):
    mu = jnp.mean(x, axis=-1, keepdims=True)
    xc = x - mu
    var = jnp.mean(xc * xc, axis=-1, keepdims=True)
    return xc * lax.rsqrt(var + LN_EPS) * g + b


def _rms_norm(x, g):
    ms = jnp.mean(x * x, axis=-1, keepdims=True)
    return x * lax.rsqrt(ms + RMS_EPS) * g


def _rope_kernel(pos_ref, freq_ref, mask_ref, cos_ref, sin_ref):
    ang = pos_ref[...].astype(F32) * freq_ref[...]
    cos_ref[...] = jnp.cos(ang) * mask_ref[...]
    sin_ref[...] = jnp.sin(ang) * mask_ref[...]


def _rope_tables(pos_col, freq_row, mask_row, tm=512):
    T = pos_col.shape[0]
    row = pl.BlockSpec((1, LANES), lambda i: (0, 0))
    out = pl.BlockSpec((tm, LANES), lambda i: (i, 0))
    return pl.pallas_call(
        _rope_kernel,
        grid=(T // tm,),
        in_specs=[pl.BlockSpec((tm, 1), lambda i: (i, 0)), row, row],
        out_specs=[out, out],
        out_shape=[jax.ShapeDtypeStruct((T, LANES), F32)] * 2,
        compiler_params=_params("parallel"),
        name="rope_tables",
    )(pos_col, freq_row, mask_row)


def _inproj_kernel(x_ref, g_ref, b_ref, w_ref, h_ref, *out_refs):
    h = _layer_norm(x_ref[...], g_ref[...], b_ref[...])
    h_ref[...] = h
    hb = h.astype(BF16)
    for (name, width), o_ref in zip(_SEG, out_refs):
        off = _SEG_OFF[name][0]
        o_ref[...] = _dot(hb, w_ref[:, off:off + width]).astype(o_ref.dtype)


def _ln_inproj(xin, g, b, w_all, tm=256):
    T = xin.shape[0]
    const = lambda i: (0, 0)
    out_shapes = [jax.ShapeDtypeStruct((T, D_MODEL), F32)]
    out_specs = [pl.BlockSpec((tm, D_MODEL), lambda i: (i, 0))]
    for name, width in _SEG:
        out_shapes.append(jax.ShapeDtypeStruct((T, width), F32 if name == "gd" else BF16))
        out_specs.append(pl.BlockSpec((tm, width), lambda i: (i, 0)))
    return pl.pallas_call(
        _inproj_kernel,
        grid=(T // tm,),
        in_specs=[pl.BlockSpec((tm, D_MODEL), lambda i: (i, 0)),
                  pl.BlockSpec((1, D_MODEL), const), pl.BlockSpec((1, D_MODEL), const),
                  pl.BlockSpec((D_MODEL, W_ALL_COLS), const)],
        out_specs=out_specs,
        out_shape=out_shapes,
        compiler_params=_params("parallel"),
        name="ln_inproj",
    )(xin, g, b, w_all)


def _mla_prep_kernel(cq_ref, ckv_ref, kr2_ref, cos_ref, sin_ref, qn_ref, kvn_ref,
                     wq_ref, wq2_ref, wk_ref, wv_ref, nope_ref, one_ref,
                     q_ref, k_ref, v_ref):
    scale = (MLA_NOPE + MLA_ROPE) ** -0.5 * LOG2_E
    cosr = cos_ref[...]
    sinr = sin_ref[...]
    nq = _rms_norm(cq_ref[...].astype(F32), qn_ref[...]).astype(BF16)
    nkv = _rms_norm(ckv_ref[...].astype(F32), kvn_ref[...]).astype(BF16)
    qa = _dot(nq, wq_ref[...])
    qb = _dot(nq, wq2_ref[...])
    ka = _dot(nkv, wk_ref[...])
    va = _dot(nkv, wv_ref[...])
    kr2 = kr2_ref[...].astype(F32)
    k_rope = kr2[:, :HEAD_PAD] * cosr + kr2[:, HEAD_PAD:] * sinr
    q_cos = (cosr + nope_ref[...]) * scale
    q_sin = sinr * scale
    for h in range(MLA_HEADS):
        sl = slice(h * HEAD_PAD, (h + 1) * HEAD_PAD)
        q_ref[0, h] = (qa[:, sl] * q_cos + qb[:, sl] * q_sin).astype(BF16)
        k_ref[0, h] = (ka[:, sl] + k_rope).astype(BF16)
        v_ref[0, h] = (va[:, sl] + one_ref[...]).astype(BF16)


def _mla_prep(cq, ckv, kr2, cos_t, sin_t, qn, kvn, wq, wq2, wk, wv, nope_row, one_row, B, S,
              tm=512):
    nb = S // tm
    rows = lambda w: pl.BlockSpec((tm, w), lambda b, i: (b * nb + i, 0))
    const = lambda r, c: pl.BlockSpec((r, c), lambda b, i: (0, 0))
    hw = MLA_HEADS * HEAD_PAD
    out = pl.BlockSpec((1, MLA_HEADS, tm, HEAD_PAD), lambda b, i: (b, 0, i, 0))
    return pl.pallas_call(
        _mla_prep_kernel,
        grid=(B, nb),
        in_specs=[rows(MLA_Q_RANK), rows(MLA_KV_RANK), rows(2 * HEAD_PAD), rows(LANES),
                  rows(LANES), const(1, MLA_Q_RANK), const(1, MLA_KV_RANK),
                  const(MLA_Q_RANK, hw), const(MLA_Q_RANK, hw), const(MLA_KV_RANK, hw),
                  const(MLA_KV_RANK, hw), const(1, LANES), const(1, LANES)],
        out_specs=[out, out, out],
        out_shape=[jax.ShapeDtypeStruct((B, MLA_HEADS, S, HEAD_PAD), BF16)] * 3,
        compiler_params=_params("parallel", "parallel"),
        name="mla_prep",
    )(cq, ckv, kr2, cos_t, sin_t, qn, kvn, wq, wq2, wk, wv, nope_row, one_row)


def _flash_kernel(q_ref, k_ref, v_ref, o_ref, m_scr, acc_scr):
    kv = pl.program_id(2)

    @pl.when(kv == 0)
    def _():
        m_scr[...] = jnp.full(m_scr.shape, -jnp.inf, F32)
        acc_scr[...] = jnp.zeros(acc_scr.shape, F32)

    tk = k_ref.shape[2]
    for h in range(MLA_HEADS):
        s = _dot_nt(q_ref[0, h], k_ref[0, h])
        m_prev = m_scr[h]
        m_new = jnp.maximum(m_prev, jnp.max(s, axis=1, keepdims=True))
        p = jnp.exp2(s - jnp.concatenate([m_new] * (tk // LANES), axis=1))
        alpha = jnp.exp2(m_prev - m_new)
        acc_scr[h] = alpha * acc_scr[h] + _dot(p.astype(BF16), v_ref[0, h])
        m_scr[h] = m_new

    @pl.when(kv == pl.num_programs(2) - 1)
    def _():
        for h in range(MLA_HEADS):
            acc = acc_scr[h]
            o_ref[0, h] = (acc / acc[:, MLA_V:MLA_V + 1]).astype(BF16)


def _flash(q, k, v, tq=512, tk=512):
    B, H, S, _ = q.shape
    qspec = pl.BlockSpec((1, H, tq, HEAD_PAD), lambda b, i, j: (b, 0, i, 0))
    kspec = pl.BlockSpec((1, H, tk, HEAD_PAD), lambda b, i, j: (b, 0, j, 0))
    return pl.pallas_call(
        _flash_kernel,
        grid=(B, S // tq, S // tk),
        in_specs=[qspec, kspec, kspec],
        out_specs=qspec,
        out_shape=jax.ShapeDtypeStruct((B, H, S, HEAD_PAD), BF16),
        scratch_shapes=[pltpu.VMEM((H, tq, LANES), F32), pltpu.VMEM((H, tq, HEAD_PAD), F32)],
        compiler_params=_params("parallel", "parallel", "arbitrary"),
        name="mla_flash",
    )(q, k, v)


def _log_sigmoid(x):
    return jnp.minimum(x, 0.0) - jnp.log1p(jnp.exp(-jnp.abs(x)))


def _gla_scan(reverse, gq_ref, gk_ref, gv_ref, gd_ref, wd_ref, bd_ref, tri_ref, state_scr,
              o_scr):
    tm = gq_ref.shape[0]
    L = GLA_CHUNK
    n_chunks = tm // L

    @pl.when(pl.program_id(1) == 0)
    def _():
        state_scr[...] = jnp.zeros(state_scr.shape, F32)

    logits = _dot3(gd_ref[...], wd_ref[...]) + bd_ref[...]
    g = _log_sigmoid(logits) * (1.0 / GLA_GATE_NORM)
    g_hi, g_lo = _split(g)
    b = _dot(tri_ref[...], g_hi) + _dot(tri_ref[...], g_lo)
    edge = 0 if reverse else L - 1
    b_last = jnp.concatenate(
        [jnp.broadcast_to(b[c * L + edge:c * L + edge + 1, :], (L, GLA_W))
         for c in range(n_chunks)], axis=0)
    q_in = (gq_ref[...].astype(F32) * (GLA_DK ** -0.5) * jnp.exp(b)).astype(BF16)
    k_f = gk_ref[...].astype(F32)
    k_in = (k_f * jnp.exp(-b)).astype(BF16)
    k_st = (k_f * jnp.exp(b_last - b)).astype(BF16)
    dec = jnp.exp(b_last)
    v = gv_ref[...]
    lane = lax.broadcasted_iota(jnp.int32, (HEAD_PAD, 2 * L), 1)
    order = range(n_chunks - 1, -1, -1) if reverse else range(n_chunks)
    for h in range(GLA_HEADS):
        hs = slice(h * HEAD_PAD, (h + 1) * HEAD_PAD)
        a = _dot_nt(q_in[:, hs], k_in[:, hs]).astype(BF16) * tri_ref[...]
        o_scr[:, hs] = _dot(a, v[:, hs])
        v_t = v[:, hs].astype(F32).T
        st = state_scr[h]
        for c in order:
            rs = slice(c * L, (c + 1) * L)
            pair = slice((c // 2) * 2 * L, (c // 2 + 1) * 2 * L)
            o_scr[rs, hs] += _dot_nt(q_in[rs, hs], st.astype(BF16))
            in_chunk = (lane >= L) if c % 2 else (lane < L)
            v_tc = jnp.where(in_chunk, v_t[:, pair], 0.0).astype(BF16)
            st = dec[c * L:c * L + 1, hs] * st + _dot(v_tc, k_st[pair, hs])
        state_scr[h] = st


def _gla_fwd_kernel(gq_ref, gk_ref, gv_ref, gd_ref, wd_ref, bd_ref, tri_ref, o_ref, state_scr,
                    o_scr):
    _gla_scan(False, gq_ref, gk_ref, gv_ref, gd_ref, wd_ref, bd_ref, tri_ref, state_scr, o_scr)
    o_ref[...] = o_scr[...]


def _gla_bwd_kernel(gq_ref, gk_ref, gv_ref, gd_ref, wd_ref, bd_ref, tri_ref, of_ref, gr_ref,
                    ng_ref, out_ref, state_scr, o_scr):
    _gla_scan(True, gq_ref, gk_ref, gv_ref, gd_ref, wd_ref, bd_ref, tri_ref, state_scr, o_scr)
    r = gr_ref[...].astype(F32)
    gate = r * jax.nn.sigmoid(r)
    for h in range(GLA_HEADS):
        hs = slice(h * HEAD_PAD, (h + 1) * HEAD_PAD)
        o = _rms_norm(of_ref[:, hs] + o_scr[:, hs], ng_ref[...])
        out_ref[:, hs] = (o * gate[:, hs]).astype(BF16)


def _gla(gq, gk, gv, gd, gr, wd, bd, tri_f, tri_b, norm_g, B, S, tm=512):
    nb = S // tm
    T = B * S

    def call(reverse):
        blk = (lambda b, i: (b * nb + nb - 1 - i, 0)) if reverse else (lambda b, i: (b * nb + i, 0))
        rows = lambda w: pl.BlockSpec((tm, w), blk)
        const = lambda r, c: pl.BlockSpec((r, c), lambda b, i: (0, 0))
        in_specs = [rows(GLA_W), rows(GLA_W), rows(GLA_W), rows(LANES),
                    const(LANES, GLA_W), const(1, GLA_W), const(tm, tm)]
        scratch = [pltpu.VMEM((GLA_HEADS, GLA_DV, HEAD_PAD), F32), pltpu.VMEM((tm, GLA_W), F32)]
        d = 1 if reverse else 0
        args = [gq, gk, gv, gd, wd[d], bd[d], tri_b if reverse else tri_f]
        if reverse:
            in_specs += [rows(GLA_W), rows(GLA_W), const(1, GLA_DV)]
            args += [o_f, gr, norm_g]
        return pl.pallas_call(
            _gla_bwd_kernel if reverse else _gla_fwd_kernel,
            grid=(B, nb),
            in_specs=in_specs,
            out_specs=rows(GLA_W),
            out_shape=jax.ShapeDtypeStruct((T, GLA_W), BF16 if reverse else F32),
            scratch_shapes=scratch,
            compiler_params=_params("parallel", "arbitrary"),
            name="gla_bwd" if reverse else "gla_fwd",
        )(*args)

    o_f = call(False)
    return call(True)


def _merge_kernel(S, u_ref, up_ref, un_ref, om_ref, gla_ref, gl_ref, h_ref,
                  pw_ref, ps_ref, wa_ref, wb_ref, wc_ref, bg_ref, wo_ref, g_ref, b_ref,
                  rw_ref, rwt_ref, ha_ref, afft_ref):
    tm = u_ref.shape[0]
    i = pl.program_id(1)
    ext = jnp.concatenate([up_ref[...], u_ref[...], un_ref[...]], axis=0).astype(F32)
    pos = i * tm + lax.broadcasted_iota(jnp.int32, (tm, LANES), 0)
    pooled = []
    for gi, w in enumerate(POOL_WINDOWS):
        hw = w // 2
        cs = slice(gi * LANES, (gi + 1) * LANES)
        acc = jnp.zeros((tm, LANES), F32)
        for d in range(-hw, hw):
            valid = (pos + d >= 0) & (pos + d < S)
            acc += jnp.where(valid, ext[POOL_HALO + d:POOL_HALO + d + tm, cs], 0.0)
        cnt = (jnp.minimum(pos + hw, S) - jnp.maximum(pos - hw, 0)).astype(F32)
        pg = (acc / cnt - ext[POOL_HALO:POOL_HALO + tm, cs]).astype(BF16)
        pooled.append(_dot(pg, pw_ref[gi]))
    pa = (jnp.concatenate(pooled, axis=1) * ps_ref[...]).astype(BF16)
    y_a = _dot(pa, wa_ref[...])
    y_b = _dot(om_ref[0, 0], wb_ref[0])
    for hh in range(1, MLA_HEADS):
        y_b += _dot(om_ref[0, hh], wb_ref[hh])
    y_c = _dot(gla_ref[...], wc_ref[...])
    gates = jax.nn.sigmoid(gl_ref[...].astype(F32) + bg_ref[...])
    merged = (gates[:, :D_MODEL] * y_a + gates[:, D_MODEL:2 * D_MODEL] * y_b
              + gates[:, 2 * D_MODEL:] * y_c)
    mix = _dot(merged.astype(BF16), wo_ref[...])
    h1 = _layer_norm(DN_ALPHA * h_ref[...] + mix, g_ref[...], b_ref[...])
    lane = lax.broadcasted_iota(jnp.int32, (tm, LANES), 1)
    logits = jnp.where(lane < N_EXPERTS, _dot3(h1, rw_ref[...]), -jnp.inf)
    e = jnp.exp(logits - jnp.max(logits, axis=1, keepdims=True))
    aff = e / jnp.sum(e, axis=1, keepdims=True)
    lt = _dot3(rwt_ref[...], h1, nt=True)[:N_EXPERTS]
    et = jnp.exp(lt - jnp.max(lt, axis=0, keepdims=True))
    afft_ref[0] = et / jnp.sum(et, axis=0, keepdims=True)
    ha_ref[:, :D_MODEL] = DN_ALPHA * h1
    ha_ref[:, D_MODEL:2 * D_MODEL] = h1
    ha_ref[:, 2 * D_MODEL:] = aff


def _merge(u, o_mla, gla_o, gl, h, pw, ps, wa, wb, wc, bg, wo, g, b, rw, rwt, B, S, tm=512):
    nb = S // tm
    T = B * S
    hb = tm // POOL_HALO
    n_halo = T // POOL_HALO
    rows = lambda w: pl.BlockSpec((tm, w), lambda bb, i: (bb * nb + i, 0))
    prev = pl.BlockSpec((POOL_HALO, POOL_DIM),
                        lambda bb, i: (jnp.maximum((bb * nb + i) * hb - 1, 0), 0))
    nxt = pl.BlockSpec((POOL_HALO, POOL_DIM),
                       lambda bb, i: (jnp.minimum((bb * nb + i + 1) * hb, n_halo - 1), 0))
    c2 = lambda r, c: pl.BlockSpec((r, c), lambda bb, i: (0, 0))
    c3 = lambda a, r, c: pl.BlockSpec((a, r, c), lambda bb, i: (0, 0, 0))
    return pl.pallas_call(
        functools.partial(_merge_kernel, S),
        grid=(B, nb),
        in_specs=[rows(POOL_DIM), prev, nxt,
                  pl.BlockSpec((1, MLA_HEADS, tm, HEAD_PAD), lambda bb, i: (bb, 0, i, 0)),
                  rows(GLA_W), rows(3 * D_MODEL), rows(D_MODEL),
                  c3(len(POOL_WINDOWS), LANES, LANES), c2(1, POOL_DIM), c2(POOL_DIM, D_MODEL),
                  c3(MLA_HEADS, HEAD_PAD, D_MODEL), c2(GLA_W, D_MODEL), c2(1, 3 * D_MODEL),
                  c2(D_MODEL, D_MODEL), c2(1, D_MODEL), c2(1, D_MODEL),
                  c2(D_MODEL, LANES), c2(LANES, D_MODEL)],
        out_specs=[rows(HA_W),
                   pl.BlockSpec((1, N_EXPERTS, tm), lambda bb, i: (bb, 0, i))],
        out_shape=[jax.ShapeDtypeStruct((T, HA_W), F32),
                   jax.ShapeDtypeStruct((B, N_EXPERTS, S), F32)],
        compiler_params=_params("parallel", "parallel"),
        name="merge",
    )(u, u, u, o_mla, gla_o, gl, h, pw, ps, wa, wb, wc, bg, wo, g, b, rw, rwt)


def _topk_kernel(C, aff_ref, tri_ref, idx_ref, p_scr, acc_scr):
    S = aff_ref.shape[2]
    n_chunks = S // LANES
    aff = aff_ref[0]

    def count(mask):
        return jnp.sum(jnp.where(mask, 1.0, 0.0), axis=1, keepdims=True)

    def as_float(bits):
        return lax.bitcast_convert_type(bits, F32)

    def refine(i, thr):
        cand = thr | jnp.left_shift(jnp.int32(1), 30 - i)
        return jnp.where(count(aff >= as_float(cand)) >= C, cand, thr)

    thr = lax.fori_loop(0, 31, refine, jnp.zeros((N_EXPERTS, 1), jnp.int32))
    above = aff >= as_float(thr + 1)
    tied = (aff >= as_float(thr)) & jnp.logical_not(above)
    need = C - count(above)
    tri = tri_ref[...]

    def chunk_prefix(mask_f32, c, run):
        m = mask_f32[:, c * LANES:(c + 1) * LANES]
        incl = _dot(m.astype(BF16), tri) + run
        return incl, incl[:, LANES - 1:LANES]

    tied_f = tied.astype(F32)
    run = jnp.zeros((N_EXPERTS, 1), F32)
    sel_parts = []
    for c in range(n_chunks):
        incl, run = chunk_prefix(tied_f, c, run)
        cs = slice(c * LANES, (c + 1) * LANES)
        sel_parts.append(jnp.where(above[:, cs] | (tied[:, cs] & (incl <= need)), 1.0, 0.0))
    run = jnp.zeros((N_EXPERTS, 1), F32)
    for c in range(n_chunks):
        incl = _dot(sel_parts[c].astype(BF16), tri) + run
        run = incl[:, LANES - 1:LANES]
        p_scr[c] = incl

    slot = lax.broadcasted_iota(jnp.int32, (C, LANES), 0).astype(F32)
    ones = jnp.ones((8, LANES), BF16)
    for e in range(N_EXPERTS):
        acc_scr[...] = jnp.zeros(acc_scr.shape, F32)

        def add_chunk(c, carry):
            acc_scr[...] += jnp.where(p_scr[c][e:e + 1, :] <= slot, 1.0, 0.0)
            return carry

        lax.fori_loop(0, n_chunks, add_chunk, 0)
        tok = _dot_nt(ones, acc_scr[...].astype(BF16))[0:1, :]
        idx_ref[0, e:e + 1, :] = tok.astype(jnp.int32)


def _topk(afft, tri, C):
    B, E, S = afft.shape
    return pl.pallas_call(
        functools.partial(_topk_kernel, C),
        grid=(B,),
        in_specs=[pl.BlockSpec((1, E, S), lambda b: (b, 0, 0)),
                  pl.BlockSpec((LANES, LANES), lambda b: (0, 0))],
        out_specs=pl.BlockSpec((1, E, C), lambda b: (b, 0, 0)),
        out_shape=jax.ShapeDtypeStruct((B, E, C), jnp.int32),
        scratch_shapes=[pltpu.VMEM((S // LANES, E, LANES), F32), pltpu.VMEM((C, LANES), F32)],
        compiler_params=_params("parallel"),
        name="expert_choice",
    )(afft, tri)


def _ffn_kernel(tc, n_steps, idx_ref, ha_in_ref, wg_ref, wu_ref, wd_ref, ha_ref,
                gbuf, sbuf, gsem, ssem):
    del ha_in_ref
    e = pl.program_id(0)
    s = (e * pl.num_programs(1) + pl.program_id(1)) * pl.num_programs(2) + pl.program_id(2)
    slot = s % 2

    def gather_copy(step, r, sl):
        row = idx_ref[step * tc + r]
        return pltpu.make_async_copy(ha_ref.at[pl.ds(row, 1), :],
                                     gbuf.at[sl, pl.ds(r, 1), :], gsem.at[sl])

    def scatter_copy(step, r, sl):
        row = idx_ref[step * tc + r]
        return pltpu.make_async_copy(sbuf.at[sl, pl.ds(r, 1), :],
                                     ha_ref.at[pl.ds(row, 1), pl.ds(0, D_MODEL)], ssem.at[sl])

    def for_rows(fn):
        def body(r, carry):
            fn(r)
            return carry
        lax.fori_loop(0, tc, body, 0)

    @pl.when(s == 0)
    def _():
        for_rows(lambda r: gather_copy(s, r, slot).start())

    @pl.when(s >= 2)
    def _():
        for_rows(lambda r: scatter_copy(s - 2, r, slot).wait())

    @pl.when(s + 1 < n_steps)
    def _():
        for_rows(lambda r: gather_copy(s + 1, r, 1 - slot).start())

    for_rows(lambda r: gather_copy(s, r, slot).wait())

    rows = gbuf[slot]
    x = rows[:, D_MODEL:2 * D_MODEL].astype(BF16)
    lane = lax.broadcasted_iota(jnp.int32, (tc, LANES), 1)
    gate = jnp.sum(jnp.where(lane == e, rows[:, 2 * D_MODEL:], 0.0), axis=1, keepdims=True)
    y = jnp.zeros((tc, D_MODEL), F32)
    fh = D_EXPERT // 2
    for f in range(2):
        fs = slice(f * fh, (f + 1) * fh)
        hg = _dot(x, wg_ref[0, :, fs])
        hu = _dot(x, wu_ref[0, :, fs])
        hid = (hg * jax.nn.sigmoid(hg) * hu).astype(BF16)
        y += _dot(hid, wd_ref[0, fs, :])
    sbuf[slot] = rows[:, :D_MODEL] + gate * y
    for_rows(lambda r: scatter_copy(s, r, slot).start())

    @pl.when(s == n_steps - 1)
    def _():
        if n_steps >= 2:
            for_rows(lambda r: scatter_copy(s - 1, r, 1 - slot).wait())
        for_rows(lambda r: scatter_copy(s, r, slot).wait())


def _ffn(ha, idx_steps, wg, wu, wd, B, C, tc=512):
    nblk = C // tc
    n_steps = N_EXPERTS * B * nblk
    wspec = lambda r, c: pl.BlockSpec((1, r, c), lambda e, b, j, idx: (e, 0, 0))
    return pl.pallas_call(
        functools.partial(_ffn_kernel, tc, n_steps),
        grid_spec=pltpu.PrefetchScalarGridSpec(
            num_scalar_prefetch=1,
            grid=(N_EXPERTS, B, nblk),
            in_specs=[pl.BlockSpec(memory_space=pl.ANY),
                      wspec(D_MODEL, D_EXPERT), wspec(D_MODEL, D_EXPERT),
                      wspec(D_EXPERT, D_MODEL)],
            out_specs=pl.BlockSpec(memory_space=pl.ANY),
            scratch_shapes=[pltpu.VMEM((2, tc, HA_W), F32), pltpu.VMEM((2, tc, D_MODEL), F32),
                            pltpu.SemaphoreType.DMA((2,)), pltpu.SemaphoreType.DMA((2,))]),
        out_shape=jax.ShapeDtypeStruct(ha.shape, F32),
        input_output_aliases={1: 0},
        compiler_params=_params("arbitrary", "arbitrary", "arbitrary"),
        name="expert_ffn",
    )(idx_steps, ha, wg, wu, wd)


def _ln_kernel(x_ref, g_ref, b_ref, o_ref):
    o_ref[...] = _layer_norm(x_ref[...], g_ref[...], b_ref[...])


def _ln_rows(xin, g, b, tm=512):
    T = xin.shape[0]
    const = lambda i: (0, 0)
    return pl.pallas_call(
        _ln_kernel,
        grid=(T // tm,),
        in_specs=[pl.BlockSpec((tm, D_MODEL), lambda i: (i, 0)),
                  pl.BlockSpec((1, D_MODEL), const), pl.BlockSpec((1, D_MODEL), const)],
        out_specs=pl.BlockSpec((tm, D_MODEL), lambda i: (i, 0)),
        out_shape=jax.ShapeDtypeStruct((T, D_MODEL), F32),
        compiler_params=_params("parallel"),
        name="final_ln",
    )(xin, g, b)


def _pad_heads(w, heads, width):
    lead = w.shape[:-1]
    w = w.reshape(lead + (heads, width))
    w = jnp.pad(w, [(0, 0)] * len(lead) + [(0, 0), (0, HEAD_PAD - width)])
    return w.reshape(lead + (heads * HEAD_PAD,))


def _rot_half(w):
    half = w.shape[-1] // 2
    return jnp.concatenate([-w[..., half:], w[..., :half]], axis=-1)


def _pack_w_in(w):
    o = 0
    parts = {}
    for name, width in (("pool", POOL_DIM), ("cq", MLA_Q_RANK), ("ckv", MLA_KV_RANK),
                        ("kr", MLA_ROPE), ("gq", GLA_HEADS * GLA_DK), ("gk", GLA_HEADS * GLA_DK),
                        ("gv", GLA_HEADS * GLA_DV), ("gr", GLA_HEADS * GLA_DV),
                        ("gd", 2 * GLA_GATE_RANK), ("gates", 3 * D_MODEL)):
        parts[name] = w[:, o:o + width]
        o += width
    rope_slot = lambda m: jnp.pad(m, ((0, 0), (MLA_NOPE, HEAD_PAD - MLA_NOPE - MLA_ROPE)))
    kr2 = jnp.concatenate([rope_slot(parts["kr"]), rope_slot(_rot_half(parts["kr"]))], axis=1)
    gd = jnp.pad(parts["gd"], ((0, 0), (0, LANES - 2 * GLA_GATE_RANK)))
    cols = [parts["pool"], parts["cq"], parts["ckv"], kr2,
            _pad_heads(parts["gq"], GLA_HEADS, GLA_DK), _pad_heads(parts["gk"], GLA_HEADS, GLA_DK),
            parts["gv"], parts["gr"], gd, parts["gates"]]
    return jnp.concatenate(cols, axis=1).astype(BF16)


def _pack_mla(w_uq, w_ukv, w_up_b):
    r = w_uq.shape[0]
    uq = w_uq.reshape(r, MLA_HEADS, MLA_NOPE + MLA_ROPE)
    nope, rope = uq[..., :MLA_NOPE], uq[..., MLA_NOPE:]
    tail = jnp.zeros((r, MLA_HEADS, HEAD_PAD - MLA_NOPE - MLA_ROPE), F32)
    wq = jnp.concatenate([nope, rope, tail], axis=-1).reshape(r, -1)
    wq2 = jnp.concatenate([jnp.zeros_like(nope), _rot_half(rope), tail], axis=-1).reshape(r, -1)
    rk = w_ukv.shape[0]
    ukv = w_ukv.reshape(rk, MLA_HEADS, MLA_NOPE + MLA_V)
    wk = _pad_heads(ukv[..., :MLA_NOPE].reshape(rk, -1), MLA_HEADS, MLA_NOPE)
    wv = _pad_heads(ukv[..., MLA_NOPE:].reshape(rk, -1), MLA_HEADS, MLA_V)
    wb = jnp.pad(w_up_b.reshape(MLA_HEADS, MLA_V, D_MODEL), ((0, 0), (0, HEAD_PAD - MLA_V), (0, 0)))
    return wq.astype(BF16), wq2.astype(BF16), wk.astype(BF16), wv.astype(BF16), wb.astype(BF16)


def _pack_gla_decay(w_dec, b_dec):
    wd = []
    for d in range(2):
        rows = _pad_heads(w_dec[d], GLA_HEADS, GLA_DK)
        wd.append(jnp.pad(rows, ((d * GLA_GATE_RANK, LANES - (d + 1) * GLA_GATE_RANK), (0, 0))))
    bd = [_pad_heads(b_dec[d][None, :], GLA_HEADS, GLA_DK) for d in range(2)]
    return wd, bd


def _chunk_tri(tm, reverse):
    r = jnp.arange(tm)[:, None]
    c = jnp.arange(tm)[None, :]
    same = (r // GLA_CHUNK) == (c // GLA_CHUNK)
    return (same & ((c >= r) if reverse else (c <= r))).astype(BF16)


def kernel(x, positions, ln0_g, ln0_b, w_in, b_gate, pool_w, pool_scale, w_up_a, mla_q_norm,
           mla_w_uq, mla_kv_norm, mla_w_ukv, w_up_b, gla_w_dec, gla_b_dec, gla_norm, w_up_c,
           w_out, ln1_g, ln1_b, router_w, exp_w_gate, exp_w_up, exp_w_down, ln2_g, ln2_b):
    B, S, D = x.shape
    assert D == D_MODEL and S % 512 == 0 and B == 2
    T = B * S
    C = CAPACITY_FACTOR * S // N_EXPERTS
    tc = min(512, C)
    row = lambda v: v.reshape(1, -1).astype(F32)

    half = MLA_ROPE // 2
    freqs = ROPE_THETA ** (-jnp.arange(half, dtype=F32) / half)
    lanes = jnp.arange(LANES)
    in_rope = (lanes >= MLA_NOPE) & (lanes < MLA_NOPE + MLA_ROPE)
    freq_row = jnp.where(in_rope, freqs[(lanes - MLA_NOPE) % half], 0.0).reshape(1, LANES)
    rope_mask = in_rope.astype(F32).reshape(1, LANES)
    nope_row = (lanes < MLA_NOPE).astype(F32).reshape(1, LANES)
    one_row = (lanes == MLA_V).astype(F32).reshape(1, LANES)
    cos_t, sin_t = _rope_tables(positions.reshape(T, 1), freq_row, rope_mask)

    tri_f, tri_b = _chunk_tri(512, False), _chunk_tri(512, True)
    tri_lane = (jnp.arange(LANES)[:, None] <= jnp.arange(LANES)[None, :]).astype(BF16)

    stream, g_in, b_in = x.reshape(T, D), ln0_g, ln0_b
    for l in range(DEPTH):
        h, u, cq, ckv, kr2, gq, gk, gv, gr, gd, gl = _ln_inproj(
            stream, row(g_in), row(b_in), _pack_w_in(w_in[l]))
        wq, wq2, wk, wv, wb = _pack_mla(mla_w_uq[l], mla_w_ukv[l], w_up_b[l])
        q, k, v = _mla_prep(cq, ckv, kr2, cos_t, sin_t, row(mla_q_norm[l]), row(mla_kv_norm[l]),
                            wq, wq2, wk, wv, nope_row, one_row, B, S)
        o_mla = _flash(q, k, v)
        wd, bd = _pack_gla_decay(gla_w_dec[l], gla_b_dec[l])
        gla_o = _gla(gq, gk, gv, gd, gr, wd, bd, tri_f, tri_b, row(gla_norm[l]), B, S)
        rw = jnp.pad(router_w[l], ((0, 0), (0, LANES - N_EXPERTS)))
        ha, afft = _merge(u, o_mla, gla_o, gl, h, pool_w[l].astype(BF16), row(pool_scale[l]),
                          w_up_a[l].astype(BF16), wb, w_up_c[l].astype(BF16), row(b_gate[l]),
                          w_out[l].astype(BF16), row(ln1_g[l]), row(ln1_b[l]), rw, rw.T, B, S)
        idx = _topk(afft, tri_lane, C)
        idx_steps = (idx + (jnp.arange(B, dtype=jnp.int32) * S)[:, None, None])
        idx_steps = idx_steps.transpose(1, 0, 2).reshape(-1)
        stream = _ffn(ha, idx_steps, exp_w_gate[l].astype(BF16), exp_w_up[l].astype(BF16),
                      exp_w_down[l].astype(BF16), B, C, tc)
        g_in, b_in = ln2_g[l], ln2_b[l]
    out = _ln_rows(stream, row(g_in), row(b_in))
    return out.reshape(B, S, D)
```

```python
import functools

import jax
import jax.numpy as jnp
from jax import lax
from jax.experimental import pallas as pl
from jax.experimental.pallas import tpu as pltpu

F32 = jnp.float32
BF16 = jnp.bfloat16

LANES = 128
VMEM_LIMIT = 56 * 1024 * 1024

D_MODEL = 1024
DEPTH = 2
POOL_WINDOWS = (2, 4, 8, 16)
POOL_DIM = 512
POOL_HALO = 16
MLA_HEADS = 8
MLA_NOPE = 64
MLA_ROPE = 32
MLA_V = 64
MLA_Q_RANK = 384
MLA_KV_RANK = 256
ROPE_THETA = 10000.0
GLA_HEADS = 4
GLA_DK = 64
GLA_DV = 128
GLA_GATE_RANK = 16
GLA_GATE_NORM = 16.0
GLA_CHUNK = 64
N_EXPERTS = 16
CAPACITY_FACTOR = 2
D_EXPERT = 2048
DN_ALPHA = (2 * DEPTH) ** 0.25
LN_EPS = 1e-5
RMS_EPS = 1e-6
LOG2_E = 1.4426950408889634
NEVER = 1e9

HEAD_PAD = LANES
GLA_W = GLA_HEADS * HEAD_PAD
HA_W = 2 * D_MODEL + LANES
DMA_UNROLL = 16

_SEG = (("pool", POOL_DIM), ("cq", MLA_Q_RANK), ("ckv", MLA_KV_RANK), ("kr2", 2 * HEAD_PAD),
        ("gq", GLA_W), ("gk", GLA_W), ("gv", GLA_W), ("gr", GLA_W), ("gd", LANES),
        ("gates", 3 * D_MODEL))
_SEG_OFF = {}
_off = 0
for _n, _w in _SEG:
    _SEG_OFF[_n] = (_off, _w)
    _off += _w
W_ALL_COLS = _off


def _params(*sem):
    return pltpu.CompilerParams(dimension_semantics=sem, vmem_limit_bytes=VMEM_LIMIT)


def _dot(a, b):
    return jnp.dot(a, b, preferred_element_type=F32)


def _dot_nt(a, b):
    return lax.dot_general(a, b, (((1,), (1,)), ((), ())), preferred_element_type=F32)


def _split(x):
    hi = x.astype(BF16)
    lo = (x - hi.astype(F32)).astype(BF16)
    return hi, lo


def _dot3(a, b, nt=False):
    d = _dot_nt if nt else _dot
    ah, al = _split(a)
    bh, bl = _split(b)
    return d(ah, bh) + d(ah, bl) + d(al, bh)


def _layer_norm(x, g, b):
    mu = jnp.mean(x, axis=-1, keepdims=True)
    xc = x - mu
    var = jnp.mean(xc * xc, axis=-1, keepdims=True)
    return xc * lax.rsqrt(var + LN_EPS) * g + b


def _rms_norm(x, g):
    ms = jnp.mean(x * x, axis=-1, keepdims=True)
    return x * lax.rsqrt(ms + RMS_EPS) * g


def _rope_kernel(pos_ref, freq_ref, mask_ref, cos_ref, sin_ref):
    ang = pos_ref[...].astype(F32) * freq_ref[...]
    cos_ref[...] = jnp.cos(ang) * mask_ref[...]
    sin_ref[...] = jnp.sin(ang) * mask_ref[...]


def _rope_tables(pos_col, freq_row, mask_row, tm=512):
    T = pos_col.shape[0]
    row = pl.BlockSpec((1, LANES), lambda i: (0, 0))
    out = pl.BlockSpec((tm, LANES), lambda i: (i, 0))
    return pl.pallas_call(
        _rope_kernel,
        grid=(T // tm,),
        in_specs=[pl.BlockSpec((tm, 1), lambda i: (i, 0)), row, row],
        out_specs=[out, out],
        out_shape=[jax.ShapeDtypeStruct((T, LANES), F32)] * 2,
        compiler_params=_params("parallel"),
        name="rope_tables",
    )(pos_col, freq_row, mask_row)


def _inproj_kernel(x_ref, g_ref, b_ref, w_ref, h_ref, *out_refs):
    h = _layer_norm(x_ref[...], g_ref[...], b_ref[...])
    h_ref[...] = h
    hb = h.astype(BF16)
    for (name, width), o_ref in zip(_SEG, out_refs):
        off = _SEG_OFF[name][0]
        o_ref[...] = _dot(hb, w_ref[:, off:off + width]).astype(o_ref.dtype)


def _ln_inproj(xin, g, b, w_all, tm=256):
    T = xin.shape[0]
    const = lambda i: (0, 0)
    out_shapes = [jax.ShapeDtypeStruct((T, D_MODEL), F32)]
    out_specs = [pl.BlockSpec((tm, D_MODEL), lambda i: (i, 0))]
    for name, width in _SEG:
        out_shapes.append(jax.ShapeDtypeStruct((T, width), F32 if name == "gd" else BF16))
        out_specs.append(pl.BlockSpec((tm, width), lambda i: (i, 0)))
    return pl.pallas_call(
        _inproj_kernel,
        grid=(T // tm,),
        in_specs=[pl.BlockSpec((tm, D_MODEL), lambda i: (i, 0)),
                  pl.BlockSpec((1, D_MODEL), const), pl.BlockSpec((1, D_MODEL), const),
                  pl.BlockSpec((D_MODEL, W_ALL_COLS), const)],
        out_specs=out_specs,
        out_shape=out_shapes,
        compiler_params=_params("parallel"),
        name="ln_inproj",
    )(xin, g, b, w_all)


def _mla_prep_kernel(cq_ref, ckv_ref, kr2_ref, cos_ref, sin_ref, qn_ref, kvn_ref,
                     wq_ref, wq2_ref, wk_ref, wv_ref, nope_ref, one_ref,
                     q_ref, k_ref, v_ref):
    scale = (MLA_NOPE + MLA_ROPE) ** -0.5 * LOG2_E
    cosr = cos_ref[...]
    sinr = sin_ref[...]
    nq = _rms_norm(cq_ref[...].astype(F32), qn_ref[...]).astype(BF16)
    nkv = _rms_norm(ckv_ref[...].astype(F32), kvn_ref[...]).astype(BF16)
    qa = _dot(nq, wq_ref[...])
    qb = _dot(nq, wq2_ref[...])
    ka = _dot(nkv, wk_ref[...])
    va = _dot(nkv, wv_ref[...])
    kr2 = kr2_ref[...].astype(F32)
    k_rope = kr2[:, :HEAD_PAD] * cosr + kr2[:, HEAD_PAD:] * sinr
    q_cos = (cosr + nope_ref[...]) * scale
    q_sin = sinr * scale
    for h in range(MLA_HEADS):
        sl = slice(h * HEAD_PAD, (h + 1) * HEAD_PAD)
        q_ref[0, h] = (qa[:, sl] * q_cos + qb[:, sl] * q_sin).astype(BF16)
        k_ref[0, h] = (ka[:, sl] + k_rope).astype(BF16)
        v_ref[0, h] = (va[:, sl] + one_ref[...]).astype(BF16)


def _mla_prep(cq, ckv, kr2, cos_t, sin_t, qn, kvn, wq, wq2, wk, wv, nope_row, one_row, B, S,
              tm=512):
    nb = S // tm
    rows = lambda w: pl.BlockSpec((tm, w), lambda b, i: (b * nb + i, 0))
    const = lambda r, c: pl.BlockSpec((r, c), lambda b, i: (0, 0))
    hw = MLA_HEADS * HEAD_PAD
    out = pl.BlockSpec((1, MLA_HEADS, tm, HEAD_PAD), lambda b, i: (b, 0, i, 0))
    return pl.pallas_call(
        _mla_prep_kernel,
        grid=(B, nb),
        in_specs=[rows(MLA_Q_RANK), rows(MLA_KV_RANK), rows(2 * HEAD_PAD), rows(LANES),
                  rows(LANES), const(1, MLA_Q_RANK), const(1, MLA_KV_RANK),
                  const(MLA_Q_RANK, hw), const(MLA_Q_RANK, hw), const(MLA_KV_RANK, hw),
                  const(MLA_KV_RANK, hw), const(1, LANES), const(1, LANES)],
        out_specs=[out, out, out],
        out_shape=[jax.ShapeDtypeStruct((B, MLA_HEADS, S, HEAD_PAD), BF16)] * 3,
        compiler_params=_params("parallel", "parallel"),
        name="mla_prep",
    )(cq, ckv, kr2, cos_t, sin_t, qn, kvn, wq, wq2, wk, wv, nope_row, one_row)


def _flash_kernel(q_ref, k_ref, v_ref, o_ref, m_scr, acc_scr):
    kv = pl.program_id(2)

    @pl.when(kv == 0)
    def _():
        m_scr[...] = jnp.full(m_scr.shape, -jnp.inf, F32)
        acc_scr[...] = jnp.zeros(acc_scr.shape, F32)

    tk = k_ref.shape[2]
    for h in range(MLA_HEADS):
        s = _dot_nt(q_ref[0, h], k_ref[0, h])
        m_prev = m_scr[h]
        m_new = jnp.maximum(m_prev, jnp.max(s, axis=1, keepdims=True))
        p = jnp.exp2(s - jnp.concatenate([m_new] * (tk // LANES), axis=1))
        alpha = jnp.exp2(m_prev - m_new)
        acc_scr[h] = alpha * acc_scr[h] + _dot(p.astype(BF16), v_ref[0, h])
        m_scr[h] = m_new

    @pl.when(kv == pl.num_programs(2) - 1)
    def _():
        lane = lax.broadcasted_iota(jnp.int32, (q_ref.shape[2], HEAD_PAD), 1)
        for hp in range(MLA_HEADS // 2):
            a0 = acc_scr[2 * hp]
            a1 = acc_scr[2 * hp + 1]
            o0 = a0 / a0[:, MLA_V:MLA_V + 1]
            o1 = a1 / a1[:, MLA_V:MLA_V + 1]
            pair = jnp.where(lane < MLA_V, o0, pltpu.roll(o1, MLA_V, axis=1))
            o_ref[0, :, hp * HEAD_PAD:(hp + 1) * HEAD_PAD] = pair.astype(BF16)


def _flash(q, k, v, tq=512, tk=2048):
    B, H, S, _ = q.shape
    qspec = pl.BlockSpec((1, H, tq, HEAD_PAD), lambda b, i, j: (b, 0, i, 0))
    kspec = pl.BlockSpec((1, H, tk, HEAD_PAD), lambda b, i, j: (b, 0, j, 0))
    return pl.pallas_call(
        _flash_kernel,
        grid=(B, S // tq, S // tk),
        in_specs=[qspec, kspec, kspec],
        out_specs=pl.BlockSpec((1, tq, H * MLA_V), lambda b, i, j: (b, i, 0)),
        out_shape=jax.ShapeDtypeStruct((B, S, H * MLA_V), BF16),
        scratch_shapes=[pltpu.VMEM((H, tq, LANES), F32), pltpu.VMEM((H, tq, HEAD_PAD), F32)],
        compiler_params=_params("parallel", "parallel", "arbitrary"),
        name="mla_flash",
    )(q, k, v)


def _log_sigmoid(x):
    return jnp.minimum(x, 0.0) - jnp.log1p(jnp.exp(-jnp.abs(x)))


def _gla_scan(reverse, gq_ref, gk_ref, gv_ref, gd_ref, wd_ref, bd_ref, tri_ref, state_scr,
              o_scr):
    tm = gq_ref.shape[0]
    L = GLA_CHUNK
    n_chunks = tm // L

    @pl.when(pl.program_id(1) == 0)
    def _():
        state_scr[...] = jnp.zeros(state_scr.shape, F32)

    logits = _dot3(gd_ref[...], wd_ref[...]) + bd_ref[...]
    g = _log_sigmoid(logits) * (1.0 / GLA_GATE_NORM)
    g_hi, g_lo = _split(g)
    b = _dot(tri_ref[...], g_hi) + _dot(tri_ref[...], g_lo)
    edge = 0 if reverse else L - 1
    b_last = jnp.concatenate(
        [jnp.broadcast_to(b[c * L + edge:c * L + edge + 1, :], (L, GLA_W))
         for c in range(n_chunks)], axis=0)
    q_in = (gq_ref[...].astype(F32) * (GLA_DK ** -0.5) * jnp.exp(b)).astype(BF16)
    k_f = gk_ref[...].astype(F32)
    k_in = (k_f * jnp.exp(-b)).astype(BF16)
    k_st = (k_f * jnp.exp(b_last - b)).astype(BF16)
    dec = jnp.exp(b_last)
    v = gv_ref[...]
    lane = lax.broadcasted_iota(jnp.int32, (HEAD_PAD, 2 * L), 1)
    order = range(n_chunks - 1, -1, -1) if reverse else range(n_chunks)
    for h in range(GLA_HEADS):
        hs = slice(h * HEAD_PAD, (h + 1) * HEAD_PAD)
        a = _dot_nt(q_in[:, hs], k_in[:, hs]).astype(BF16) * tri_ref[...]
        o_scr[:, hs] = _dot(a, v[:, hs])
        v_t = v[:, hs].astype(F32).T
        st = state_scr[h]
        for c in order:
            rs = slice(c * L, (c + 1) * L)
            pair = slice((c // 2) * 2 * L, (c // 2 + 1) * 2 * L)
            o_scr[rs, hs] += _dot_nt(q_in[rs, hs], st.astype(BF16))
            in_chunk = (lane >= L) if c % 2 else (lane < L)
            v_tc = jnp.where(in_chunk, v_t[:, pair], 0.0).astype(BF16)
            st = dec[c * L:c * L + 1, hs] * st + _dot(v_tc, k_st[pair, hs])
        state_scr[h] = st


def _gla_fwd_kernel(gq_ref, gk_ref, gv_ref, gd_ref, wd_ref, bd_ref, tri_ref, o_ref, state_scr,
                    o_scr):
    _gla_scan(False, gq_ref, gk_ref, gv_ref, gd_ref, wd_ref, bd_ref, tri_ref, state_scr, o_scr)
    o_ref[...] = o_scr[...]


def _gla_bwd_kernel(gq_ref, gk_ref, gv_ref, gd_ref, wd_ref, bd_ref, tri_ref, of_ref, gr_ref,
                    ng_ref, out_ref, state_scr, o_scr):
    _gla_scan(True, gq_ref, gk_ref, gv_ref, gd_ref, wd_ref, bd_ref, tri_ref, state_scr, o_scr)
    r = gr_ref[...].astype(F32)
    gate = r * jax.nn.sigmoid(r)
    for h in range(GLA_HEADS):
        hs = slice(h * HEAD_PAD, (h + 1) * HEAD_PAD)
        o = _rms_norm(of_ref[:, hs] + o_scr[:, hs], ng_ref[...])
        out_ref[:, hs] = (o * gate[:, hs]).astype(BF16)


def _gla(gq, gk, gv, gd, gr, wd, bd, tri_f, tri_b, norm_g, B, S, tm=512):
    nb = S // tm
    T = B * S

    def call(reverse):
        blk = (lambda b, i: (b * nb + nb - 1 - i, 0)) if reverse else (lambda b, i: (b * nb + i, 0))
        rows = lambda w: pl.BlockSpec((tm, w), blk)
        const = lambda r, c: pl.BlockSpec((r, c), lambda b, i: (0, 0))
        in_specs = [rows(GLA_W), rows(GLA_W), rows(GLA_W), rows(LANES),
                    const(LANES, GLA_W), const(1, GLA_W), const(tm, tm)]
        scratch = [pltpu.VMEM((GLA_HEADS, GLA_DV, HEAD_PAD), F32), pltpu.VMEM((tm, GLA_W), F32)]
        d = 1 if reverse else 0
        args = [gq, gk, gv, gd, wd[d], bd[d], tri_b if reverse else tri_f]
        if reverse:
            in_specs += [rows(GLA_W), rows(GLA_W), const(1, GLA_DV)]
            args += [o_f, gr, norm_g]
        return pl.pallas_call(
            _gla_bwd_kernel if reverse else _gla_fwd_kernel,
            grid=(B, nb),
            in_specs=in_specs,
            out_specs=rows(GLA_W),
            out_shape=jax.ShapeDtypeStruct((T, GLA_W), BF16 if reverse else F32),
            scratch_shapes=scratch,
            compiler_params=_params("parallel", "arbitrary"),
            name="gla_bwd" if reverse else "gla_fwd",
        )(*args)

    o_f = call(False)
    return call(True)


def _merge_kernel(S, u_ref, up_ref, un_ref, om_ref, gla_ref, gl_ref, h_ref,
                  pw_ref, ps_ref, wa_ref, wb_ref, wc_ref, bg_ref, wo_ref, g_ref, b_ref,
                  rw_ref, ha_ref, afft_ref):
    tm = u_ref.shape[0]
    i = pl.program_id(1)
    ext = jnp.concatenate([up_ref[...], u_ref[...], un_ref[...]], axis=0).astype(F32)
    pos = i * tm + lax.broadcasted_iota(jnp.int32, (tm, LANES), 0)
    pooled = []
    for gi, w in enumerate(POOL_WINDOWS):
        hw = w // 2
        cs = slice(gi * LANES, (gi + 1) * LANES)
        acc = jnp.zeros((tm, LANES), F32)
        for d in range(-hw, hw):
            valid = (pos + d >= 0) & (pos + d < S)
            acc += jnp.where(valid, ext[POOL_HALO + d:POOL_HALO + d + tm, cs], 0.0)
        cnt = (jnp.minimum(pos + hw, S) - jnp.maximum(pos - hw, 0)).astype(F32)
        pg = (acc / cnt - ext[POOL_HALO:POOL_HALO + tm, cs]).astype(BF16)
        pooled.append(_dot(pg, pw_ref[gi]))
    pa = (jnp.concatenate(pooled, axis=1) * ps_ref[...]).astype(BF16)
    y_a = _dot(pa, wa_ref[...])
    y_b = _dot(om_ref[...], wb_ref[...])
    y_c = _dot(gla_ref[...], wc_ref[...])
    gates = jax.nn.sigmoid(gl_ref[...].astype(F32) + bg_ref[...])
    merged = (gates[:, :D_MODEL] * y_a + gates[:, D_MODEL:2 * D_MODEL] * y_b
              + gates[:, 2 * D_MODEL:] * y_c)
    mix = _dot(merged.astype(BF16), wo_ref[...])
    h1 = _layer_norm(DN_ALPHA * h_ref[...] + mix, g_ref[...], b_ref[...])
    lane = lax.broadcasted_iota(jnp.int32, (tm, LANES), 1)
    logits = jnp.where(lane < N_EXPERTS, _dot3(h1, rw_ref[...]), -jnp.inf)
    e = jnp.exp(logits - jnp.max(logits, axis=1, keepdims=True))
    aff = e / jnp.sum(e, axis=1, keepdims=True)
    afft_ref[0] = aff.T[:N_EXPERTS]
    ha_ref[:, :D_MODEL] = DN_ALPHA * h1
    ha_ref[:, D_MODEL:2 * D_MODEL] = h1
    ha_ref[:, 2 * D_MODEL:] = aff


def _merge(u, o_mla, gla_o, gl, h, pw, ps, wa, wb, wc, bg, wo, g, b, rw, B, S, tm=512):
    nb = S // tm
    T = B * S
    hb = tm // POOL_HALO
    n_halo = T // POOL_HALO
    rows = lambda w: pl.BlockSpec((tm, w), lambda bb, i: (bb * nb + i, 0))
    prev = pl.BlockSpec((POOL_HALO, POOL_DIM),
                        lambda bb, i: (jnp.maximum((bb * nb + i) * hb - 1, 0), 0))
    nxt = pl.BlockSpec((POOL_HALO, POOL_DIM),
                       lambda bb, i: (jnp.minimum((bb * nb + i + 1) * hb, n_halo - 1), 0))
    c2 = lambda r, c: pl.BlockSpec((r, c), lambda bb, i: (0, 0))
    c3 = lambda a, r, c: pl.BlockSpec((a, r, c), lambda bb, i: (0, 0, 0))
    return pl.pallas_call(
        functools.partial(_merge_kernel, S),
        grid=(B, nb),
        in_specs=[rows(POOL_DIM), prev, nxt,
                  rows(MLA_HEADS * MLA_V), rows(GLA_W), rows(3 * D_MODEL), rows(D_MODEL),
                  c3(len(POOL_WINDOWS), LANES, LANES), c2(1, POOL_DIM), c2(POOL_DIM, D_MODEL),
                  c2(MLA_HEADS * MLA_V, D_MODEL), c2(GLA_W, D_MODEL), c2(1, 3 * D_MODEL),
                  c2(D_MODEL, D_MODEL), c2(1, D_MODEL), c2(1, D_MODEL), c2(D_MODEL, LANES)],
        out_specs=[rows(HA_W),
                   pl.BlockSpec((1, N_EXPERTS, tm), lambda bb, i: (bb, 0, i))],
        out_shape=[jax.ShapeDtypeStruct((T, HA_W), F32),
                   jax.ShapeDtypeStruct((B, N_EXPERTS, S), F32)],
        compiler_params=_params("parallel", "parallel"),
        name="merge",
    )(u, u, u, o_mla, gla_o, gl, h, pw, ps, wa, wb, wc, bg, wo, g, b, rw)


def _topk_kernel(C, aff_ref, tri_ref, idx_ref, p_scr):
    S = aff_ref.shape[2]
    n_chunks = S // LANES
    aff = aff_ref[0]

    def count(mask):
        return jnp.sum(jnp.where(mask, 1.0, 0.0), axis=1, keepdims=True)

    def as_float(bits):
        return lax.bitcast_convert_type(bits, F32)

    def refine(i, thr):
        cand = thr | jnp.left_shift(jnp.int32(1), 30 - i)
        return jnp.where(count(aff >= as_float(cand)) >= C, cand, thr)

    thr = lax.fori_loop(0, 31, refine, jnp.zeros((N_EXPERTS, 1), jnp.int32))
    above = aff >= as_float(thr + 1)
    tied = (aff >= as_float(thr)) & jnp.logical_not(above)
    need = C - count(above)
    tri = tri_ref[...]

    tied_f = tied.astype(F32)
    run = jnp.zeros((N_EXPERTS, 1), F32)
    sel_parts = []
    for c in range(n_chunks):
        cs = slice(c * LANES, (c + 1) * LANES)
        incl = _dot(tied_f[:, cs].astype(BF16), tri) + run
        run = incl[:, LANES - 1:LANES]
        sel_parts.append(jnp.where(above[:, cs] | (tied[:, cs] & (incl <= need)), 1.0, 0.0))

    p_scr[...] = jnp.zeros(p_scr.shape, F32)
    lane_e = lax.broadcasted_iota(jnp.int32, (N_EXPERTS, LANES), 1)
    chunk_end = jnp.full((N_EXPERTS, LANES), NEVER, F32)
    run = jnp.zeros((N_EXPERTS, 1), F32)
    for c in range(n_chunks):
        rel = _dot(sel_parts[c].astype(BF16), tri)
        p_scr[:, c, :] = rel
        run = run + rel[:, LANES - 1:LANES]
        chunk_end = jnp.where(lane_e == c, run, chunk_end)

    slot = lax.broadcasted_iota(jnp.int32, (C, LANES), 0).astype(F32)
    lane_c = lax.broadcasted_iota(jnp.int32, (C, LANES), 1).astype(F32)
    ones = jnp.ones((8, LANES), BF16)
    for e in range(N_EXPERTS):
        ce = chunk_end[e:e + 1, :]
        full = ce <= slot
        n_full = jnp.sum(jnp.where(full, 1.0, 0.0), axis=1, keepdims=True)
        base = jnp.max(jnp.where(full, ce, 0.0), axis=1, keepdims=True)
        pick = jnp.where(lane_c == n_full, 1.0, 0.0).astype(BF16)
        rel = _dot(pick, p_scr[e].astype(BF16))
        w = jnp.where(rel <= slot - base, 1.0, 0.0) + jnp.where(full, float(LANES), 0.0)
        tok = _dot_nt(ones, w.astype(BF16))[0:1, :]
        idx_ref[0, e:e + 1, :] = tok.astype(jnp.int32)


def _topk(afft, tri, C):
    B, E, S = afft.shape
    assert S // LANES <= LANES
    return pl.pallas_call(
        functools.partial(_topk_kernel, C),
        grid=(B,),
        in_specs=[pl.BlockSpec((1, E, S), lambda b: (b, 0, 0)),
                  pl.BlockSpec((LANES, LANES), lambda b: (0, 0))],
        out_specs=pl.BlockSpec((1, E, C), lambda b: (b, 0, 0)),
        out_shape=jax.ShapeDtypeStruct((B, E, C), jnp.int32),
        scratch_shapes=[pltpu.VMEM((E, LANES, LANES), F32)],
        compiler_params=_params("parallel"),
        name="expert_choice",
    )(afft, tri)


def _ffn_kernel(tc, n_steps, idx_ref, ha_in_ref, wg_ref, wu_ref, wd_ref, ha_ref,
                gbuf, sbuf, gsem, ssem):
    del ha_in_ref
    e = pl.program_id(0)
    s = (e * pl.num_programs(1) + pl.program_id(1)) * pl.num_programs(2) + pl.program_id(2)
    slot = s % 2

    def gather_copy(step, r, sl):
        row = idx_ref[step * tc + r]
        return pltpu.make_async_copy(ha_ref.at[pl.ds(row, 1), :],
                                     gbuf.at[sl, pl.ds(r, 1), :], gsem.at[sl])

    def scatter_copy(step, r, sl):
        row = idx_ref[step * tc + r]
        return pltpu.make_async_copy(sbuf.at[sl, pl.ds(r, 1), :],
                                     ha_ref.at[pl.ds(row, 1), pl.ds(0, D_MODEL)], ssem.at[sl])

    def start_rows(copy, step, sl):
        def body(r, carry):
            copy(step, r, sl).start()
            return carry
        lax.fori_loop(0, tc, body, 0, unroll=DMA_UNROLL)

    def wait_gather(sl):
        pltpu.make_async_copy(ha_ref.at[pl.ds(0, tc), :], gbuf.at[sl], gsem.at[sl]).wait()

    def wait_scatter(sl):
        pltpu.make_async_copy(sbuf.at[sl], ha_ref.at[pl.ds(0, tc), pl.ds(0, D_MODEL)],
                              ssem.at[sl]).wait()

    @pl.when(s == 0)
    def _():
        start_rows(gather_copy, s, slot)

    @pl.when(s >= 2)
    def _():
        wait_scatter(slot)

    @pl.when(s + 1 < n_steps)
    def _():
        start_rows(gather_copy, s + 1, 1 - slot)

    wait_gather(slot)

    rows = gbuf[slot]
    x = rows[:, D_MODEL:2 * D_MODEL].astype(BF16)
    lane = lax.broadcasted_iota(jnp.int32, (tc, LANES), 1)
    gate = jnp.sum(jnp.where(lane == e, rows[:, 2 * D_MODEL:], 0.0), axis=1, keepdims=True)
    y = jnp.zeros((tc, D_MODEL), F32)
    fh = D_EXPERT // 2
    for f in range(2):
        fs = slice(f * fh, (f + 1) * fh)
        hg = _dot(x, wg_ref[0, 0, :, fs])
        hu = _dot(x, wu_ref[0, 0, :, fs])
        hid = (hg * jax.nn.sigmoid(hg) * hu).astype(BF16)
        y += _dot(hid, wd_ref[0, 0, fs, :])
    sbuf[slot] = rows[:, :D_MODEL] + gate * y
    start_rows(scatter_copy, s, slot)

    @pl.when(s == n_steps - 1)
    def _():
        if n_steps >= 2:
            wait_scatter(1 - slot)
        wait_scatter(slot)


def _ffn(ha, idx_steps, wg, wu, wd, layer, B, C, tc=512):
    nblk = C // tc
    n_steps = N_EXPERTS * B * nblk
    wspec = lambda r, c: pl.BlockSpec((1, 1, r, c), lambda e, b, j, idx: (layer, e, 0, 0))
    return pl.pallas_call(
        functools.partial(_ffn_kernel, tc, n_steps),
        grid_spec=pltpu.PrefetchScalarGridSpec(
            num_scalar_prefetch=1,
            grid=(N_EXPERTS, B, nblk),
            in_specs=[pl.BlockSpec(memory_space=pl.ANY),
                      wspec(D_MODEL, D_EXPERT), wspec(D_MODEL, D_EXPERT),
                      wspec(D_EXPERT, D_MODEL)],
            out_specs=pl.BlockSpec(memory_space=pl.ANY),
            scratch_shapes=[pltpu.VMEM((2, tc, HA_W), F32), pltpu.VMEM((2, tc, D_MODEL), F32),
                            pltpu.SemaphoreType.DMA((2,)), pltpu.SemaphoreType.DMA((2,))]),
        out_shape=jax.ShapeDtypeStruct(ha.shape, F32),
        input_output_aliases={1: 0},
        compiler_params=_params("arbitrary", "arbitrary", "arbitrary"),
        name="expert_ffn",
    )(idx_steps, ha, wg, wu, wd)


def _ln_kernel(x_ref, g_ref, b_ref, o_ref):
    o_ref[...] = _layer_norm(x_ref[...], g_ref[...], b_ref[...])


def _ln_rows(xin, g, b, tm=512):
    T = xin.shape[0]
    const = lambda i: (0, 0)
    return pl.pallas_call(
        _ln_kernel,
        grid=(T // tm,),
        in_specs=[pl.BlockSpec((tm, D_MODEL), lambda i: (i, 0)),
                  pl.BlockSpec((1, D_MODEL), const), pl.BlockSpec((1, D_MODEL), const)],
        out_specs=pl.BlockSpec((tm, D_MODEL), lambda i: (i, 0)),
        out_shape=jax.ShapeDtypeStruct((T, D_MODEL), F32),
        compiler_params=_params("parallel"),
        name="final_ln",
    )(xin, g, b)


def _pad_heads(w, heads, width):
    lead = w.shape[:-1]
    w = w.reshape(lead + (heads, width))
    w = jnp.pad(w, [(0, 0)] * len(lead) + [(0, 0), (0, HEAD_PAD - width)])
    return w.reshape(lead + (heads * HEAD_PAD,))


def _rot_half(w):
    half = w.shape[-1] // 2
    return jnp.concatenate([-w[..., half:], w[..., :half]], axis=-1)


def _pack_w_in(w):
    o = 0
    parts = {}
    for name, width in (("pool", POOL_DIM), ("cq", MLA_Q_RANK), ("ckv", MLA_KV_RANK),
                        ("kr", MLA_ROPE), ("gq", GLA_HEADS * GLA_DK), ("gk", GLA_HEADS * GLA_DK),
                        ("gv", GLA_HEADS * GLA_DV), ("gr", GLA_HEADS * GLA_DV),
                        ("gd", 2 * GLA_GATE_RANK), ("gates", 3 * D_MODEL)):
        parts[name] = w[:, o:o + width]
        o += width
    rope_slot = lambda m: jnp.pad(m, ((0, 0), (MLA_NOPE, HEAD_PAD - MLA_NOPE - MLA_ROPE)))
    kr2 = jnp.concatenate([rope_slot(parts["kr"]), rope_slot(_rot_half(parts["kr"]))], axis=1)
    gd = jnp.pad(parts["gd"], ((0, 0), (0, LANES - 2 * GLA_GATE_RANK)))
    cols = [parts["pool"], parts["cq"], parts["ckv"], kr2,
            _pad_heads(parts["gq"], GLA_HEADS, GLA_DK), _pad_heads(parts["gk"], GLA_HEADS, GLA_DK),
            parts["gv"], parts["gr"], gd, parts["gates"]]
    return jnp.concatenate(cols, axis=1).astype(BF16)


def _pack_mla(w_uq, w_ukv):
    r = w_uq.shape[0]
    uq = w_uq.reshape(r, MLA_HEADS, MLA_NOPE + MLA_ROPE)
    nope, rope = uq[..., :MLA_NOPE], uq[..., MLA_NOPE:]
    tail = jnp.zeros((r, MLA_HEADS, HEAD_PAD - MLA_NOPE - MLA_ROPE), F32)
    wq = jnp.concatenate([nope, rope, tail], axis=-1).reshape(r, -1)
    wq2 = jnp.concatenate([jnp.zeros_like(nope), _rot_half(rope), tail], axis=-1).reshape(r, -1)
    rk = w_ukv.shape[0]
    ukv = w_ukv.reshape(rk, MLA_HEADS, MLA_NOPE + MLA_V)
    wk = _pad_heads(ukv[..., :MLA_NOPE].reshape(rk, -1), MLA_HEADS, MLA_NOPE)
    wv = _pad_heads(ukv[..., MLA_NOPE:].reshape(rk, -1), MLA_HEADS, MLA_V)
    return wq.astype(BF16), wq2.astype(BF16), wk.astype(BF16), wv.astype(BF16)


def _pack_gla_decay(w_dec, b_dec):
    wd = []
    for d in range(2):
        rows = _pad_heads(w_dec[d], GLA_HEADS, GLA_DK)
        wd.append(jnp.pad(rows, ((d * GLA_GATE_RANK, LANES - (d + 1) * GLA_GATE_RANK), (0, 0))))
    bd = [_pad_heads(b_dec[d][None, :], GLA_HEADS, GLA_DK) for d in range(2)]
    return wd, bd


def _chunk_tri(tm, reverse):
    r = jnp.arange(tm)[:, None]
    c = jnp.arange(tm)[None, :]
    same = (r // GLA_CHUNK) == (c // GLA_CHUNK)
    return (same & ((c >= r) if reverse else (c <= r))).astype(BF16)


def kernel(x, positions, ln0_g, ln0_b, w_in, b_gate, pool_w, pool_scale, w_up_a, mla_q_norm,
           mla_w_uq, mla_kv_norm, mla_w_ukv, w_up_b, gla_w_dec, gla_b_dec, gla_norm, w_up_c,
           w_out, ln1_g, ln1_b, router_w, exp_w_gate, exp_w_up, exp_w_down, ln2_g, ln2_b):
    B, S, D = x.shape
    assert D == D_MODEL and S % 512 == 0 and B == 2
    T = B * S
    C = CAPACITY_FACTOR * S // N_EXPERTS
    tc = min(512, C)
    row = lambda v: v.reshape(1, -1).astype(F32)

    half = MLA_ROPE // 2
    freqs = ROPE_THETA ** (-jnp.arange(half, dtype=F32) / half)
    lanes = jnp.arange(LANES)
    in_rope = (lanes >= MLA_NOPE) & (lanes < MLA_NOPE + MLA_ROPE)
    freq_row = jnp.where(in_rope, freqs[(lanes - MLA_NOPE) % half], 0.0).reshape(1, LANES)
    rope_mask = in_rope.astype(F32).reshape(1, LANES)
    nope_row = (lanes < MLA_NOPE).astype(F32).reshape(1, LANES)
    one_row = (lanes == MLA_V).astype(F32).reshape(1, LANES)
    cos_t, sin_t = _rope_tables(positions.reshape(T, 1), freq_row, rope_mask)

    tri_f, tri_b = _chunk_tri(512, False), _chunk_tri(512, True)
    tri_lane = (jnp.arange(LANES)[:, None] <= jnp.arange(LANES)[None, :]).astype(BF16)

    wg_all, wu_all, wd_all = (w.astype(BF16) for w in (exp_w_gate, exp_w_up, exp_w_down))
    stream, g_in, b_in = x.reshape(T, D), ln0_g, ln0_b
    for l in range(DEPTH):
        h, u, cq, ckv, kr2, gq, gk, gv, gr, gd, gl = _ln_inproj(
            stream, row(g_in), row(b_in), _pack_w_in(w_in[l]))
        wq, wq2, wk, wv = _pack_mla(mla_w_uq[l], mla_w_ukv[l])
        q, k, v = _mla_prep(cq, ckv, kr2, cos_t, sin_t, row(mla_q_norm[l]), row(mla_kv_norm[l]),
                            wq, wq2, wk, wv, nope_row, one_row, B, S)
        o_mla = _flash(q, k, v).reshape(T, MLA_HEADS * MLA_V)
        wd, bd = _pack_gla_decay(gla_w_dec[l], gla_b_dec[l])
        gla_o = _gla(gq, gk, gv, gd, gr, wd, bd, tri_f, tri_b, row(gla_norm[l]), B, S)
        rw = jnp.pad(router_w[l], ((0, 0), (0, LANES - N_EXPERTS)))
        ha, afft = _merge(u, o_mla, gla_o, gl, h, pool_w[l].astype(BF16), row(pool_scale[l]),
                          w_up_a[l].astype(BF16), w_up_b[l].astype(BF16),
                          w_up_c[l].astype(BF16), row(b_gate[l]),
                          w_out[l].astype(BF16), row(ln1_g[l]), row(ln1_b[l]), rw, B, S)
        idx = _topk(afft, tri_lane, C)
        idx_steps = (idx + (jnp.arange(B, dtype=jnp.int32) * S)[:, None, None])
        idx_steps = idx_steps.transpose(1, 0, 2).reshape(-1)
        stream = _ffn(ha, idx_steps, wg_all, wu_all, wd_all, l, B, C, tc)
        g_in, b_in = ln2_g[l], ln2_b[l]
    out = _ln_rows(stream, row(g_in), row(b_in))
    return out.reshape(B, S, D)
```

```python
import functools

import jax
import jax.numpy as jnp
from jax import lax
from jax.experimental import pallas as pl
from jax.experimental.pallas import tpu as pltpu

F32 = jnp.float32
BF16 = jnp.bfloat16

LANES = 128
VMEM_LIMIT = 56 * 1024 * 1024

D_MODEL = 1024
DEPTH = 2
POOL_WINDOWS = (2, 4, 8, 16)
POOL_DIM = 512
POOL_HALO = 16
MLA_HEADS = 8
MLA_NOPE = 64
MLA_ROPE = 32
MLA_V = 64
MLA_Q_RANK = 384
MLA_KV_RANK = 256
ROPE_THETA = 10000.0
GLA_HEADS = 4
GLA_DK = 64
GLA_DV = 128
GLA_GATE_RANK = 16
GLA_GATE_NORM = 16.0
GLA_CHUNK = 64
N_EXPERTS = 16
CAPACITY_FACTOR = 2
D_EXPERT = 2048
DN_ALPHA = (2 * DEPTH) ** 0.25
LN_EPS = 1e-5
RMS_EPS = 1e-6
LOG2_E = 1.4426950408889634
NEVER = 1e9

HEAD_PAD = LANES
GLA_W = GLA_HEADS * HEAD_PAD
HA_W = 2 * D_MODEL + LANES
DMA_UNROLL = 16

_SEG = (("pool", POOL_DIM), ("cq", MLA_Q_RANK), ("ckv", MLA_KV_RANK), ("kr2", 2 * HEAD_PAD),
        ("gq", GLA_W), ("gk", GLA_W), ("gv", GLA_W), ("gr", GLA_W), ("gd", LANES),
        ("gates", 3 * D_MODEL))
_SEG_OFF = {}
_off = 0
for _n, _w in _SEG:
    _SEG_OFF[_n] = (_off, _w)
    _off += _w
W_ALL_COLS = _off


def _params(*sem):
    return pltpu.CompilerParams(dimension_semantics=sem, vmem_limit_bytes=VMEM_LIMIT)


def _dot(a, b):
    return jnp.dot(a, b, preferred_element_type=F32)


def _dot_nt(a, b):
    return lax.dot_general(a, b, (((1,), (1,)), ((), ())), preferred_element_type=F32)


def _split(x):
    hi = x.astype(BF16)
    lo = (x - hi.astype(F32)).astype(BF16)
    return hi, lo


def _dot3(a, b, nt=False):
    d = _dot_nt if nt else _dot
    ah, al = _split(a)
    bh, bl = _split(b)
    return d(ah, bh) + d(ah, bl) + d(al, bh)


def _layer_norm(x, g, b):
    mu = jnp.mean(x, axis=-1, keepdims=True)
    xc = x - mu
    var = jnp.mean(xc * xc, axis=-1, keepdims=True)
    return xc * lax.rsqrt(var + LN_EPS) * g + b


def _rms_norm(x, g):
    ms = jnp.mean(x * x, axis=-1, keepdims=True)
    return x * lax.rsqrt(ms + RMS_EPS) * g


def _rope_kernel(pos_ref, freq_ref, mask_ref, cos_ref, sin_ref):
    ang = pos_ref[...].astype(F32) * freq_ref[...]
    cos_ref[...] = jnp.cos(ang) * mask_ref[...]
    sin_ref[...] = jnp.sin(ang) * mask_ref[...]


def _rope_tables(pos_col, freq_row, mask_row, tm=512):
    T = pos_col.shape[0]
    row = pl.BlockSpec((1, LANES), lambda i: (0, 0))
    out = pl.BlockSpec((tm, LANES), lambda i: (i, 0))
    return pl.pallas_call(
        _rope_kernel,
        grid=(T // tm,),
        in_specs=[pl.BlockSpec((tm, 1), lambda i: (i, 0)), row, row],
        out_specs=[out, out],
        out_shape=[jax.ShapeDtypeStruct((T, LANES), F32)] * 2,
        compiler_params=_params("parallel"),
        name="rope_tables",
    )(pos_col, freq_row, mask_row)


def _inproj_kernel(x_ref, g_ref, b_ref, w_ref, h_ref, *out_refs):
    h = _layer_norm(x_ref[...], g_ref[...], b_ref[...])
    h_ref[...] = h
    hb = h.astype(BF16)
    for (name, width), o_ref in zip(_SEG, out_refs):
        off = _SEG_OFF[name][0]
        o_ref[...] = _dot(hb, w_ref[:, off:off + width]).astype(o_ref.dtype)


def _ln_inproj(xin, g, b, w_all, tm=256):
    T = xin.shape[0]
    const = lambda i: (0, 0)
    out_shapes = [jax.ShapeDtypeStruct((T, D_MODEL), F32)]
    out_specs = [pl.BlockSpec((tm, D_MODEL), lambda i: (i, 0))]
    for name, width in _SEG:
        out_shapes.append(jax.ShapeDtypeStruct((T, width), F32 if name == "gd" else BF16))
        out_specs.append(pl.BlockSpec((tm, width), lambda i: (i, 0)))
    return pl.pallas_call(
        _inproj_kernel,
        grid=(T // tm,),
        in_specs=[pl.BlockSpec((tm, D_MODEL), lambda i: (i, 0)),
                  pl.BlockSpec((1, D_MODEL), const), pl.BlockSpec((1, D_MODEL), const),
                  pl.BlockSpec((D_MODEL, W_ALL_COLS), const)],
        out_specs=out_specs,
        out_shape=out_shapes,
        compiler_params=_params("parallel"),
        name="ln_inproj",
    )(xin, g, b, w_all)


def _mla_prep_kernel(cq_ref, ckv_ref, kr2_ref, cos_ref, sin_ref, qn_ref, kvn_ref,
                     wq_ref, wq2_ref, wk_ref, wv_ref, nope_ref, one_ref,
                     q_ref, k_ref, v_ref):
    scale = (MLA_NOPE + MLA_ROPE) ** -0.5 * LOG2_E
    cosr = cos_ref[...]
    sinr = sin_ref[...]
    nq = _rms_norm(cq_ref[...].astype(F32), qn_ref[...]).astype(BF16)
    nkv = _rms_norm(ckv_ref[...].astype(F32), kvn_ref[...]).astype(BF16)
    qa = _dot(nq, wq_ref[...])
    qb = _dot(nq, wq2_ref[...])
    ka = _dot(nkv, wk_ref[...])
    va = _dot(nkv, wv_ref[...])
    kr2 = kr2_ref[...].astype(F32)
    k_rope = kr2[:, :HEAD_PAD] * cosr + kr2[:, HEAD_PAD:] * sinr
    q_cos = (cosr + nope_ref[...]) * scale
    q_sin = sinr * scale
    for h in range(MLA_HEADS):
        sl = slice(h * HEAD_PAD, (h + 1) * HEAD_PAD)
        q_ref[0, h] = (qa[:, sl] * q_cos + qb[:, sl] * q_sin).astype(BF16)
        k_ref[0, h] = (ka[:, sl] + k_rope).astype(BF16)
        v_ref[0, h] = (va[:, sl] + one_ref[...]).astype(BF16)


def _mla_prep(cq, ckv, kr2, cos_t, sin_t, qn, kvn, wq, wq2, wk, wv, nope_row, one_row, B, S,
              tm=512):
    nb = S // tm
    rows = lambda w: pl.BlockSpec((tm, w), lambda b, i: (b * nb + i, 0))
    const = lambda r, c: pl.BlockSpec((r, c), lambda b, i: (0, 0))
    hw = MLA_HEADS * HEAD_PAD
    out = pl.BlockSpec((1, MLA_HEADS, tm, HEAD_PAD), lambda b, i: (b, 0, i, 0))
    return pl.pallas_call(
        _mla_prep_kernel,
        grid=(B, nb),
        in_specs=[rows(MLA_Q_RANK), rows(MLA_KV_RANK), rows(2 * HEAD_PAD), rows(LANES),
                  rows(LANES), const(1, MLA_Q_RANK), const(1, MLA_KV_RANK),
                  const(MLA_Q_RANK, hw), const(MLA_Q_RANK, hw), const(MLA_KV_RANK, hw),
                  const(MLA_KV_RANK, hw), const(1, LANES), const(1, LANES)],
        out_specs=[out, out, out],
        out_shape=[jax.ShapeDtypeStruct((B, MLA_HEADS, S, HEAD_PAD), BF16)] * 3,
        compiler_params=_params("parallel", "parallel"),
        name="mla_prep",
    )(cq, ckv, kr2, cos_t, sin_t, qn, kvn, wq, wq2, wk, wv, nope_row, one_row)


def _flash_kernel(q_ref, k_ref, v_ref, o_ref, m_scr, acc_scr):
    kv = pl.program_id(2)

    @pl.when(kv == 0)
    def _():
        m_scr[...] = jnp.full(m_scr.shape, -jnp.inf, F32)
        acc_scr[...] = jnp.zeros(acc_scr.shape, F32)

    tk = k_ref.shape[2]
    for h in range(MLA_HEADS):
        s = _dot_nt(q_ref[0, h], k_ref[0, h])
        m_prev = m_scr[h]
        m_new = jnp.maximum(m_prev, jnp.max(s, axis=1, keepdims=True))
        p = jnp.exp2(s - jnp.concatenate([m_new] * (tk // LANES), axis=1))
        alpha = jnp.exp2(m_prev - m_new)
        acc_scr[h] = alpha * acc_scr[h] + _dot(p.astype(BF16), v_ref[0, h])
        m_scr[h] = m_new

    @pl.when(kv == pl.num_programs(2) - 1)
    def _():
        lane = lax.broadcasted_iota(jnp.int32, (q_ref.shape[2], HEAD_PAD), 1)
        for hp in range(MLA_HEADS // 2):
            a0 = acc_scr[2 * hp]
            a1 = acc_scr[2 * hp + 1]
            o0 = a0 / a0[:, MLA_V:MLA_V + 1]
            o1 = a1 / a1[:, MLA_V:MLA_V + 1]
            pair = jnp.where(lane < MLA_V, o0, pltpu.roll(o1, MLA_V, axis=1))
            o_ref[0, :, hp * HEAD_PAD:(hp + 1) * HEAD_PAD] = pair.astype(BF16)


def _flash(q, k, v, tq=512, tk=2048):
    B, H, S, _ = q.shape
    tk = min(tk, S)
    assert S % tq == 0 and S % tk == 0
    qspec = pl.BlockSpec((1, H, tq, HEAD_PAD), lambda b, i, j: (b, 0, i, 0))
    kspec = pl.BlockSpec((1, H, tk, HEAD_PAD), lambda b, i, j: (b, 0, j, 0))
    return pl.pallas_call(
        _flash_kernel,
        grid=(B, S // tq, S // tk),
        in_specs=[qspec, kspec, kspec],
        out_specs=pl.BlockSpec((1, tq, H * MLA_V), lambda b, i, j: (b, i, 0)),
        out_shape=jax.ShapeDtypeStruct((B, S, H * MLA_V), BF16),
        scratch_shapes=[pltpu.VMEM((H, tq, LANES), F32), pltpu.VMEM((H, tq, HEAD_PAD), F32)],
        compiler_params=_params("parallel", "parallel", "arbitrary"),
        name="mla_flash",
    )(q, k, v)


def _log_sigmoid(x):
    return jnp.minimum(x, 0.0) - jnp.log1p(jnp.exp(-jnp.abs(x)))


def _gla_scan(reverse, gq_ref, gk_ref, gv_ref, gd_ref, wd_ref, bd_ref, tri_ref, state_scr,
              o_scr):
    tm = gq_ref.shape[0]
    L = GLA_CHUNK
    n_chunks = tm // L

    @pl.when(pl.program_id(1) == 0)
    def _():
        state_scr[...] = jnp.zeros(state_scr.shape, F32)

    logits = _dot3(gd_ref[...], wd_ref[...]) + bd_ref[...]
    g = _log_sigmoid(logits) * (1.0 / GLA_GATE_NORM)
    g_hi, g_lo = _split(g)
    b = _dot(tri_ref[...], g_hi) + _dot(tri_ref[...], g_lo)
    edge = 0 if reverse else L - 1
    b_last = jnp.concatenate(
        [jnp.broadcast_to(b[c * L + edge:c * L + edge + 1, :], (L, GLA_W))
         for c in range(n_chunks)], axis=0)
    q_in = (gq_ref[...].astype(F32) * (GLA_DK ** -0.5) * jnp.exp(b)).astype(BF16)
    k_f = gk_ref[...].astype(F32)
    k_in = (k_f * jnp.exp(-b)).astype(BF16)
    k_st = (k_f * jnp.exp(b_last - b)).astype(BF16)
    dec = jnp.exp(b_last)
    v = gv_ref[...]
    lane = lax.broadcasted_iota(jnp.int32, (HEAD_PAD, 2 * L), 1)
    order = range(n_chunks - 1, -1, -1) if reverse else range(n_chunks)
    heads = [slice(h * HEAD_PAD, (h + 1) * HEAD_PAD) for h in range(GLA_HEADS)]
    kv = {}
    for h, hs in enumerate(heads):
        a = _dot_nt(q_in[:, hs], k_in[:, hs]).astype(BF16) * tri_ref[...]
        o_scr[:, hs] = _dot(a, v[:, hs])
        v_t = v[:, hs].astype(F32).T
        for c in range(n_chunks):
            pair = slice((c // 2) * 2 * L, (c // 2 + 1) * 2 * L)
            in_chunk = (lane >= L) if c % 2 else (lane < L)
            v_tc = jnp.where(in_chunk, v_t[:, pair], 0.0).astype(BF16)
            kv[c, h] = _dot(v_tc, k_st[pair, hs])
    entering = {}
    for h, hs in enumerate(heads):
        st = state_scr[h]
        for c in order:
            entering[c, h] = st.astype(BF16)
            st = dec[c * L:c * L + 1, hs] * st + kv[c, h]
        state_scr[h] = st
    for c in range(n_chunks):
        rs = slice(c * L, (c + 1) * L)
        for h, hs in enumerate(heads):
            o_scr[rs, hs] += _dot_nt(q_in[rs, hs], entering[c, h])


def _gla_fwd_kernel(gq_ref, gk_ref, gv_ref, gd_ref, wd_ref, bd_ref, tri_ref, o_ref, state_scr,
                    o_scr):
    _gla_scan(False, gq_ref, gk_ref, gv_ref, gd_ref, wd_ref, bd_ref, tri_ref, state_scr, o_scr)
    o_ref[...] = o_scr[...]


def _gla_bwd_kernel(gq_ref, gk_ref, gv_ref, gd_ref, wd_ref, bd_ref, tri_ref, of_ref, gr_ref,
                    ng_ref, out_ref, state_scr, o_scr):
    _gla_scan(True, gq_ref, gk_ref, gv_ref, gd_ref, wd_ref, bd_ref, tri_ref, state_scr, o_scr)
    r = gr_ref[...].astype(F32)
    gate = r * jax.nn.sigmoid(r)
    for h in range(GLA_HEADS):
        hs = slice(h * HEAD_PAD, (h + 1) * HEAD_PAD)
        o = _rms_norm(of_ref[:, hs] + o_scr[:, hs], ng_ref[...])
        out_ref[:, hs] = (o * gate[:, hs]).astype(BF16)


def _gla(gq, gk, gv, gd, gr, wd, bd, tri_f, tri_b, norm_g, B, S, tm=512):
    nb = S // tm
    T = B * S

    def call(reverse):
        blk = (lambda b, i: (b * nb + nb - 1 - i, 0)) if reverse else (lambda b, i: (b * nb + i, 0))
        rows = lambda w: pl.BlockSpec((tm, w), blk)
        const = lambda r, c: pl.BlockSpec((r, c), lambda b, i: (0, 0))
        in_specs = [rows(GLA_W), rows(GLA_W), rows(GLA_W), rows(LANES),
                    const(LANES, GLA_W), const(1, GLA_W), const(tm, tm)]
        scratch = [pltpu.VMEM((GLA_HEADS, GLA_DV, HEAD_PAD), F32), pltpu.VMEM((tm, GLA_W), F32)]
        d = 1 if reverse else 0
        args = [gq, gk, gv, gd, wd[d], bd[d], tri_b if reverse else tri_f]
        if reverse:
            in_specs += [rows(GLA_W), rows(GLA_W), const(1, GLA_DV)]
            args += [o_f, gr, norm_g]
        return pl.pallas_call(
            _gla_bwd_kernel if reverse else _gla_fwd_kernel,
            grid=(B, nb),
            in_specs=in_specs,
            out_specs=rows(GLA_W),
            out_shape=jax.ShapeDtypeStruct((T, GLA_W), BF16 if reverse else F32),
            scratch_shapes=scratch,
            compiler_params=_params("parallel", "arbitrary"),
            name="gla_bwd" if reverse else "gla_fwd",
        )(*args)

    o_f = call(False)
    return call(True)


def _merge_kernel(S, u_ref, up_ref, un_ref, om_ref, gla_ref, gl_ref, h_ref,
                  pw_ref, ps_ref, wa_ref, wb_ref, wc_ref, bg_ref, wo_ref, g_ref, b_ref,
                  rw_ref, ha_ref, afft_ref):
    tm = u_ref.shape[0]
    i = pl.program_id(1)
    ext = jnp.concatenate([up_ref[...], u_ref[...], un_ref[...]], axis=0).astype(F32)
    n_ext = tm + 2 * POOL_HALO
    pos_ext = i * tm - POOL_HALO + lax.broadcasted_iota(jnp.int32, (n_ext, LANES), 0)
    in_seq = (pos_ext >= 0) & (pos_ext < S)
    pos = i * tm + lax.broadcasted_iota(jnp.int32, (tm, LANES), 0)
    core = slice(POOL_HALO, POOL_HALO + tm)

    def rows_at(a, d):
        return pltpu.roll(a, (-d) % n_ext, axis=0)

    pooled = []
    for gi, w in enumerate(POOL_WINDOWS):
        hw = w // 2
        cs = slice(gi * LANES, (gi + 1) * LANES)
        x = jnp.where(in_seq, ext[:, cs], 0.0)
        win = rows_at(x, -1) + x
        reach = 1
        while reach < hw:
            win = rows_at(win, -reach) + rows_at(win, reach)
            reach *= 2
        cnt = (jnp.minimum(pos + hw, S) - jnp.maximum(pos - hw, 0)).astype(F32)
        pg = (win[core] / cnt - x[core]).astype(BF16)
        pooled.append(_dot(pg, pw_ref[gi]))
    pa = (jnp.concatenate(pooled, axis=1) * ps_ref[...]).astype(BF16)
    y_a = _dot(pa, wa_ref[...])
    y_b = _dot(om_ref[...], wb_ref[...])
    y_c = _dot(gla_ref[...], wc_ref[...])
    gates = jax.nn.sigmoid(gl_ref[...].astype(F32) + bg_ref[...])
    merged = (gates[:, :D_MODEL] * y_a + gates[:, D_MODEL:2 * D_MODEL] * y_b
              + gates[:, 2 * D_MODEL:] * y_c)
    mix = _dot(merged.astype(BF16), wo_ref[...])
    h1 = _layer_norm(DN_ALPHA * h_ref[...] + mix, g_ref[...], b_ref[...])
    lane = lax.broadcasted_iota(jnp.int32, (tm, LANES), 1)
    logits = jnp.where(lane < N_EXPERTS, _dot3(h1, rw_ref[...]), -jnp.inf)
    e = jnp.exp(logits - jnp.max(logits, axis=1, keepdims=True))
    aff = e / jnp.sum(e, axis=1, keepdims=True)
    afft_ref[0] = aff.T[:N_EXPERTS]
    ha_ref[:, :D_MODEL] = DN_ALPHA * h1
    ha_ref[:, D_MODEL:2 * D_MODEL] = h1
    ha_ref[:, 2 * D_MODEL:] = aff


def _merge(u, o_mla, gla_o, gl, h, pw, ps, wa, wb, wc, bg, wo, g, b, rw, B, S, tm=512):
    nb = S // tm
    T = B * S
    hb = tm // POOL_HALO
    n_halo = T // POOL_HALO
    rows = lambda w: pl.BlockSpec((tm, w), lambda bb, i: (bb * nb + i, 0))
    prev = pl.BlockSpec((POOL_HALO, POOL_DIM),
                        lambda bb, i: (jnp.maximum((bb * nb + i) * hb - 1, 0), 0))
    nxt = pl.BlockSpec((POOL_HALO, POOL_DIM),
                       lambda bb, i: (jnp.minimum((bb * nb + i + 1) * hb, n_halo - 1), 0))
    c2 = lambda r, c: pl.BlockSpec((r, c), lambda bb, i: (0, 0))
    c3 = lambda a, r, c: pl.BlockSpec((a, r, c), lambda bb, i: (0, 0, 0))
    return pl.pallas_call(
        functools.partial(_merge_kernel, S),
        grid=(B, nb),
        in_specs=[rows(POOL_DIM), prev, nxt,
                  rows(MLA_HEADS * MLA_V), rows(GLA_W), rows(3 * D_MODEL), rows(D_MODEL),
                  c3(len(POOL_WINDOWS), LANES, LANES), c2(1, POOL_DIM), c2(POOL_DIM, D_MODEL),
                  c2(MLA_HEADS * MLA_V, D_MODEL), c2(GLA_W, D_MODEL), c2(1, 3 * D_MODEL),
                  c2(D_MODEL, D_MODEL), c2(1, D_MODEL), c2(1, D_MODEL), c2(D_MODEL, LANES)],
        out_specs=[rows(HA_W),
                   pl.BlockSpec((1, N_EXPERTS, tm), lambda bb, i: (bb, 0, i))],
        out_shape=[jax.ShapeDtypeStruct((T, HA_W), F32),
                   jax.ShapeDtypeStruct((B, N_EXPERTS, S), F32)],
        compiler_params=_params("parallel", "parallel"),
        name="merge",
    )(u, u, u, o_mla, gla_o, gl, h, pw, ps, wa, wb, wc, bg, wo, g, b, rw)


def _topk_kernel(C, aff_ref, tri_ref, idx_ref, p_scr):
    S = aff_ref.shape[2]
    n_chunks = S // LANES
    aff = aff_ref[0]

    def count(mask):
        return jnp.sum(jnp.where(mask, 1.0, 0.0), axis=1, keepdims=True)

    def as_float(bits):
        return lax.bitcast_convert_type(bits, F32)

    def refine(i, thr):
        cand = thr | jnp.left_shift(jnp.int32(1), 30 - i)
        return jnp.where(count(aff >= as_float(cand)) >= C, cand, thr)

    thr = lax.fori_loop(0, 31, refine, jnp.zeros((N_EXPERTS, 1), jnp.int32))
    above = aff >= as_float(thr + 1)
    tied = (aff >= as_float(thr)) & jnp.logical_not(above)
    need = C - count(above)
    tri = tri_ref[...]

    tied_f = tied.astype(F32)
    run = jnp.zeros((N_EXPERTS, 1), F32)
    sel_parts = []
    for c in range(n_chunks):
        cs = slice(c * LANES, (c + 1) * LANES)
        incl = _dot(tied_f[:, cs].astype(BF16), tri) + run
        run = incl[:, LANES - 1:LANES]
        sel_parts.append(jnp.where(above[:, cs] | (tied[:, cs] & (incl <= need)), 1.0, 0.0))

    p_scr[...] = jnp.zeros(p_scr.shape, F32)
    lane_e = lax.broadcasted_iota(jnp.int32, (N_EXPERTS, LANES), 1)
    chunk_end = jnp.full((N_EXPERTS, LANES), NEVER, F32)
    run = jnp.zeros((N_EXPERTS, 1), F32)
    for c in range(n_chunks):
        rel = _dot(sel_parts[c].astype(BF16), tri)
        p_scr[:, c, :] = rel
        run = run + rel[:, LANES - 1:LANES]
        chunk_end = jnp.where(lane_e == c, run, chunk_end)

    slot = lax.broadcasted_iota(jnp.int32, (C, LANES), 0).astype(F32)
    lane_c = lax.broadcasted_iota(jnp.int32, (C, LANES), 1).astype(F32)
    ones = jnp.ones((8, LANES), BF16)
    for e in range(N_EXPERTS):
        ce = chunk_end[e:e + 1, :]
        full = ce <= slot
        n_full = jnp.sum(jnp.where(full, 1.0, 0.0), axis=1, keepdims=True)
        base = jnp.max(jnp.where(full, ce, 0.0), axis=1, keepdims=True)
        pick = jnp.where(lane_c == n_full, 1.0, 0.0).astype(BF16)
        rel = _dot(pick, p_scr[e].astype(BF16))
        w = jnp.where(rel <= slot - base, 1.0, 0.0) + jnp.where(full, float(LANES), 0.0)
        tok = _dot_nt(ones, w.astype(BF16))[0:1, :]
        idx_ref[0, e:e + 1, :] = tok.astype(jnp.int32)


def _topk(afft, tri, C):
    B, E, S = afft.shape
    assert S // LANES <= LANES
    return pl.pallas_call(
        functools.partial(_topk_kernel, C),
        grid=(B,),
        in_specs=[pl.BlockSpec((1, E, S), lambda b: (b, 0, 0)),
                  pl.BlockSpec((LANES, LANES), lambda b: (0, 0))],
        out_specs=pl.BlockSpec((1, E, C), lambda b: (b, 0, 0)),
        out_shape=jax.ShapeDtypeStruct((B, E, C), jnp.int32),
        scratch_shapes=[pltpu.VMEM((E, LANES, LANES), F32)],
        compiler_params=_params("parallel"),
        name="expert_choice",
    )(afft, tri)


def _ffn_kernel(tc, n_steps, idx_ref, ha_in_ref, wg_ref, wu_ref, wd_ref, ha_ref,
                gbuf, sbuf, gsem, ssem):
    del ha_in_ref
    e = pl.program_id(0)
    s = (e * pl.num_programs(1) + pl.program_id(1)) * pl.num_programs(2) + pl.program_id(2)
    slot = s % 2

    def gather_copy(step, r, sl):
        row = idx_ref[step * tc + r]
        return pltpu.make_async_copy(ha_ref.at[pl.ds(row, 1), :],
                                     gbuf.at[sl, pl.ds(r, 1), :], gsem.at[sl])

    def scatter_copy(step, r, sl):
        row = idx_ref[step * tc + r]
        return pltpu.make_async_copy(sbuf.at[sl, pl.ds(r, 1), :],
                                     ha_ref.at[pl.ds(row, 1), pl.ds(0, D_MODEL)], ssem.at[sl])

    def start_rows(copy, step, sl):
        def body(r, carry):
            copy(step, r, sl).start()
            return carry
        lax.fori_loop(0, tc, body, 0, unroll=DMA_UNROLL)

    def wait_gather(sl):
        pltpu.make_async_copy(ha_ref.at[pl.ds(0, tc), :], gbuf.at[sl], gsem.at[sl]).wait()

    def wait_scatter(sl):
        pltpu.make_async_copy(sbuf.at[sl], ha_ref.at[pl.ds(0, tc), pl.ds(0, D_MODEL)],
                              ssem.at[sl]).wait()

    @pl.when(s == 0)
    def _():
        start_rows(gather_copy, s, slot)

    @pl.when(s >= 2)
    def _():
        wait_scatter(slot)

    @pl.when(s + 1 < n_steps)
    def _():
        start_rows(gather_copy, s + 1, 1 - slot)

    wait_gather(slot)

    rows = gbuf[slot]
    x = rows[:, D_MODEL:2 * D_MODEL].astype(BF16)
    lane = lax.broadcasted_iota(jnp.int32, (tc, LANES), 1)
    gate = jnp.sum(jnp.where(lane == e, rows[:, 2 * D_MODEL:], 0.0), axis=1, keepdims=True)
    y = jnp.zeros((tc, D_MODEL), F32)
    fh = D_EXPERT // 2
    for f in range(2):
        fs = slice(f * fh, (f + 1) * fh)
        hg = _dot(x, wg_ref[0, 0, :, fs])
        hu = _dot(x, wu_ref[0, 0, :, fs])
        hid = (hg * jax.nn.sigmoid(hg) * hu).astype(BF16)
        y += _dot(hid, wd_ref[0, 0, fs, :])
    sbuf[slot] = rows[:, :D_MODEL] + gate * y
    start_rows(scatter_copy, s, slot)

    @pl.when(s == n_steps - 1)
    def _():
        if n_steps >= 2:
            wait_scatter(1 - slot)
        wait_scatter(slot)


def _ffn(ha, idx_steps, wg, wu, wd, layer, B, C, tc=512):
    nblk = C // tc
    n_steps = N_EXPERTS * B * nblk
    wspec = lambda r, c: pl.BlockSpec((1, 1, r, c), lambda e, b, j, idx: (layer, e, 0, 0))
    return pl.pallas_call(
        functools.partial(_ffn_kernel, tc, n_steps),
        grid_spec=pltpu.PrefetchScalarGridSpec(
            num_scalar_prefetch=1,
            grid=(N_EXPERTS, B, nblk),
            in_specs=[pl.BlockSpec(memory_space=pl.ANY),
                      wspec(D_MODEL, D_EXPERT), wspec(D_MODEL, D_EXPERT),
                      wspec(D_EXPERT, D_MODEL)],
            out_specs=pl.BlockSpec(memory_space=pl.ANY),
            scratch_shapes=[pltpu.VMEM((2, tc, HA_W), F32), pltpu.VMEM((2, tc, D_MODEL), F32),
                            pltpu.SemaphoreType.DMA((2,)), pltpu.SemaphoreType.DMA((2,))]),
        out_shape=jax.ShapeDtypeStruct(ha.shape, F32),
        input_output_aliases={1: 0},
        compiler_params=_params("arbitrary", "arbitrary", "arbitrary"),
        name="expert_ffn",
    )(idx_steps, ha, wg, wu, wd)


def _ln_kernel(x_ref, g_ref, b_ref, o_ref):
    o_ref[...] = _layer_norm(x_ref[...], g_ref[...], b_ref[...])


def _ln_rows(xin, g, b, tm=512):
    T = xin.shape[0]
    const = lambda i: (0, 0)
    return pl.pallas_call(
        _ln_kernel,
        grid=(T // tm,),
        in_specs=[pl.BlockSpec((tm, D_MODEL), lambda i: (i, 0)),
                  pl.BlockSpec((1, D_MODEL), const), pl.BlockSpec((1, D_MODEL), const)],
        out_specs=pl.BlockSpec((tm, D_MODEL), lambda i: (i, 0)),
        out_shape=jax.ShapeDtypeStruct((T, D_MODEL), F32),
        compiler_params=_params("parallel"),
        name="final_ln",
    )(xin, g, b)


def _pad_heads(w, heads, width):
    lead = w.shape[:-1]
    w = w.reshape(lead + (heads, width))
    w = jnp.pad(w, [(0, 0)] * len(lead) + [(0, 0), (0, HEAD_PAD - width)])
    return w.reshape(lead + (heads * HEAD_PAD,))


def _rot_half(w):
    half = w.shape[-1] // 2
    return jnp.concatenate([-w[..., half:], w[..., :half]], axis=-1)


def _pack_w_in(w):
    o = 0
    parts = {}
    for name, width in (("pool", POOL_DIM), ("cq", MLA_Q_RANK), ("ckv", MLA_KV_RANK),
                        ("kr", MLA_ROPE), ("gq", GLA_HEADS * GLA_DK), ("gk", GLA_HEADS * GLA_DK),
                        ("gv", GLA_HEADS * GLA_DV), ("gr", GLA_HEADS * GLA_DV),
                        ("gd", 2 * GLA_GATE_RANK), ("gates", 3 * D_MODEL)):
        parts[name] = w[:, o:o + width]
        o += width
    rope_slot = lambda m: jnp.pad(m, ((0, 0), (MLA_NOPE, HEAD_PAD - MLA_NOPE - MLA_ROPE)))
    kr2 = jnp.concatenate([rope_slot(parts["kr"]), rope_slot(_rot_half(parts["kr"]))], axis=1)
    gd = jnp.pad(parts["gd"], ((0, 0), (0, LANES - 2 * GLA_GATE_RANK)))
    cols = [parts["pool"], parts["cq"], parts["ckv"], kr2,
            _pad_heads(parts["gq"], GLA_HEADS, GLA_DK), _pad_heads(parts["gk"], GLA_HEADS, GLA_DK),
            parts["gv"], parts["gr"], gd, parts["gates"]]
    return jnp.concatenate(cols, axis=1).astype(BF16)


def _pack_mla(w_uq, w_ukv):
    r = w_uq.shape[0]
    uq = w_uq.reshape(r, MLA_HEADS, MLA_NOPE + MLA_ROPE)
    nope, rope = uq[..., :MLA_NOPE], uq[..., MLA_NOPE:]
    tail = jnp.zeros((r, MLA_HEADS, HEAD_PAD - MLA_NOPE - MLA_ROPE), F32)
    wq = jnp.concatenate([nope, rope, tail], axis=-1).reshape(r, -1)
    wq2 = jnp.concatenate([jnp.zeros_like(nope), _rot_half(rope), tail], axis=-1).reshape(r, -1)
    rk = w_ukv.shape[0]
    ukv = w_ukv.reshape(rk, MLA_HEADS, MLA_NOPE + MLA_V)
    wk = _pad_heads(ukv[..., :MLA_NOPE].reshape(rk, -1), MLA_HEADS, MLA_NOPE)
    wv = _pad_heads(ukv[..., MLA_NOPE:].reshape(rk, -1), MLA_HEADS, MLA_V)
    return wq.astype(BF16), wq2.astype(BF16), wk.astype(BF16), wv.astype(BF16)


def _pack_gla_decay(w_dec, b_dec):
    wd = []
    for d in range(2):
        rows = _pad_heads(w_dec[d], GLA_HEADS, GLA_DK)
        wd.append(jnp.pad(rows, ((d * GLA_GATE_RANK, LANES - (d + 1) * GLA_GATE_RANK), (0, 0))))
    bd = [_pad_heads(b_dec[d][None, :], GLA_HEADS, GLA_DK) for d in range(2)]
    return wd, bd


def _chunk_tri(tm, reverse):
    r = jnp.arange(tm)[:, None]
    c = jnp.arange(tm)[None, :]
    same = (r // GLA_CHUNK) == (c // GLA_CHUNK)
    return (same & ((c >= r) if reverse else (c <= r))).astype(BF16)


def kernel(x, positions, ln0_g, ln0_b, w_in, b_gate, pool_w, pool_scale, w_up_a, mla_q_norm,
           mla_w_uq, mla_kv_norm, mla_w_ukv, w_up_b, gla_w_dec, gla_b_dec, gla_norm, w_up_c,
           w_out, ln1_g, ln1_b, router_w, exp_w_gate, exp_w_up, exp_w_down, ln2_g, ln2_b):
    B, S, D = x.shape
    assert D == D_MODEL and S % 512 == 0 and B == 2
    T = B * S
    C = CAPACITY_FACTOR * S // N_EXPERTS
    tc = min(512, C)
    row = lambda v: v.reshape(1, -1).astype(F32)

    half = MLA_ROPE // 2
    freqs = ROPE_THETA ** (-jnp.arange(half, dtype=F32) / half)
    lanes = jnp.arange(LANES)
    in_rope = (lanes >= MLA_NOPE) & (lanes < MLA_NOPE + MLA_ROPE)
    freq_row = jnp.where(in_rope, freqs[(lanes - MLA_NOPE) % half], 0.0).reshape(1, LANES)
    rope_mask = in_rope.astype(F32).reshape(1, LANES)
    nope_row = (lanes < MLA_NOPE).astype(F32).reshape(1, LANES)
    one_row = (lanes == MLA_V).astype(F32).reshape(1, LANES)
    cos_t, sin_t = _rope_tables(positions.reshape(T, 1), freq_row, rope_mask)

    tri_f, tri_b = _chunk_tri(512, False), _chunk_tri(512, True)
    tri_lane = (jnp.arange(LANES)[:, None] <= jnp.arange(LANES)[None, :]).astype(BF16)

    wg_all, wu_all, wd_all = (w.astype(BF16) for w in (exp_w_gate, exp_w_up, exp_w_down))
    stream, g_in, b_in = x.reshape(T, D), ln0_g, ln0_b
    for l in range(DEPTH):
        h, u, cq, ckv, kr2, gq, gk, gv, gr, gd, gl = _ln_inproj(
            stream, row(g_in), row(b_in), _pack_w_in(w_in[l]))
        wq, wq2, wk, wv = _pack_mla(mla_w_uq[l], mla_w_ukv[l])
        q, k, v = _mla_prep(cq, ckv, kr2, cos_t, sin_t, row(mla_q_norm[l]), row(mla_kv_norm[l]),
                            wq, wq2, wk, wv, nope_row, one_row, B, S)
        o_mla = _flash(q, k, v, tq=1024).reshape(T, MLA_HEADS * MLA_V)
        wd, bd = _pack_gla_decay(gla_w_dec[l], gla_b_dec[l])
        gla_o = _gla(gq, gk, gv, gd, gr, wd, bd, tri_f, tri_b, row(gla_norm[l]), B, S)
        rw = jnp.pad(router_w[l], ((0, 0), (0, LANES - N_EXPERTS)))
        ha, afft = _merge(u, o_mla, gla_o, gl, h, pool_w[l].astype(BF16), row(pool_scale[l]),
                          w_up_a[l].astype(BF16), w_up_b[l].astype(BF16),
                          w_up_c[l].astype(BF16), row(b_gate[l]),
                          w_out[l].astype(BF16), row(ln1_g[l]), row(ln1_b[l]), rw, B, S)
        idx = _topk(afft, tri_lane, C)
        idx_steps = (idx + (jnp.arange(B, dtype=jnp.int32) * S)[:, None, None])
        idx_steps = idx_steps.transpose(1, 0, 2).reshape(-1)
        stream = _ffn(ha, idx_steps, wg_all, wu_all, wd_all, l, B, C, tc)
        g_in, b_in = ln2_g[l], ln2_b[l]
    out = _ln_rows(stream, row(g_in), row(b_in))
    return out.reshape(B, S, D)
```

```python
import functools

import jax
import jax.numpy as jnp
from jax import lax
from jax.experimental import pallas as pl
from jax.experimental.pallas import tpu as pltpu

F32 = jnp.float32
BF16 = jnp.bfloat16

LANES = 128
VMEM_LIMIT = 56 * 1024 * 1024
FLASH_VMEM_LIMIT = 60 * 1024 * 1024

D_MODEL = 1024
DEPTH = 2
POOL_WINDOWS = (2, 4, 8, 16)
POOL_DIM = 512
POOL_HALO = 16
MLA_HEADS = 8
MLA_NOPE = 64
MLA_ROPE = 32
MLA_V = 64
MLA_Q_RANK = 384
MLA_KV_RANK = 256
ROPE_THETA = 10000.0
GLA_HEADS = 4
GLA_DK = 64
GLA_DV = 128
GLA_GATE_RANK = 16
GLA_GATE_NORM = 16.0
GLA_CHUNK = 64
N_EXPERTS = 16
CAPACITY_FACTOR = 2
D_EXPERT = 2048
DN_ALPHA = (2 * DEPTH) ** 0.25
LN_EPS = 1e-5
RMS_EPS = 1e-6
LOG2_E = 1.4426950408889634
NEVER = 1e9

HEAD_PAD = LANES
GLA_W = GLA_HEADS * HEAD_PAD
HA_W = 2 * D_MODEL + LANES
DMA_UNROLL = 16

_SEG = (("pool", POOL_DIM), ("cq", MLA_Q_RANK), ("ckv", MLA_KV_RANK), ("kr2", 2 * HEAD_PAD),
        ("gq", GLA_W), ("gk", GLA_W), ("gv", GLA_W), ("gr", GLA_W), ("gd", LANES),
        ("gates", 3 * D_MODEL))
_SEG_OFF = {}
_off = 0
for _n, _w in _SEG:
    _SEG_OFF[_n] = (_off, _w)
    _off += _w
W_ALL_COLS = _off


def _params(*sem):
    return pltpu.CompilerParams(dimension_semantics=sem, vmem_limit_bytes=VMEM_LIMIT)


def _dot(a, b):
    return jnp.dot(a, b, preferred_element_type=F32)


def _dot_nt(a, b):
    return lax.dot_general(a, b, (((1,), (1,)), ((), ())), preferred_element_type=F32)


def _split(x):
    hi = x.astype(BF16)
    lo = (x - hi.astype(F32)).astype(BF16)
    return hi, lo


def _dot3(a, b, nt=False):
    d = _dot_nt if nt else _dot
    ah, al = _split(a)
    bh, bl = _split(b)
    return d(ah, bh) + d(ah, bl) + d(al, bh)


def _layer_norm(x, g, b):
    mu = jnp.mean(x, axis=-1, keepdims=True)
    xc = x - mu
    var = jnp.mean(xc * xc, axis=-1, keepdims=True)
    return xc * lax.rsqrt(var + LN_EPS) * g + b


def _rms_norm(x, g):
    ms = jnp.mean(x * x, axis=-1, keepdims=True)
    return x * lax.rsqrt(ms + RMS_EPS) * g


def _rope_kernel(pos_ref, freq_ref, mask_ref, cos_ref, sin_ref):
    ang = pos_ref[...].astype(F32) * freq_ref[...]
    cos_ref[...] = jnp.cos(ang) * mask_ref[...]
    sin_ref[...] = jnp.sin(ang) * mask_ref[...]


def _rope_tables(pos_col, freq_row, mask_row, tm=512):
    T = pos_col.shape[0]
    row = pl.BlockSpec((1, LANES), lambda i: (0, 0))
    out = pl.BlockSpec((tm, LANES), lambda i: (i, 0))
    return pl.pallas_call(
        _rope_kernel,
        grid=(T // tm,),
        in_specs=[pl.BlockSpec((tm, 1), lambda i: (i, 0)), row, row],
        out_specs=[out, out],
        out_shape=[jax.ShapeDtypeStruct((T, LANES), F32)] * 2,
        compiler_params=_params("parallel"),
        name="rope_tables",
    )(pos_col, freq_row, mask_row)


def _inproj_kernel(x_ref, g_ref, b_ref, w_ref, h_ref, *out_refs):
    h = _layer_norm(x_ref[...], g_ref[...], b_ref[...])
    h_ref[...] = h
    hb = h.astype(BF16)
    for (name, width), o_ref in zip(_SEG, out_refs):
        off = _SEG_OFF[name][0]
        o_ref[...] = _dot(hb, w_ref[:, off:off + width]).astype(o_ref.dtype)


def _ln_inproj(xin, g, b, w_all, tm=256):
    T = xin.shape[0]
    const = lambda i: (0, 0)
    out_shapes = [jax.ShapeDtypeStruct((T, D_MODEL), F32)]
    out_specs = [pl.BlockSpec((tm, D_MODEL), lambda i: (i, 0))]
    for name, width in _SEG:
        out_shapes.append(jax.ShapeDtypeStruct((T, width), F32 if name == "gd" else BF16))
        out_specs.append(pl.BlockSpec((tm, width), lambda i: (i, 0)))
    return pl.pallas_call(
        _inproj_kernel,
        grid=(T // tm,),
        in_specs=[pl.BlockSpec((tm, D_MODEL), lambda i: (i, 0)),
                  pl.BlockSpec((1, D_MODEL), const), pl.BlockSpec((1, D_MODEL), const),
                  pl.BlockSpec((D_MODEL, W_ALL_COLS), const)],
        out_specs=out_specs,
        out_shape=out_shapes,
        compiler_params=_params("parallel"),
        name="ln_inproj",
    )(xin, g, b, w_all)


def _mla_prep_kernel(cq_ref, ckv_ref, kr2_ref, cos_ref, sin_ref, qn_ref, kvn_ref,
                     wq_ref, wq2_ref, wk_ref, wv_ref, nope_ref, one_ref,
                     q_ref, k_ref, v_ref):
    scale = (MLA_NOPE + MLA_ROPE) ** -0.5 * LOG2_E
    cosr = cos_ref[...]
    sinr = sin_ref[...]
    nq = _rms_norm(cq_ref[...].astype(F32), qn_ref[...]).astype(BF16)
    nkv = _rms_norm(ckv_ref[...].astype(F32), kvn_ref[...]).astype(BF16)
    qa = _dot(nq, wq_ref[...])
    qb = _dot(nq, wq2_ref[...])
    ka = _dot(nkv, wk_ref[...])
    va = _dot(nkv, wv_ref[...])
    kr2 = kr2_ref[...].astype(F32)
    k_rope = kr2[:, :HEAD_PAD] * cosr + kr2[:, HEAD_PAD:] * sinr
    q_cos = (cosr + nope_ref[...]) * scale
    q_sin = sinr * scale
    for h in range(MLA_HEADS):
        sl = slice(h * HEAD_PAD, (h + 1) * HEAD_PAD)
        q_ref[0, h] = (qa[:, sl] * q_cos + qb[:, sl] * q_sin).astype(BF16)
        k_ref[0, h] = (ka[:, sl] + k_rope).astype(BF16)
        v_ref[0, h] = (va[:, sl] + one_ref[...]).astype(BF16)


def _mla_prep(cq, ckv, kr2, cos_t, sin_t, qn, kvn, wq, wq2, wk, wv, nope_row, one_row, B, S,
              tm=512):
    nb = S // tm
    rows = lambda w: pl.BlockSpec((tm, w), lambda b, i: (b * nb + i, 0))
    const = lambda r, c: pl.BlockSpec((r, c), lambda b, i: (0, 0))
    hw = MLA_HEADS * HEAD_PAD
    out = pl.BlockSpec((1, MLA_HEADS, tm, HEAD_PAD), lambda b, i: (b, 0, i, 0))
    return pl.pallas_call(
        _mla_prep_kernel,
        grid=(B, nb),
        in_specs=[rows(MLA_Q_RANK), rows(MLA_KV_RANK), rows(2 * HEAD_PAD), rows(LANES),
                  rows(LANES), const(1, MLA_Q_RANK), const(1, MLA_KV_RANK),
                  const(MLA_Q_RANK, hw), const(MLA_Q_RANK, hw), const(MLA_KV_RANK, hw),
                  const(MLA_KV_RANK, hw), const(1, LANES), const(1, LANES)],
        out_specs=[out, out, out],
        out_shape=[jax.ShapeDtypeStruct((B, MLA_HEADS, S, HEAD_PAD), BF16)] * 3,
        compiler_params=_params("parallel", "parallel"),
        name="mla_prep",
    )(cq, ckv, kr2, cos_t, sin_t, qn, kvn, wq, wq2, wk, wv, nope_row, one_row)


def _flash_kernel(q_ref, k_ref, v_ref, wg_ref, wu_ref, wd_ref, o_ref, wgb_ref, wub_ref, wdb_ref,
                  m_scr, acc_scr):
    kv = pl.program_id(2)
    wgb_ref[...] = wg_ref[0].astype(BF16)
    wub_ref[...] = wu_ref[0].astype(BF16)
    wdb_ref[...] = wd_ref[0].astype(BF16)

    @pl.when(kv == 0)
    def _():
        m_scr[...] = jnp.full(m_scr.shape, -jnp.inf, F32)
        acc_scr[...] = jnp.zeros(acc_scr.shape, F32)

    tk = k_ref.shape[2]
    for h in range(MLA_HEADS):
        s = _dot_nt(q_ref[0, h], k_ref[0, h])
        m_prev = m_scr[h]
        m_new = jnp.maximum(m_prev, jnp.max(s, axis=1, keepdims=True))
        p = jnp.exp2(s - jnp.concatenate([m_new] * (tk // LANES), axis=1))
        alpha = jnp.exp2(m_prev - m_new)
        acc_scr[h] = alpha * acc_scr[h] + _dot(p.astype(BF16), v_ref[0, h])
        m_scr[h] = m_new

    @pl.when(kv == pl.num_programs(2) - 1)
    def _():
        lane = lax.broadcasted_iota(jnp.int32, (q_ref.shape[2], HEAD_PAD), 1)
        for hp in range(MLA_HEADS // 2):
            a0 = acc_scr[2 * hp]
            a1 = acc_scr[2 * hp + 1]
            o0 = a0 / a0[:, MLA_V:MLA_V + 1]
            o1 = a1 / a1[:, MLA_V:MLA_V + 1]
            pair = jnp.where(lane < MLA_V, o0, pltpu.roll(o1, MLA_V, axis=1))
            o_ref[0, :, hp * HEAD_PAD:(hp + 1) * HEAD_PAD] = pair.astype(BF16)


def _flash(q, k, v, expert_w, layer, tq=1024, tk=2048):
    B, H, S, _ = q.shape
    tq, tk = min(tq, S), min(tk, S)
    assert S % tq == 0 and S % tk == 0
    nq, nk = S // tq, S // tk
    n_steps = B * nq * nk
    qspec = pl.BlockSpec((1, H, tq, HEAD_PAD), lambda b, i, j: (b, 0, i, 0))
    kspec = pl.BlockSpec((1, H, tk, HEAD_PAD), lambda b, i, j: (b, 0, j, 0))
    w_in, w_specs, wb_specs, wb_shapes = [], [], [], []
    for w in expert_w:
        n_l, n_e, r, c = w.shape
        slab = n_e * r // n_steps
        assert n_e * r % n_steps == 0 and slab % 16 == 0
        w_in.append(w.reshape(n_l, n_steps, slab, c))
        w_specs.append(pl.BlockSpec((1, 1, slab, c),
                                    lambda b, i, j: (layer, (b * nq + i) * nk + j, 0, 0)))
        wb_specs.append(pl.BlockSpec((1, slab, c), lambda b, i, j: ((b * nq + i) * nk + j, 0, 0)))
        wb_shapes.append(jax.ShapeDtypeStruct((n_steps, slab, c), BF16))
    o, *wb = pl.pallas_call(
        _flash_kernel,
        grid=(B, nq, nk),
        in_specs=[qspec, kspec, kspec] + w_specs,
        out_specs=[pl.BlockSpec((1, tq, H * MLA_V), lambda b, i, j: (b, i, 0))] + wb_specs,
        out_shape=[jax.ShapeDtypeStruct((B, S, H * MLA_V), BF16)] + wb_shapes,
        scratch_shapes=[pltpu.VMEM((H, tq, LANES), F32), pltpu.VMEM((H, tq, HEAD_PAD), F32)],
        compiler_params=pltpu.CompilerParams(
            dimension_semantics=("parallel", "parallel", "arbitrary"),
            vmem_limit_bytes=FLASH_VMEM_LIMIT),
        name="mla_flash",
    )(q, k, v, *w_in)
    return o, [b16.reshape(w.shape[1:]) for b16, w in zip(wb, expert_w)]


def _log_sigmoid(x):
    return jnp.minimum(x, 0.0) - jnp.log1p(jnp.exp(-jnp.abs(x)))


def _gla_scan(reverse, gq_ref, gk_ref, gv_ref, gd_ref, wd_ref, bd_ref, tri_ref, state_scr,
              o_scr):
    tm = gq_ref.shape[0]
    L = GLA_CHUNK
    n_chunks = tm // L

    @pl.when(pl.program_id(1) == 0)
    def _():
        state_scr[...] = jnp.zeros(state_scr.shape, F32)

    logits = _dot3(gd_ref[...], wd_ref[...]) + bd_ref[...]
    g = _log_sigmoid(logits) * (1.0 / GLA_GATE_NORM)
    g_hi, g_lo = _split(g)
    b = _dot(tri_ref[...], g_hi) + _dot(tri_ref[...], g_lo)
    edge = 0 if reverse else L - 1
    b_last = jnp.concatenate(
        [jnp.broadcast_to(b[c * L + edge:c * L + edge + 1, :], (L, GLA_W))
         for c in range(n_chunks)], axis=0)
    q_in = (gq_ref[...].astype(F32) * (GLA_DK ** -0.5) * jnp.exp(b)).astype(BF16)
    k_f = gk_ref[...].astype(F32)
    k_in = (k_f * jnp.exp(-b)).astype(BF16)
    k_st = (k_f * jnp.exp(b_last - b)).astype(BF16)
    dec = jnp.exp(b_last)
    v = gv_ref[...]
    lane = lax.broadcasted_iota(jnp.int32, (HEAD_PAD, 2 * L), 1)
    order = range(n_chunks - 1, -1, -1) if reverse else range(n_chunks)
    heads = [slice(h * HEAD_PAD, (h + 1) * HEAD_PAD) for h in range(GLA_HEADS)]
    kv = {}
    for h, hs in enumerate(heads):
        a = _dot_nt(q_in[:, hs], k_in[:, hs]).astype(BF16) * tri_ref[...]
        o_scr[:, hs] = _dot(a, v[:, hs])
        v_t = v[:, hs].astype(F32).T
        for c in range(n_chunks):
            pair = slice((c // 2) * 2 * L, (c // 2 + 1) * 2 * L)
            in_chunk = (lane >= L) if c % 2 else (lane < L)
            v_tc = jnp.where(in_chunk, v_t[:, pair], 0.0).astype(BF16)
            kv[c, h] = _dot(v_tc, k_st[pair, hs])
    entering = {}
    for h, hs in enumerate(heads):
        st = state_scr[h]
        for c in order:
            entering[c, h] = st.astype(BF16)
            st = dec[c * L:c * L + 1, hs] * st + kv[c, h]
        state_scr[h] = st
    for c in range(n_chunks):
        rs = slice(c * L, (c + 1) * L)
        for h, hs in enumerate(heads):
            o_scr[rs, hs] += _dot_nt(q_in[rs, hs], entering[c, h])


def _gla_fwd_kernel(gq_ref, gk_ref, gv_ref, gd_ref, wd_ref, bd_ref, tri_ref, o_ref, state_scr,
                    o_scr):
    _gla_scan(False, gq_ref, gk_ref, gv_ref, gd_ref, wd_ref, bd_ref, tri_ref, state_scr, o_scr)
    o_ref[...] = o_scr[...]


def _gla_bwd_kernel(gq_ref, gk_ref, gv_ref, gd_ref, wd_ref, bd_ref, tri_ref, of_ref, gr_ref,
                    ng_ref, out_ref, state_scr, o_scr):
    _gla_scan(True, gq_ref, gk_ref, gv_ref, gd_ref, wd_ref, bd_ref, tri_ref, state_scr, o_scr)
    r = gr_ref[...].astype(F32)
    gate = r * jax.nn.sigmoid(r)
    for h in range(GLA_HEADS):
        hs = slice(h * HEAD_PAD, (h + 1) * HEAD_PAD)
        o = _rms_norm(of_ref[:, hs] + o_scr[:, hs], ng_ref[...])
        out_ref[:, hs] = (o * gate[:, hs]).astype(BF16)


def _gla(gq, gk, gv, gd, gr, wd, bd, tri_f, tri_b, norm_g, B, S, tm=512):
    nb = S // tm
    T = B * S

    def call(reverse):
        blk = (lambda b, i: (b * nb + nb - 1 - i, 0)) if reverse else (lambda b, i: (b * nb + i, 0))
        rows = lambda w: pl.BlockSpec((tm, w), blk)
        const = lambda r, c: pl.BlockSpec((r, c), lambda b, i: (0, 0))
        in_specs = [rows(GLA_W), rows(GLA_W), rows(GLA_W), rows(LANES),
                    const(LANES, GLA_W), const(1, GLA_W), const(tm, tm)]
        scratch = [pltpu.VMEM((GLA_HEADS, GLA_DV, HEAD_PAD), F32), pltpu.VMEM((tm, GLA_W), F32)]
        d = 1 if reverse else 0
        args = [gq, gk, gv, gd, wd[d], bd[d], tri_b if reverse else tri_f]
        if reverse:
            in_specs += [rows(GLA_W), rows(GLA_W), const(1, GLA_DV)]
            args += [o_f, gr, norm_g]
        return pl.pallas_call(
            _gla_bwd_kernel if reverse else _gla_fwd_kernel,
            grid=(B, nb),
            in_specs=in_specs,
            out_specs=rows(GLA_W),
            out_shape=jax.ShapeDtypeStruct((T, GLA_W), BF16 if reverse else F32),
            scratch_shapes=scratch,
            compiler_params=_params("parallel", "arbitrary"),
            name="gla_bwd" if reverse else "gla_fwd",
        )(*args)

    o_f = call(False)
    return call(True)


def _merge_kernel(S, u_ref, up_ref, un_ref, om_ref, gla_ref, gl_ref, h_ref,
                  pw_ref, ps_ref, wa_ref, wb_ref, wc_ref, bg_ref, wo_ref, g_ref, b_ref,
                  rw_ref, ha_ref, afft_ref):
    tm = u_ref.shape[0]
    i = pl.program_id(1)
    ext = jnp.concatenate([up_ref[...], u_ref[...], un_ref[...]], axis=0).astype(F32)
    n_ext = tm + 2 * POOL_HALO
    pos_ext = i * tm - POOL_HALO + lax.broadcasted_iota(jnp.int32, (n_ext, LANES), 0)
    in_seq = (pos_ext >= 0) & (pos_ext < S)
    pos = i * tm + lax.broadcasted_iota(jnp.int32, (tm, LANES), 0)
    core = slice(POOL_HALO, POOL_HALO + tm)

    def rows_at(a, d):
        return pltpu.roll(a, (-d) % n_ext, axis=0)

    pooled = []
    for gi, w in enumerate(POOL_WINDOWS):
        hw = w // 2
        cs = slice(gi * LANES, (gi + 1) * LANES)
        x = jnp.where(in_seq, ext[:, cs], 0.0)
        win = rows_at(x, -1) + x
        reach = 1
        while reach < hw:
            win = rows_at(win, -reach) + rows_at(win, reach)
            reach *= 2
        cnt = (jnp.minimum(pos + hw, S) - jnp.maximum(pos - hw, 0)).astype(F32)
        pg = (win[core] / cnt - x[core]).astype(BF16)
        pooled.append(_dot(pg, pw_ref[gi]))
    pa = (jnp.concatenate(pooled, axis=1) * ps_ref[...]).astype(BF16)
    y_a = _dot(pa, wa_ref[...])
    y_b = _dot(om_ref[...], wb_ref[...])
    y_c = _dot(gla_ref[...], wc_ref[...])
    gates = jax.nn.sigmoid(gl_ref[...].astype(F32) + bg_ref[...])
    merged = (gates[:, :D_MODEL] * y_a + gates[:, D_MODEL:2 * D_MODEL] * y_b
              + gates[:, 2 * D_MODEL:] * y_c)
    mix = _dot(merged.astype(BF16), wo_ref[...])
    h1 = _layer_norm(DN_ALPHA * h_ref[...] + mix, g_ref[...], b_ref[...])
    lane = lax.broadcasted_iota(jnp.int32, (tm, LANES), 1)
    logits = jnp.where(lane < N_EXPERTS, _dot3(h1, rw_ref[...]), -jnp.inf)
    e = jnp.exp(logits - jnp.max(logits, axis=1, keepdims=True))
    aff = e / jnp.sum(e, axis=1, keepdims=True)
    afft_ref[0] = aff.T[:N_EXPERTS]
    ha_ref[:, :D_MODEL] = DN_ALPHA * h1
    ha_ref[:, D_MODEL:2 * D_MODEL] = h1
    ha_ref[:, 2 * D_MODEL:] = aff


def _merge(u, o_mla, gla_o, gl, h, pw, ps, wa, wb, wc, bg, wo, g, b, rw, B, S, tm=512):
    nb = S // tm
    T = B * S
    hb = tm // POOL_HALO
    n_halo = T // POOL_HALO
    rows = lambda w: pl.BlockSpec((tm, w), lambda bb, i: (bb * nb + i, 0))
    prev = pl.BlockSpec((POOL_HALO, POOL_DIM),
                        lambda bb, i: (jnp.maximum((bb * nb + i) * hb - 1, 0), 0))
    nxt = pl.BlockSpec((POOL_HALO, POOL_DIM),
                       lambda bb, i: (jnp.minimum((bb * nb + i + 1) * hb, n_halo - 1), 0))
    c2 = lambda r, c: pl.BlockSpec((r, c), lambda bb, i: (0, 0))
    c3 = lambda a, r, c: pl.BlockSpec((a, r, c), lambda bb, i: (0, 0, 0))
    return pl.pallas_call(
        functools.partial(_merge_kernel, S),
        grid=(B, nb),
        in_specs=[rows(POOL_DIM), prev, nxt,
                  rows(MLA_HEADS * MLA_V), rows(GLA_W), rows(3 * D_MODEL), rows(D_MODEL),
                  c3(len(POOL_WINDOWS), LANES, LANES), c2(1, POOL_DIM), c2(POOL_DIM, D_MODEL),
                  c2(MLA_HEADS * MLA_V, D_MODEL), c2(GLA_W, D_MODEL), c2(1, 3 * D_MODEL),
                  c2(D_MODEL, D_MODEL), c2(1, D_MODEL), c2(1, D_MODEL), c2(D_MODEL, LANES)],
        out_specs=[rows(HA_W),
                   pl.BlockSpec((1, N_EXPERTS, tm), lambda bb, i: (bb, 0, i))],
        out_shape=[jax.ShapeDtypeStruct((T, HA_W), F32),
                   jax.ShapeDtypeStruct((B, N_EXPERTS, S), F32)],
        compiler_params=_params("parallel", "parallel"),
        name="merge",
    )(u, u, u, o_mla, gla_o, gl, h, pw, ps, wa, wb, wc, bg, wo, g, b, rw)


def _topk_kernel(C, aff_ref, tri_ref, idx_ref, p_scr):
    S = aff_ref.shape[2]
    n_chunks = S // LANES
    aff = aff_ref[0]

    def count(mask):
        return jnp.sum(jnp.where(mask, 1.0, 0.0), axis=1, keepdims=True)

    def as_float(bits):
        return lax.bitcast_convert_type(bits, F32)

    def refine(i, thr):
        cand = thr | jnp.left_shift(jnp.int32(1), 30 - i)
        return jnp.where(count(aff >= as_float(cand)) >= C, cand, thr)

    thr = lax.fori_loop(0, 31, refine, jnp.zeros((N_EXPERTS, 1), jnp.int32))
    above = aff >= as_float(thr + 1)
    tied = (aff >= as_float(thr)) & jnp.logical_not(above)
    need = C - count(above)
    tri = tri_ref[...]

    tied_f = tied.astype(F32)
    run = jnp.zeros((N_EXPERTS, 1), F32)
    sel_parts = []
    for c in range(n_chunks):
        cs = slice(c * LANES, (c + 1) * LANES)
        incl = _dot(tied_f[:, cs].astype(BF16), tri) + run
        run = incl[:, LANES - 1:LANES]
        sel_parts.append(jnp.where(above[:, cs] | (tied[:, cs] & (incl <= need)), 1.0, 0.0))

    p_scr[...] = jnp.zeros(p_scr.shape, F32)
    lane_e = lax.broadcasted_iota(jnp.int32, (N_EXPERTS, LANES), 1)
    chunk_end = jnp.full((N_EXPERTS, LANES), NEVER, F32)
    run = jnp.zeros((N_EXPERTS, 1), F32)
    for c in range(n_chunks):
        rel = _dot(sel_parts[c].astype(BF16), tri)
        p_scr[:, c, :] = rel
        run = run + rel[:, LANES - 1:LANES]
        chunk_end = jnp.where(lane_e == c, run, chunk_end)

    slot = lax.broadcasted_iota(jnp.int32, (C, LANES), 0).astype(F32)
    lane_c = lax.broadcasted_iota(jnp.int32, (C, LANES), 1).astype(F32)
    ones = jnp.ones((8, LANES), BF16)
    for e in range(N_EXPERTS):
        ce = chunk_end[e:e + 1, :]
        full = ce <= slot
        n_full = jnp.sum(jnp.where(full, 1.0, 0.0), axis=1, keepdims=True)
        base = jnp.max(jnp.where(full, ce, 0.0), axis=1, keepdims=True)
        pick = jnp.where(lane_c == n_full, 1.0, 0.0).astype(BF16)
        rel = _dot(pick, p_scr[e].astype(BF16))
        w = jnp.where(rel <= slot - base, 1.0, 0.0) + jnp.where(full, float(LANES), 0.0)
        tok = _dot_nt(ones, w.astype(BF16))[0:1, :]
        idx_ref[0, e:e + 1, :] = tok.astype(jnp.int32)


def _topk(afft, tri, C):
    B, E, S = afft.shape
    assert S // LANES <= LANES
    return pl.pallas_call(
        functools.partial(_topk_kernel, C),
        grid=(B,),
        in_specs=[pl.BlockSpec((1, E, S), lambda b: (b, 0, 0)),
                  pl.BlockSpec((LANES, LANES), lambda b: (0, 0))],
        out_specs=pl.BlockSpec((1, E, C), lambda b: (b, 0, 0)),
        out_shape=jax.ShapeDtypeStruct((B, E, C), jnp.int32),
        scratch_shapes=[pltpu.VMEM((E, LANES, LANES), F32)],
        compiler_params=_params("parallel"),
        name="expert_choice",
    )(afft, tri)


def _ffn_kernel(tc, n_steps, idx_ref, ha_in_ref, wg_ref, wu_ref, wd_ref, ha_ref,
                gbuf, sbuf, gsem, ssem):
    del ha_in_ref
    e = pl.program_id(0)
    s = (e * pl.num_programs(1) + pl.program_id(1)) * pl.num_programs(2) + pl.program_id(2)
    slot = s % 2

    def gather_copy(step, r, sl):
        row = idx_ref[step * tc + r]
        return pltpu.make_async_copy(ha_ref.at[pl.ds(row, 1), :],
                                     gbuf.at[sl, pl.ds(r, 1), :], gsem.at[sl])

    def scatter_copy(step, r, sl):
        row = idx_ref[step * tc + r]
        return pltpu.make_async_copy(sbuf.at[sl, pl.ds(r, 1), :],
                                     ha_ref.at[pl.ds(row, 1), pl.ds(0, D_MODEL)], ssem.at[sl])

    def start_rows(copy, step, sl):
        def body(r, carry):
            copy(step, r, sl).start()
            return carry
        lax.fori_loop(0, tc, body, 0, unroll=DMA_UNROLL)

    def wait_gather(sl):
        pltpu.make_async_copy(ha_ref.at[pl.ds(0, tc), :], gbuf.at[sl], gsem.at[sl]).wait()

    def wait_scatter(sl):
        pltpu.make_async_copy(sbuf.at[sl], ha_ref.at[pl.ds(0, tc), pl.ds(0, D_MODEL)],
                              ssem.at[sl]).wait()

    @pl.when(s == 0)
    def _():
        start_rows(gather_copy, s, slot)

    @pl.when(s >= 2)
    def _():
        wait_scatter(slot)

    @pl.when(s + 1 < n_steps)
    def _():
        start_rows(gather_copy, s + 1, 1 - slot)

    wait_gather(slot)

    rows = gbuf[slot]
    x = rows[:, D_MODEL:2 * D_MODEL].astype(BF16)
    lane = lax.broadcasted_iota(jnp.int32, (tc, LANES), 1)
    gate = jnp.sum(jnp.where(lane == e, rows[:, 2 * D_MODEL:], 0.0), axis=1, keepdims=True)
    y = jnp.zeros((tc, D_MODEL), F32)
    fh = D_EXPERT // 2
    for f in range(2):
        fs = slice(f * fh, (f + 1) * fh)
        hg = _dot(x, wg_ref[0, :, fs])
        hu = _dot(x, wu_ref[0, :, fs])
        hid = (hg * jax.nn.sigmoid(hg) * hu).astype(BF16)
        y += _dot(hid, wd_ref[0, fs, :])
    sbuf[slot] = rows[:, :D_MODEL] + gate * y
    start_rows(scatter_copy, s, slot)

    @pl.when(s == n_steps - 1)
    def _():
        if n_steps >= 2:
            wait_scatter(1 - slot)
        wait_scatter(slot)


def _ffn(ha, idx_steps, wg, wu, wd, B, C, tc=512):
    nblk = C // tc
    n_steps = N_EXPERTS * B * nblk
    wspec = lambda r, c: pl.BlockSpec((1, r, c), lambda e, b, j, idx: (e, 0, 0))
    return pl.pallas_call(
        functools.partial(_ffn_kernel, tc, n_steps),
        grid_spec=pltpu.PrefetchScalarGridSpec(
            num_scalar_prefetch=1,
            grid=(N_EXPERTS, B, nblk),
            in_specs=[pl.BlockSpec(memory_space=pl.ANY),
                      wspec(D_MODEL, D_EXPERT), wspec(D_MODEL, D_EXPERT),
                      wspec(D_EXPERT, D_MODEL)],
            out_specs=pl.BlockSpec(memory_space=pl.ANY),
            scratch_shapes=[pltpu.VMEM((2, tc, HA_W), F32), pltpu.VMEM((2, tc, D_MODEL), F32),
                            pltpu.SemaphoreType.DMA((2,)), pltpu.SemaphoreType.DMA((2,))]),
        out_shape=jax.ShapeDtypeStruct(ha.shape, F32),
        input_output_aliases={1: 0},
        compiler_params=_params("arbitrary", "arbitrary", "arbitrary"),
        name="expert_ffn",
    )(idx_steps, ha, wg, wu, wd)


def _ln_kernel(x_ref, g_ref, b_ref, o_ref):
    o_ref[...] = _layer_norm(x_ref[...], g_ref[...], b_ref[...])


def _ln_rows(xin, g, b, tm=512):
    T = xin.shape[0]
    const = lambda i: (0, 0)
    return pl.pallas_call(
        _ln_kernel,
        grid=(T // tm,),
        in_specs=[pl.BlockSpec((tm, D_MODEL), lambda i: (i, 0)),
                  pl.BlockSpec((1, D_MODEL), const), pl.BlockSpec((1, D_MODEL), const)],
        out_specs=pl.BlockSpec((tm, D_MODEL), lambda i: (i, 0)),
        out_shape=jax.ShapeDtypeStruct((T, D_MODEL), F32),
        compiler_params=_params("parallel"),
        name="final_ln",
    )(xin, g, b)


def _pad_heads(w, heads, width):
    lead = w.shape[:-1]
    w = w.reshape(lead + (heads, width))
    w = jnp.pad(w, [(0, 0)] * len(lead) + [(0, 0), (0, HEAD_PAD - width)])
    return w.reshape(lead + (heads * HEAD_PAD,))


def _rot_half(w):
    half = w.shape[-1] // 2
    return jnp.concatenate([-w[..., half:], w[..., :half]], axis=-1)


def _pack_w_in(w):
    o = 0
    parts = {}
    for name, width in (("pool", POOL_DIM), ("cq", MLA_Q_RANK), ("ckv", MLA_KV_RANK),
                        ("kr", MLA_ROPE), ("gq", GLA_HEADS * GLA_DK), ("gk", GLA_HEADS * GLA_DK),
                        ("gv", GLA_HEADS * GLA_DV), ("gr", GLA_HEADS * GLA_DV),
                        ("gd", 2 * GLA_GATE_RANK), ("gates", 3 * D_MODEL)):
        parts[name] = w[:, o:o + width]
        o += width
    rope_slot = lambda m: jnp.pad(m, ((0, 0), (MLA_NOPE, HEAD_PAD - MLA_NOPE - MLA_ROPE)))
    kr2 = jnp.concatenate([rope_slot(parts["kr"]), rope_slot(_rot_half(parts["kr"]))], axis=1)
    gd = jnp.pad(parts["gd"], ((0, 0), (0, LANES - 2 * GLA_GATE_RANK)))
    cols = [parts["pool"], parts["cq"], parts["ckv"], kr2,
            _pad_heads(parts["gq"], GLA_HEADS, GLA_DK), _pad_heads(parts["gk"], GLA_HEADS, GLA_DK),
            parts["gv"], parts["gr"], gd, parts["gates"]]
    return jnp.concatenate(cols, axis=1).astype(BF16)


def _pack_mla(w_uq, w_ukv):
    r = w_uq.shape[0]
    uq = w_uq.reshape(r, MLA_HEADS, MLA_NOPE + MLA_ROPE)
    nope, rope = uq[..., :MLA_NOPE], uq[..., MLA_NOPE:]
    tail = jnp.zeros((r, MLA_HEADS, HEAD_PAD - MLA_NOPE - MLA_ROPE), F32)
    wq = jnp.concatenate([nope, rope, tail], axis=-1).reshape(r, -1)
    wq2 = jnp.concatenate([jnp.zeros_like(nope), _rot_half(rope), tail], axis=-1).reshape(r, -1)
    rk = w_ukv.shape[0]
    ukv = w_ukv.reshape(rk, MLA_HEADS, MLA_NOPE + MLA_V)
    wk = _pad_heads(ukv[..., :MLA_NOPE].reshape(rk, -1), MLA_HEADS, MLA_NOPE)
    wv = _pad_heads(ukv[..., MLA_NOPE:].reshape(rk, -1), MLA_HEADS, MLA_V)
    return wq.astype(BF16), wq2.astype(BF16), wk.astype(BF16), wv.astype(BF16)


def _pack_gla_decay(w_dec, b_dec):
    wd = []
    for d in range(2):
        rows = _pad_heads(w_dec[d], GLA_HEADS, GLA_DK)
        wd.append(jnp.pad(rows, ((d * GLA_GATE_RANK, LANES - (d + 1) * GLA_GATE_RANK), (0, 0))))
    bd = [_pad_heads(b_dec[d][None, :], GLA_HEADS, GLA_DK) for d in range(2)]
    return wd, bd


def _chunk_tri(tm, reverse):
    r = jnp.arange(tm)[:, None]
    c = jnp.arange(tm)[None, :]
    same = (r // GLA_CHUNK) == (c // GLA_CHUNK)
    return (same & ((c >= r) if reverse else (c <= r))).astype(BF16)


def kernel(x, positions, ln0_g, ln0_b, w_in, b_gate, pool_w, pool_scale, w_up_a, mla_q_norm,
           mla_w_uq, mla_kv_norm, mla_w_ukv, w_up_b, gla_w_dec, gla_b_dec, gla_norm, w_up_c,
           w_out, ln1_g, ln1_b, router_w, exp_w_gate, exp_w_up, exp_w_down, ln2_g, ln2_b):
    B, S, D = x.shape
    assert D == D_MODEL and S % 512 == 0 and B == 2
    T = B * S
    C = CAPACITY_FACTOR * S // N_EXPERTS
    tc = min(512, C)
    row = lambda v: v.reshape(1, -1).astype(F32)

    half = MLA_ROPE // 2
    freqs = ROPE_THETA ** (-jnp.arange(half, dtype=F32) / half)
    lanes = jnp.arange(LANES)
    in_rope = (lanes >= MLA_NOPE) & (lanes < MLA_NOPE + MLA_ROPE)
    freq_row = jnp.where(in_rope, freqs[(lanes - MLA_NOPE) % half], 0.0).reshape(1, LANES)
    rope_mask = in_rope.astype(F32).reshape(1, LANES)
    nope_row = (lanes < MLA_NOPE).astype(F32).reshape(1, LANES)
    one_row = (lanes == MLA_V).astype(F32).reshape(1, LANES)
    cos_t, sin_t = _rope_tables(positions.reshape(T, 1), freq_row, rope_mask)

    tri_f, tri_b = _chunk_tri(512, False), _chunk_tri(512, True)
    tri_lane = (jnp.arange(LANES)[:, None] <= jnp.arange(LANES)[None, :]).astype(BF16)

    stream, g_in, b_in = x.reshape(T, D), ln0_g, ln0_b
    for l in range(DEPTH):
        h, u, cq, ckv, kr2, gq, gk, gv, gr, gd, gl = _ln_inproj(
            stream, row(g_in), row(b_in), _pack_w_in(w_in[l]))
        wq, wq2, wk, wv = _pack_mla(mla_w_uq[l], mla_w_ukv[l])
        q, k, v = _mla_prep(cq, ckv, kr2, cos_t, sin_t, row(mla_q_norm[l]), row(mla_kv_norm[l]),
                            wq, wq2, wk, wv, nope_row, one_row, B, S)
        o_mla, (wg_b, wu_b, wd_b) = _flash(q, k, v, (exp_w_gate, exp_w_up, exp_w_down), l)
        o_mla = o_mla.reshape(T, MLA_HEADS * MLA_V)
        wd, bd = _pack_gla_decay(gla_w_dec[l], gla_b_dec[l])
        gla_o = _gla(gq, gk, gv, gd, gr, wd, bd, tri_f, tri_b, row(gla_norm[l]), B, S)
        rw = jnp.pad(router_w[l], ((0, 0), (0, LANES - N_EXPERTS)))
        ha, afft = _merge(u, o_mla, gla_o, gl, h, pool_w[l].astype(BF16), row(pool_scale[l]),
                          w_up_a[l].astype(BF16), w_up_b[l].astype(BF16),
                          w_up_c[l].astype(BF16), row(b_gate[l]),
                          w_out[l].astype(BF16), row(ln1_g[l]), row(ln1_b[l]), rw, B, S)
        idx = _topk(afft, tri_lane, C)
        idx_steps = (idx + (jnp.arange(B, dtype=jnp.int32) * S)[:, None, None])
        idx_steps = idx_steps.transpose(1, 0, 2).reshape(-1)
        stream = _ffn(ha, idx_steps, wg_b, wu_b, wd_b, B, C, tc)
        g_in, b_in = ln2_g[l], ln2_b[l]
    out = _ln_rows(stream, row(g_in), row(b_in))
    return out.reshape(B, S, D)
```

```python
import functools

import jax
import jax.numpy as jnp
from jax import lax
from jax.experimental import pallas as pl
from jax.experimental.pallas import tpu as pltpu

F32 = jnp.float32
BF16 = jnp.bfloat16

LANES = 128
VMEM_LIMIT = 56 * 1024 * 1024
FLASH_VMEM_LIMIT = 60 * 1024 * 1024

D_MODEL = 1024
DEPTH = 2
POOL_WINDOWS = (2, 4, 8, 16)
POOL_DIM = 512
POOL_HALO = 16
MLA_HEADS = 8
MLA_NOPE = 64
MLA_ROPE = 32
MLA_V = 64
MLA_Q_RANK = 384
MLA_KV_RANK = 256
ROPE_THETA = 10000.0
GLA_HEADS = 4
GLA_DK = 64
GLA_DV = 128
GLA_GATE_RANK = 16
GLA_GATE_NORM = 16.0
GLA_CHUNK = 64
N_EXPERTS = 16
CAPACITY_FACTOR = 2
D_EXPERT = 2048
DN_ALPHA = (2 * DEPTH) ** 0.25
LN_EPS = 1e-5
RMS_EPS = 1e-6
LOG2_E = 1.4426950408889634
NEVER = 1e9

HEAD_PAD = LANES
GLA_W = GLA_HEADS * HEAD_PAD
HA_W = 2 * D_MODEL + LANES
DMA_UNROLL = 16

_SEG = (("pool", POOL_DIM), ("cq", MLA_Q_RANK), ("ckv", MLA_KV_RANK), ("kr2", 2 * HEAD_PAD),
        ("gq", GLA_W), ("gk", GLA_W), ("gv", GLA_W), ("gr", GLA_W), ("gd", LANES),
        ("gates", 3 * D_MODEL))
_SEG_OFF = {}
_off = 0
for _n, _w in _SEG:
    _SEG_OFF[_n] = (_off, _w)
    _off += _w
W_ALL_COLS = _off


def _params(*sem):
    return pltpu.CompilerParams(dimension_semantics=sem, vmem_limit_bytes=VMEM_LIMIT)


def _dot(a, b):
    return jnp.dot(a, b, preferred_element_type=F32)


def _dot_nt(a, b):
    return lax.dot_general(a, b, (((1,), (1,)), ((), ())), preferred_element_type=F32)


def _split(x):
    hi = x.astype(BF16)
    lo = (x - hi.astype(F32)).astype(BF16)
    return hi, lo


def _dot3(a, b, nt=False):
    d = _dot_nt if nt else _dot
    ah, al = _split(a)
    bh, bl = _split(b)
    return d(ah, bh) + d(ah, bl) + d(al, bh)


def _layer_norm(x, g, b):
    mu = jnp.mean(x, axis=-1, keepdims=True)
    xc = x - mu
    var = jnp.mean(xc * xc, axis=-1, keepdims=True)
    return xc * lax.rsqrt(var + LN_EPS) * g + b


def _rms_norm(x, g):
    ms = jnp.mean(x * x, axis=-1, keepdims=True)
    return x * lax.rsqrt(ms + RMS_EPS) * g


def _rope_kernel(pos_ref, freq_ref, mask_ref, cos_ref, sin_ref):
    ang = pos_ref[...].astype(F32) * freq_ref[...]
    cos_ref[...] = jnp.cos(ang) * mask_ref[...]
    sin_ref[...] = jnp.sin(ang) * mask_ref[...]


def _rope_tables(pos_col, freq_row, mask_row, tm=512):
    T = pos_col.shape[0]
    row = pl.BlockSpec((1, LANES), lambda i: (0, 0))
    out = pl.BlockSpec((tm, LANES), lambda i: (i, 0))
    return pl.pallas_call(
        _rope_kernel,
        grid=(T // tm,),
        in_specs=[pl.BlockSpec((tm, 1), lambda i: (i, 0)), row, row],
        out_specs=[out, out],
        out_shape=[jax.ShapeDtypeStruct((T, LANES), F32)] * 2,
        compiler_params=_params("parallel"),
        name="rope_tables",
    )(pos_col, freq_row, mask_row)


def _inproj_kernel(x_ref, xn_ref, g_ref, b_ref, w_ref, h_ref, *rest):
    out_refs, h_scr = rest[:-1], rest[-1]

    @pl.when(pl.program_id(0) == 0)
    def _():
        h_scr[...] = _layer_norm(x_ref[...], g_ref[...], b_ref[...])

    h = h_scr[...]
    h_ref[...] = h
    hb = h.astype(BF16)
    for (name, width), o_ref in zip(_SEG, out_refs):
        off = _SEG_OFF[name][0]
        o_ref[...] = _dot(hb, w_ref[:, off:off + width]).astype(o_ref.dtype)
    h_scr[...] = _layer_norm(xn_ref[...], g_ref[...], b_ref[...])


def _ln_inproj(xin, g, b, w_all, tm=512):
    T = xin.shape[0]
    n_steps = T // tm
    const = lambda i: (0, 0)
    rows = lambda w: pl.BlockSpec((tm, w), lambda i: (i, 0))
    out_shapes = [jax.ShapeDtypeStruct((T, D_MODEL), F32)]
    out_specs = [rows(D_MODEL)]
    for name, width in _SEG:
        out_shapes.append(jax.ShapeDtypeStruct((T, width), F32 if name == "gd" else BF16))
        out_specs.append(rows(width))
    return pl.pallas_call(
        _inproj_kernel,
        grid=(n_steps,),
        in_specs=[rows(D_MODEL),
                  pl.BlockSpec((tm, D_MODEL), lambda i: (jnp.minimum(i + 1, n_steps - 1), 0)),
                  pl.BlockSpec((1, D_MODEL), const), pl.BlockSpec((1, D_MODEL), const),
                  pl.BlockSpec((D_MODEL, W_ALL_COLS), const)],
        out_specs=out_specs,
        out_shape=out_shapes,
        scratch_shapes=[pltpu.VMEM((tm, D_MODEL), F32)],
        compiler_params=_params("arbitrary"),
        name="ln_inproj",
    )(xin, xin, g, b, w_all)


def _mla_prep_kernel(cq_ref, ckv_ref, kr2_ref, cos_ref, sin_ref, qn_ref, kvn_ref,
                     wq_ref, wq2_ref, wk_ref, wv_ref, nope_ref, one_ref,
                     q_ref, k_ref, v_ref):
    scale = (MLA_NOPE + MLA_ROPE) ** -0.5 * LOG2_E
    cosr = cos_ref[...]
    sinr = sin_ref[...]
    nq = _rms_norm(cq_ref[...].astype(F32), qn_ref[...]).astype(BF16)
    nkv = _rms_norm(ckv_ref[...].astype(F32), kvn_ref[...]).astype(BF16)
    qa = _dot(nq, wq_ref[...])
    qb = _dot(nq, wq2_ref[...])
    ka = _dot(nkv, wk_ref[...])
    va = _dot(nkv, wv_ref[...])
    kr2 = kr2_ref[...].astype(F32)
    k_rope = kr2[:, :HEAD_PAD] * cosr + kr2[:, HEAD_PAD:] * sinr
    q_cos = (cosr + nope_ref[...]) * scale
    q_sin = sinr * scale
    for h in range(MLA_HEADS):
        sl = slice(h * HEAD_PAD, (h + 1) * HEAD_PAD)
        q_ref[0, h] = (qa[:, sl] * q_cos + qb[:, sl] * q_sin).astype(BF16)
        k_ref[0, h] = (ka[:, sl] + k_rope).astype(BF16)
        v_ref[0, h] = (va[:, sl] + one_ref[...]).astype(BF16)


def _mla_prep(cq, ckv, kr2, cos_t, sin_t, qn, kvn, wq, wq2, wk, wv, nope_row, one_row, B, S,
              tm=512):
    nb = S // tm
    rows = lambda w: pl.BlockSpec((tm, w), lambda b, i: (b * nb + i, 0))
    const = lambda r, c: pl.BlockSpec((r, c), lambda b, i: (0, 0))
    hw = MLA_HEADS * HEAD_PAD
    out = pl.BlockSpec((1, MLA_HEADS, tm, HEAD_PAD), lambda b, i: (b, 0, i, 0))
    return pl.pallas_call(
        _mla_prep_kernel,
        grid=(B, nb),
        in_specs=[rows(MLA_Q_RANK), rows(MLA_KV_RANK), rows(2 * HEAD_PAD), rows(LANES),
                  rows(LANES), const(1, MLA_Q_RANK), const(1, MLA_KV_RANK),
                  const(MLA_Q_RANK, hw), const(MLA_Q_RANK, hw), const(MLA_KV_RANK, hw),
                  const(MLA_KV_RANK, hw), const(1, LANES), const(1, LANES)],
        out_specs=[out, out, out],
        out_shape=[jax.ShapeDtypeStruct((B, MLA_HEADS, S, HEAD_PAD), BF16)] * 3,
        compiler_params=_params("parallel", "parallel"),
        name="mla_prep",
    )(cq, ckv, kr2, cos_t, sin_t, qn, kvn, wq, wq2, wk, wv, nope_row, one_row)


def _flash_kernel(q_ref, k_ref, v_ref, wg_ref, wu_ref, wd_ref, o_ref, wgb_ref, wub_ref, wdb_ref,
                  m_scr, acc_scr):
    kv = pl.program_id(2)
    wgb_ref[...] = wg_ref[0].astype(BF16)
    wub_ref[...] = wu_ref[0].astype(BF16)
    wdb_ref[...] = wd_ref[0].astype(BF16)

    @pl.when(kv == 0)
    def _():
        m_scr[...] = jnp.full(m_scr.shape, -jnp.inf, F32)
        acc_scr[...] = jnp.zeros(acc_scr.shape, F32)

    tk = k_ref.shape[2]
    for h in range(MLA_HEADS):
        s = _dot_nt(q_ref[0, h], k_ref[0, h])
        m_prev = m_scr[h]
        m_new = jnp.maximum(m_prev, jnp.max(s, axis=1, keepdims=True))
        p = jnp.exp2(s - jnp.concatenate([m_new] * (tk // LANES), axis=1))
        alpha = jnp.exp2(m_prev - m_new)
        acc_scr[h] = alpha * acc_scr[h] + _dot(p.astype(BF16), v_ref[0, h])
        m_scr[h] = m_new

    @pl.when(kv == pl.num_programs(2) - 1)
    def _():
        lane = lax.broadcasted_iota(jnp.int32, (q_ref.shape[2], HEAD_PAD), 1)
        for hp in range(MLA_HEADS // 2):
            a0 = acc_scr[2 * hp]
            a1 = acc_scr[2 * hp + 1]
            o0 = a0 / a0[:, MLA_V:MLA_V + 1]
            o1 = a1 / a1[:, MLA_V:MLA_V + 1]
            pair = jnp.where(lane < MLA_V, o0, pltpu.roll(o1, MLA_V, axis=1))
            o_ref[0, :, hp * HEAD_PAD:(hp + 1) * HEAD_PAD] = pair.astype(BF16)


def _flash(q, k, v, expert_w, layer, tq=1024, tk=2048):
    B, H, S, _ = q.shape
    tq, tk = min(tq, S), min(tk, S)
    assert S % tq == 0 and S % tk == 0
    nq, nk = S // tq, S // tk
    n_steps = B * nq * nk
    qspec = pl.BlockSpec((1, H, tq, HEAD_PAD), lambda b, i, j: (b, 0, i, 0))
    kspec = pl.BlockSpec((1, H, tk, HEAD_PAD), lambda b, i, j: (b, 0, j, 0))
    w_in, w_specs, wb_specs, wb_shapes = [], [], [], []
    for w in expert_w:
        n_l, n_e, r, c = w.shape
        slab = n_e * r // n_steps
        assert n_e * r % n_steps == 0 and slab % 16 == 0
        w_in.append(w.reshape(n_l, n_steps, slab, c))
        w_specs.append(pl.BlockSpec((1, 1, slab, c),
                                    lambda b, i, j: (layer, (b * nq + i) * nk + j, 0, 0)))
        wb_specs.append(pl.BlockSpec((1, slab, c), lambda b, i, j: ((b * nq + i) * nk + j, 0, 0)))
        wb_shapes.append(jax.ShapeDtypeStruct((n_steps, slab, c), BF16))
    o, *wb = pl.pallas_call(
        _flash_kernel,
        grid=(B, nq, nk),
        in_specs=[qspec, kspec, kspec] + w_specs,
        out_specs=[pl.BlockSpec((1, tq, H * MLA_V), lambda b, i, j: (b, i, 0))] + wb_specs,
        out_shape=[jax.ShapeDtypeStruct((B, S, H * MLA_V), BF16)] + wb_shapes,
        scratch_shapes=[pltpu.VMEM((H, tq, LANES), F32), pltpu.VMEM((H, tq, HEAD_PAD), F32)],
        compiler_params=pltpu.CompilerParams(
            dimension_semantics=("parallel", "parallel", "arbitrary"),
            vmem_limit_bytes=FLASH_VMEM_LIMIT),
        name="mla_flash",
    )(q, k, v, *w_in)
    return o, [b16.reshape(w.shape[1:]) for b16, w in zip(wb, expert_w)]


def _log_sigmoid(x):
    return jnp.minimum(x, 0.0) - jnp.log(1.0 + jnp.exp(-jnp.abs(x)))


def _gla_scan(reverse, gq_ref, gk_ref, gv_ref, gd_ref, wd_ref, bd_ref, tri_ref, state_scr,
              o_scr):
    tm = gq_ref.shape[0]
    L = GLA_CHUNK
    n_chunks = tm // L

    @pl.when(pl.program_id(1) == 0)
    def _():
        state_scr[...] = jnp.zeros(state_scr.shape, F32)

    logits = _dot(gd_ref[...].astype(BF16), wd_ref[...]) + bd_ref[...]
    g = _log_sigmoid(logits) * (1.0 / GLA_GATE_NORM)
    g_hi, g_lo = _split(g)
    b = _dot(tri_ref[...], g_hi) + _dot(tri_ref[...], g_lo)
    edge = 0 if reverse else L - 1
    b_last = jnp.concatenate(
        [jnp.broadcast_to(b[c * L + edge:c * L + edge + 1, :], (L, GLA_W))
         for c in range(n_chunks)], axis=0)
    q_in = (gq_ref[...].astype(F32) * (GLA_DK ** -0.5) * jnp.exp(b)).astype(BF16)
    k_f = gk_ref[...].astype(F32)
    k_in = (k_f * jnp.exp(-b)).astype(BF16)
    k_st = (k_f * jnp.exp(b_last - b)).astype(BF16)
    dec = jnp.exp(b_last)
    v = gv_ref[...]
    lane = lax.broadcasted_iota(jnp.int32, (HEAD_PAD, 2 * L), 1)
    order = range(n_chunks - 1, -1, -1) if reverse else range(n_chunks)
    heads = [slice(h * HEAD_PAD, (h + 1) * HEAD_PAD) for h in range(GLA_HEADS)]
    kv = {}
    for h, hs in enumerate(heads):
        a = _dot_nt(q_in[:, hs], k_in[:, hs]).astype(BF16) * tri_ref[...]
        o_scr[:, hs] = _dot(a, v[:, hs])
        v_t = v[:, hs].astype(F32).T
        for c in range(n_chunks):
            pair = slice((c // 2) * 2 * L, (c // 2 + 1) * 2 * L)
            in_chunk = (lane >= L) if c % 2 else (lane < L)
            v_tc = jnp.where(in_chunk, v_t[:, pair], 0.0).astype(BF16)
            kv[c, h] = _dot(v_tc, k_st[pair, hs])
    entering = {}
    for h, hs in enumerate(heads):
        st = state_scr[h]
        for c in order:
            entering[c, h] = st.astype(BF16)
            st = dec[c * L:c * L + 1, hs] * st + kv[c, h]
        state_scr[h] = st
    for c in range(n_chunks):
        rs = slice(c * L, (c + 1) * L)
        for h, hs in enumerate(heads):
            o_scr[rs, hs] += _dot_nt(q_in[rs, hs], entering[c, h])


def _gla_fwd_kernel(gq_ref, gk_ref, gv_ref, gd_ref, wd_ref, bd_ref, tri_ref, o_ref, state_scr,
                    o_scr):
    _gla_scan(False, gq_ref, gk_ref, gv_ref, gd_ref, wd_ref, bd_ref, tri_ref, state_scr, o_scr)
    o_ref[...] = o_scr[...]


def _gla_bwd_kernel(gq_ref, gk_ref, gv_ref, gd_ref, wd_ref, bd_ref, tri_ref, of_ref, gr_ref,
                    ng_ref, out_ref, state_scr, o_scr):
    _gla_scan(True, gq_ref, gk_ref, gv_ref, gd_ref, wd_ref, bd_ref, tri_ref, state_scr, o_scr)
    r = gr_ref[...].astype(F32)
    gate = r * jax.nn.sigmoid(r)
    for h in range(GLA_HEADS):
        hs = slice(h * HEAD_PAD, (h + 1) * HEAD_PAD)
        o = _rms_norm(of_ref[:, hs] + o_scr[:, hs], ng_ref[...])
        out_ref[:, hs] = (o * gate[:, hs]).astype(BF16)


def _gla(gq, gk, gv, gd, gr, wd, bd, tri_f, tri_b, norm_g, B, S, tm=512):
    nb = S // tm
    T = B * S

    def call(reverse):
        blk = (lambda b, i: (b * nb + nb - 1 - i, 0)) if reverse else (lambda b, i: (b * nb + i, 0))
        rows = lambda w: pl.BlockSpec((tm, w), blk)
        const = lambda r, c: pl.BlockSpec((r, c), lambda b, i: (0, 0))
        in_specs = [rows(GLA_W), rows(GLA_W), rows(GLA_W), rows(LANES),
                    const(LANES, GLA_W), const(1, GLA_W), const(tm, tm)]
        scratch = [pltpu.VMEM((GLA_HEADS, GLA_DV, HEAD_PAD), F32), pltpu.VMEM((tm, GLA_W), F32)]
        d = 1 if reverse else 0
        args = [gq, gk, gv, gd, wd[d], bd[d], tri_b if reverse else tri_f]
        if reverse:
            in_specs += [rows(GLA_W), rows(GLA_W), const(1, GLA_DV)]
            args += [o_f, gr, norm_g]
        return pl.pallas_call(
            _gla_bwd_kernel if reverse else _gla_fwd_kernel,
            grid=(B, nb),
            in_specs=in_specs,
            out_specs=rows(GLA_W),
            out_shape=jax.ShapeDtypeStruct((T, GLA_W), BF16 if reverse else F32),
            scratch_shapes=scratch,
            compiler_params=_params("parallel", "arbitrary"),
            name="gla_bwd" if reverse else "gla_fwd",
        )(*args)

    o_f = call(False)
    return call(True)


def _merge_kernel(S, u_ref, up_ref, un_ref, om_ref, gla_ref, gl_ref, h_ref,
                  pw_ref, ps_ref, wa_ref, wb_ref, wc_ref, bg_ref, wo_ref, g_ref, b_ref,
                  rw_ref, ha_ref, afft_ref):
    tm = u_ref.shape[0]
    i = pl.program_id(1)
    ext = jnp.concatenate([up_ref[...], u_ref[...], un_ref[...]], axis=0).astype(F32)
    n_ext = tm + 2 * POOL_HALO
    pos_ext = i * tm - POOL_HALO + lax.broadcasted_iota(jnp.int32, (n_ext, LANES), 0)
    in_seq = (pos_ext >= 0) & (pos_ext < S)
    pos = i * tm + lax.broadcasted_iota(jnp.int32, (tm, LANES), 0)
    core = slice(POOL_HALO, POOL_HALO + tm)

    def rows_at(a, d):
        return pltpu.roll(a, (-d) % n_ext, axis=0)

    pooled = []
    for gi, w in enumerate(POOL_WINDOWS):
        hw = w // 2
        cs = slice(gi * LANES, (gi + 1) * LANES)
        x = jnp.where(in_seq, ext[:, cs], 0.0)
        win = rows_at(x, -1) + x
        reach = 1
        while reach < hw:
            win = rows_at(win, -reach) + rows_at(win, reach)
            reach *= 2
        cnt = (jnp.minimum(pos + hw, S) - jnp.maximum(pos - hw, 0)).astype(F32)
        pg = (win[core] / cnt - x[core]).astype(BF16)
        pooled.append(_dot(pg, pw_ref[gi]))
    pa = (jnp.concatenate(pooled, axis=1) * ps_ref[...]).astype(BF16)
    y_a = _dot(pa, wa_ref[...])
    y_b = _dot(om_ref[...], wb_ref[...])
    y_c = _dot(gla_ref[...], wc_ref[...])
    gates2 = 1.0 + jnp.tanh(0.5 * (gl_ref[...].astype(F32) + bg_ref[...]))
    merged2 = (gates2[:, :D_MODEL] * y_a + gates2[:, D_MODEL:2 * D_MODEL] * y_b
               + gates2[:, 2 * D_MODEL:] * y_c)
    mix = _dot(merged2.astype(BF16), wo_ref[...])
    h1 = _layer_norm(DN_ALPHA * h_ref[...] + mix, g_ref[...], b_ref[...])
    lane = lax.broadcasted_iota(jnp.int32, (tm, LANES), 1)
    logits = jnp.where(lane < N_EXPERTS, _dot3(h1, rw_ref[...]), -jnp.inf)
    e = jnp.exp(logits - jnp.max(logits, axis=1, keepdims=True))
    aff = e / jnp.sum(e, axis=1, keepdims=True)
    afft_ref[0] = aff.T[:N_EXPERTS]
    ha_ref[:, :D_MODEL] = DN_ALPHA * h1
    ha_ref[:, D_MODEL:2 * D_MODEL] = h1
    ha_ref[:, 2 * D_MODEL:] = aff


def _merge(u, o_mla, gla_o, gl, h, pw, ps, wa, wb, wc, bg, wo, g, b, rw, B, S, tm=512):
    nb = S // tm
    T = B * S
    hb = tm // POOL_HALO
    n_halo = T // POOL_HALO
    rows = lambda w: pl.BlockSpec((tm, w), lambda bb, i: (bb * nb + i, 0))
    prev = pl.BlockSpec((POOL_HALO, POOL_DIM),
                        lambda bb, i: (jnp.maximum((bb * nb + i) * hb - 1, 0), 0))
    nxt = pl.BlockSpec((POOL_HALO, POOL_DIM),
                       lambda bb, i: (jnp.minimum((bb * nb + i + 1) * hb, n_halo - 1), 0))
    c2 = lambda r, c: pl.BlockSpec((r, c), lambda bb, i: (0, 0))
    c3 = lambda a, r, c: pl.BlockSpec((a, r, c), lambda bb, i: (0, 0, 0))
    return pl.pallas_call(
        functools.partial(_merge_kernel, S),
        grid=(B, nb),
        in_specs=[rows(POOL_DIM), prev, nxt,
                  rows(MLA_HEADS * MLA_V), rows(GLA_W), rows(3 * D_MODEL), rows(D_MODEL),
                  c3(len(POOL_WINDOWS), LANES, LANES), c2(1, POOL_DIM), c2(POOL_DIM, D_MODEL),
                  c2(MLA_HEADS * MLA_V, D_MODEL), c2(GLA_W, D_MODEL), c2(1, 3 * D_MODEL),
                  c2(D_MODEL, D_MODEL), c2(1, D_MODEL), c2(1, D_MODEL), c2(D_MODEL, LANES)],
        out_specs=[rows(HA_W),
                   pl.BlockSpec((1, N_EXPERTS, tm), lambda bb, i: (bb, 0, i))],
        out_shape=[jax.ShapeDtypeStruct((T, HA_W), F32),
                   jax.ShapeDtypeStruct((B, N_EXPERTS, S), F32)],
        compiler_params=_params("parallel", "parallel"),
        name="merge",
    )(u, u, u, o_mla, gla_o, gl, h, pw, ps, wa, wb, wc, bg, wo, g, b, rw)


def _topk_kernel(C, aff_ref, tri_ref, idx_ref, p_scr):
    S = aff_ref.shape[2]
    n_chunks = S // LANES
    aff = aff_ref[0]

    def count(mask):
        return jnp.sum(jnp.where(mask, 1.0, 0.0), axis=1, keepdims=True)

    def as_float(bits):
        return lax.bitcast_convert_type(bits, F32)

    def refine(i, thr):
        cand = thr | jnp.left_shift(jnp.int32(1), 30 - i)
        return jnp.where(count(aff >= as_float(cand)) >= C, cand, thr)

    thr = lax.fori_loop(0, 31, refine, jnp.zeros((N_EXPERTS, 1), jnp.int32))
    above = aff >= as_float(thr + 1)
    tied = (aff >= as_float(thr)) & jnp.logical_not(above)
    need = C - count(above)
    tri = tri_ref[...]

    tied_f = tied.astype(F32)
    run = jnp.zeros((N_EXPERTS, 1), F32)
    sel_parts = []
    for c in range(n_chunks):
        cs = slice(c * LANES, (c + 1) * LANES)
        incl = _dot(tied_f[:, cs].astype(BF16), tri) + run
        run = incl[:, LANES - 1:LANES]
        sel_parts.append(jnp.where(above[:, cs] | (tied[:, cs] & (incl <= need)), 1.0, 0.0))

    p_scr[...] = jnp.zeros(p_scr.shape, F32)
    lane_e = lax.broadcasted_iota(jnp.int32, (N_EXPERTS, LANES), 1)
    chunk_end = jnp.full((N_EXPERTS, LANES), NEVER, F32)
    run = jnp.zeros((N_EXPERTS, 1), F32)
    for c in range(n_chunks):
        rel = _dot(sel_parts[c].astype(BF16), tri)
        p_scr[:, c, :] = rel
        run = run + rel[:, LANES - 1:LANES]
        chunk_end = jnp.where(lane_e == c, run, chunk_end)

    slot = lax.broadcasted_iota(jnp.int32, (C, LANES), 0).astype(F32)
    lane_c = lax.broadcasted_iota(jnp.int32, (C, LANES), 1).astype(F32)
    ones = jnp.ones((8, LANES), BF16)
    for e in range(N_EXPERTS):
        ce = chunk_end[e:e + 1, :]
        full = ce <= slot
        n_full = jnp.sum(jnp.where(full, 1.0, 0.0), axis=1, keepdims=True)
        base = jnp.max(jnp.where(full, ce, 0.0), axis=1, keepdims=True)
        pick = jnp.where(lane_c == n_full, 1.0, 0.0).astype(BF16)
        rel = _dot(pick, p_scr[e].astype(BF16))
        w = jnp.where(rel <= slot - base, 1.0, 0.0) + jnp.where(full, float(LANES), 0.0)
        tok = _dot_nt(ones, w.astype(BF16))[0:1, :]
        idx_ref[0, e:e + 1, :] = tok.astype(jnp.int32)


def _topk(afft, tri, C):
    B, E, S = afft.shape
    assert S // LANES <= LANES
    return pl.pallas_call(
        functools.partial(_topk_kernel, C),
        grid=(B,),
        in_specs=[pl.BlockSpec((1, E, S), lambda b: (b, 0, 0)),
                  pl.BlockSpec((LANES, LANES), lambda b: (0, 0))],
        out_specs=pl.BlockSpec((1, E, C), lambda b: (b, 0, 0)),
        out_shape=jax.ShapeDtypeStruct((B, E, C), jnp.int32),
        scratch_shapes=[pltpu.VMEM((E, LANES, LANES), F32)],
        compiler_params=_params("parallel"),
        name="expert_choice",
    )(afft, tri)


def _ffn_kernel(tc, n_steps, idx_ref, ha_in_ref, wg_ref, wu_ref, wd_ref, ha_ref,
                gbuf, sbuf, gsem, ssem):
    del ha_in_ref
    e = pl.program_id(0)
    s = (e * pl.num_programs(1) + pl.program_id(1)) * pl.num_programs(2) + pl.program_id(2)
    slot = s % 2

    def gather_copy(step, r, sl):
        row = idx_ref[step * tc + r]
        return pltpu.make_async_copy(ha_ref.at[pl.ds(row, 1), :],
                                     gbuf.at[sl, pl.ds(r, 1), :], gsem.at[sl])

    def scatter_copy(step, r, sl):
        row = idx_ref[step * tc + r]
        return pltpu.make_async_copy(sbuf.at[sl, pl.ds(r, 1), :],
                                     ha_ref.at[pl.ds(row, 1), pl.ds(0, D_MODEL)], ssem.at[sl])

    def start_rows(copy, step, sl):
        def body(r, carry):
            copy(step, r, sl).start()
            return carry
        lax.fori_loop(0, tc, body, 0, unroll=DMA_UNROLL)

    def wait_gather(sl):
        pltpu.make_async_copy(ha_ref.at[pl.ds(0, tc), :], gbuf.at[sl], gsem.at[sl]).wait()

    def wait_scatter(sl):
        pltpu.make_async_copy(sbuf.at[sl], ha_ref.at[pl.ds(0, tc), pl.ds(0, D_MODEL)],
                              ssem.at[sl]).wait()

    @pl.when(s == 0)
    def _():
        start_rows(gather_copy, s, slot)

    @pl.when(s >= 2)
    def _():
        wait_scatter(slot)

    @pl.when(s + 1 < n_steps)
    def _():
        start_rows(gather_copy, s + 1, 1 - slot)

    wait_gather(slot)

    rows = gbuf[slot]
    x = rows[:, D_MODEL:2 * D_MODEL].astype(BF16)
    lane = lax.broadcasted_iota(jnp.int32, (tc, LANES), 1)
    gate = jnp.sum(jnp.where(lane == e, rows[:, 2 * D_MODEL:], 0.0), axis=1, keepdims=True)
    y = jnp.zeros((tc, D_MODEL), F32)
    fh = D_EXPERT // 2
    for f in range(2):
        fs = slice(f * fh, (f + 1) * fh)
        hg = _dot(x, wg_ref[0, :, fs])
        hu = _dot(x, wu_ref[0, :, fs])
        hid = (hg * jax.nn.sigmoid(hg) * hu).astype(BF16)
        y += _dot(hid, wd_ref[0, fs, :])
    sbuf[slot] = rows[:, :D_MODEL] + gate * y
    start_rows(scatter_copy, s, slot)

    @pl.when(s == n_steps - 1)
    def _():
        if n_steps >= 2:
            wait_scatter(1 - slot)
        wait_scatter(slot)


def _ffn(ha, idx_steps, wg, wu, wd, B, C, tc=512):
    nblk = C // tc
    n_steps = N_EXPERTS * B * nblk
    wspec = lambda r, c: pl.BlockSpec((1, r, c), lambda e, b, j, idx: (e, 0, 0))
    return pl.pallas_call(
        functools.partial(_ffn_kernel, tc, n_steps),
        grid_spec=pltpu.PrefetchScalarGridSpec(
            num_scalar_prefetch=1,
            grid=(N_EXPERTS, B, nblk),
            in_specs=[pl.BlockSpec(memory_space=pl.ANY),
                      wspec(D_MODEL, D_EXPERT), wspec(D_MODEL, D_EXPERT),
                      wspec(D_EXPERT, D_MODEL)],
            out_specs=pl.BlockSpec(memory_space=pl.ANY),
            scratch_shapes=[pltpu.VMEM((2, tc, HA_W), F32), pltpu.VMEM((2, tc, D_MODEL), F32),
                            pltpu.SemaphoreType.DMA((2,)), pltpu.SemaphoreType.DMA((2,))]),
        out_shape=jax.ShapeDtypeStruct(ha.shape, F32),
        input_output_aliases={1: 0},
        compiler_params=_params("arbitrary", "arbitrary", "arbitrary"),
        name="expert_ffn",
    )(idx_steps, ha, wg, wu, wd)


def _ln_kernel(x_ref, g_ref, b_ref, o_ref):
    o_ref[...] = _layer_norm(x_ref[...], g_ref[...], b_ref[...])


def _ln_rows(xin, g, b, tm=512):
    T = xin.shape[0]
    const = lambda i: (0, 0)
    return pl.pallas_call(
        _ln_kernel,
        grid=(T // tm,),
        in_specs=[pl.BlockSpec((tm, D_MODEL), lambda i: (i, 0)),
                  pl.BlockSpec((1, D_MODEL), const), pl.BlockSpec((1, D_MODEL), const)],
        out_specs=pl.BlockSpec((tm, D_MODEL), lambda i: (i, 0)),
        out_shape=jax.ShapeDtypeStruct((T, D_MODEL), F32),
        compiler_params=_params("parallel"),
        name="final_ln",
    )(xin, g, b)


def _pad_heads(w, heads, width):
    lead = w.shape[:-1]
    w = w.reshape(lead + (heads, width))
    w = jnp.pad(w, [(0, 0)] * len(lead) + [(0, 0), (0, HEAD_PAD - width)])
    return w.reshape(lead + (heads * HEAD_PAD,))


def _rot_half(w):
    half = w.shape[-1] // 2
    return jnp.concatenate([-w[..., half:], w[..., :half]], axis=-1)


def _pack_w_in(w):
    o = 0
    parts = {}
    for name, width in (("pool", POOL_DIM), ("cq", MLA_Q_RANK), ("ckv", MLA_KV_RANK),
                        ("kr", MLA_ROPE), ("gq", GLA_HEADS * GLA_DK), ("gk", GLA_HEADS * GLA_DK),
                        ("gv", GLA_HEADS * GLA_DV), ("gr", GLA_HEADS * GLA_DV),
                        ("gd", 2 * GLA_GATE_RANK), ("gates", 3 * D_MODEL)):
        parts[name] = w[:, o:o + width]
        o += width
    rope_slot = lambda m: jnp.pad(m, ((0, 0), (MLA_NOPE, HEAD_PAD - MLA_NOPE - MLA_ROPE)))
    kr2 = jnp.concatenate([rope_slot(parts["kr"]), rope_slot(_rot_half(parts["kr"]))], axis=1)
    gd = jnp.pad(parts["gd"], ((0, 0), (0, LANES - 2 * GLA_GATE_RANK)))
    cols = [parts["pool"], parts["cq"], parts["ckv"], kr2,
            _pad_heads(parts["gq"], GLA_HEADS, GLA_DK), _pad_heads(parts["gk"], GLA_HEADS, GLA_DK),
            parts["gv"], parts["gr"], gd, parts["gates"]]
    return jnp.concatenate(cols, axis=1).astype(BF16)


def _pack_mla(w_uq, w_ukv):
    r = w_uq.shape[0]
    uq = w_uq.reshape(r, MLA_HEADS, MLA_NOPE + MLA_ROPE)
    nope, rope = uq[..., :MLA_NOPE], uq[..., MLA_NOPE:]
    tail = jnp.zeros((r, MLA_HEADS, HEAD_PAD - MLA_NOPE - MLA_ROPE), F32)
    wq = jnp.concatenate([nope, rope, tail], axis=-1).reshape(r, -1)
    wq2 = jnp.concatenate([jnp.zeros_like(nope), _rot_half(rope), tail], axis=-1).reshape(r, -1)
    rk = w_ukv.shape[0]
    ukv = w_ukv.reshape(rk, MLA_HEADS, MLA_NOPE + MLA_V)
    wk = _pad_heads(ukv[..., :MLA_NOPE].reshape(rk, -1), MLA_HEADS, MLA_NOPE)
    wv = _pad_heads(ukv[..., MLA_NOPE:].reshape(rk, -1), MLA_HEADS, MLA_V)
    return wq.astype(BF16), wq2.astype(BF16), wk.astype(BF16), wv.astype(BF16)


def _pack_gla_decay(w_dec, b_dec):
    wd = []
    for d in range(2):
        rows = _pad_heads(w_dec[d], GLA_HEADS, GLA_DK)
        wd.append(jnp.pad(rows, ((d * GLA_GATE_RANK, LANES - (d + 1) * GLA_GATE_RANK),
                                 (0, 0))).astype(BF16))
    bd = [_pad_heads(b_dec[d][None, :], GLA_HEADS, GLA_DK) for d in range(2)]
    return wd, bd


def _chunk_tri(tm, reverse):
    r = jnp.arange(tm)[:, None]
    c = jnp.arange(tm)[None, :]
    same = (r // GLA_CHUNK) == (c // GLA_CHUNK)
    return (same & ((c >= r) if reverse else (c <= r))).astype(BF16)


def kernel(x, positions, ln0_g, ln0_b, w_in, b_gate, pool_w, pool_scale, w_up_a, mla_q_norm,
           mla_w_uq, mla_kv_norm, mla_w_ukv, w_up_b, gla_w_dec, gla_b_dec, gla_norm, w_up_c,
           w_out, ln1_g, ln1_b, router_w, exp_w_gate, exp_w_up, exp_w_down, ln2_g, ln2_b):
    B, S, D = x.shape
    assert D == D_MODEL and S % 512 == 0 and B == 2
    T = B * S
    C = CAPACITY_FACTOR * S // N_EXPERTS
    tc = min(512, C)
    row = lambda v: v.reshape(1, -1).astype(F32)

    half = MLA_ROPE // 2
    freqs = ROPE_THETA ** (-jnp.arange(half, dtype=F32) / half)
    lanes = jnp.arange(LANES)
    in_rope = (lanes >= MLA_NOPE) & (lanes < MLA_NOPE + MLA_ROPE)
    freq_row = jnp.where(in_rope, freqs[(lanes - MLA_NOPE) % half], 0.0).reshape(1, LANES)
    rope_mask = in_rope.astype(F32).reshape(1, LANES)
    nope_row = (lanes < MLA_NOPE).astype(F32).reshape(1, LANES)
    one_row = (lanes == MLA_V).astype(F32).reshape(1, LANES)
    cos_t, sin_t = _rope_tables(positions.reshape(T, 1), freq_row, rope_mask)

    tri_f, tri_b = _chunk_tri(512, False), _chunk_tri(512, True)
    tri_lane = (jnp.arange(LANES)[:, None] <= jnp.arange(LANES)[None, :]).astype(BF16)

    stream, g_in, b_in = x.reshape(T, D), ln0_g, ln0_b
    for l in range(DEPTH):
        h, u, cq, ckv, kr2, gq, gk, gv, gr, gd, gl = _ln_inproj(
            stream, row(g_in), row(b_in), _pack_w_in(w_in[l]))
        wq, wq2, wk, wv = _pack_mla(mla_w_uq[l], mla_w_ukv[l])
        q, k, v = _mla_prep(cq, ckv, kr2, cos_t, sin_t, row(mla_q_norm[l]), row(mla_kv_norm[l]),
                            wq, wq2, wk, wv, nope_row, one_row, B, S)
        o_mla, (wg_b, wu_b, wd_b) = _flash(q, k, v, (exp_w_gate, exp_w_up, exp_w_down), l)
        o_mla = o_mla.reshape(T, MLA_HEADS * MLA_V)
        wd, bd = _pack_gla_decay(gla_w_dec[l], gla_b_dec[l])
        gla_o = _gla(gq, gk, gv, gd, gr, wd, bd, tri_f, tri_b, row(gla_norm[l]), B, S)
        rw = jnp.pad(router_w[l], ((0, 0), (0, LANES - N_EXPERTS)))
        ha, afft = _merge(u, o_mla, gla_o, gl, h, pool_w[l].astype(BF16), row(pool_scale[l]),
                          w_up_a[l].astype(BF16), w_up_b[l].astype(BF16),
                          w_up_c[l].astype(BF16), row(b_gate[l]),
                          (0.5 * w_out[l]).astype(BF16), row(ln1_g[l]), row(ln1_b[l]), rw, B, S)
        idx = _topk(afft, tri_lane, C)
        idx_steps = (idx + (jnp.arange(B, dtype=jnp.int32) * S)[:, None, None])
        idx_steps = idx_steps.transpose(1, 0, 2).reshape(-1)
        stream = _ffn(ha, idx_steps, wg_b, wu_b, wd_b, B, C, tc)
        g_in, b_in = ln2_g[l], ln2_b[l]
    out = _ln_rows(stream, row(g_in), row(b_in))
    return out.reshape(B, S, D)
```

```python
import functools

import jax
import jax.numpy as jnp
from jax import lax
from jax.experimental import pallas as pl
from jax.experimental.pallas import tpu as pltpu

F32 = jnp.float32
BF16 = jnp.bfloat16

LANES = 128
VMEM_LIMIT = 56 * 1024 * 1024
FLASH_VMEM_LIMIT = 60 * 1024 * 1024

D_MODEL = 1024
DEPTH = 2
POOL_WINDOWS = (2, 4, 8, 16)
POOL_DIM = 512
POOL_HALO = 16
MLA_HEADS = 8
MLA_NOPE = 64
MLA_ROPE = 32
MLA_V = 64
MLA_Q_RANK = 384
MLA_KV_RANK = 256
ROPE_THETA = 10000.0
GLA_HEADS = 4
GLA_DK = 64
GLA_DV = 128
GLA_GATE_RANK = 16
GLA_GATE_NORM = 16.0
GLA_CHUNK = 64
N_EXPERTS = 16
CAPACITY_FACTOR = 2
D_EXPERT = 2048
DN_ALPHA = (2 * DEPTH) ** 0.25
LN_EPS = 1e-5
RMS_EPS = 1e-6
LOG2_E = 1.4426950408889634
NEVER = 1e9

HEAD_PAD = LANES
GLA_W = GLA_HEADS * HEAD_PAD
HA_W = 2 * D_MODEL + LANES
DMA_UNROLL = 16
MERGE_GROUPS = 2

_SEG = (("pool", POOL_DIM), ("cq", MLA_Q_RANK), ("ckv", MLA_KV_RANK), ("kr2", 2 * HEAD_PAD),
        ("gq", GLA_W), ("gk", GLA_W), ("gv", GLA_W), ("gr", GLA_W), ("gd", LANES),
        ("gates", 3 * D_MODEL))
_SEG_OFF = {}
_off = 0
for _n, _w in _SEG:
    _SEG_OFF[_n] = (_off, _w)
    _off += _w
W_ALL_COLS = _off


def _params(*sem):
    return pltpu.CompilerParams(dimension_semantics=sem, vmem_limit_bytes=VMEM_LIMIT)


def _dot(a, b):
    return jnp.dot(a, b, preferred_element_type=F32)


def _dot_nt(a, b):
    return lax.dot_general(a, b, (((1,), (1,)), ((), ())), preferred_element_type=F32)


def _split(x):
    hi = x.astype(BF16)
    lo = (x - hi.astype(F32)).astype(BF16)
    return hi, lo


def _dot3(a, b, nt=False):
    d = _dot_nt if nt else _dot
    ah, al = _split(a)
    bh, bl = _split(b)
    return d(ah, bh) + d(ah, bl) + d(al, bh)


def _layer_norm(x, g, b):
    mu = jnp.mean(x, axis=-1, keepdims=True)
    xc = x - mu
    var = jnp.mean(xc * xc, axis=-1, keepdims=True)
    return xc * lax.rsqrt(var + LN_EPS) * g + b


def _rms_norm(x, g):
    ms = jnp.mean(x * x, axis=-1, keepdims=True)
    return x * lax.rsqrt(ms + RMS_EPS) * g


def _rope_kernel(pos_ref, freq_ref, mask_ref, cos_ref, sin_ref):
    ang = pos_ref[...].astype(F32) * freq_ref[...]
    cos_ref[...] = jnp.cos(ang) * mask_ref[...]
    sin_ref[...] = jnp.sin(ang) * mask_ref[...]


def _rope_tables(pos_col, freq_row, mask_row, tm=512):
    T = pos_col.shape[0]
    row = pl.BlockSpec((1, LANES), lambda i: (0, 0))
    out = pl.BlockSpec((tm, LANES), lambda i: (i, 0))
    return pl.pallas_call(
        _rope_kernel,
        grid=(T // tm,),
        in_specs=[pl.BlockSpec((tm, 1), lambda i: (i, 0)), row, row],
        out_specs=[out, out],
        out_shape=[jax.ShapeDtypeStruct((T, LANES), F32)] * 2,
        compiler_params=_params("parallel"),
        name="rope_tables",
    )(pos_col, freq_row, mask_row)


def _inproj_kernel(x_ref, xn_ref, g_ref, b_ref, w_ref, h_ref, *rest):
    out_refs, h_scr = rest[:-1], rest[-1]

    @pl.when(pl.program_id(0) == 0)
    def _():
        h_scr[...] = _layer_norm(x_ref[...], g_ref[...], b_ref[...])

    h = h_scr[...]
    h_ref[...] = h
    hb = h.astype(BF16)
    for (name, width), o_ref in zip(_SEG, out_refs):
        off = _SEG_OFF[name][0]
        o_ref[...] = _dot(hb, w_ref[:, off:off + width]).astype(o_ref.dtype)
    h_scr[...] = _layer_norm(xn_ref[...], g_ref[...], b_ref[...])


def _ln_inproj(xin, g, b, w_all, tm=512):
    T = xin.shape[0]
    n_steps = T // tm
    const = lambda i: (0, 0)
    rows = lambda w: pl.BlockSpec((tm, w), lambda i: (i, 0))
    out_shapes = [jax.ShapeDtypeStruct((T, D_MODEL), F32)]
    out_specs = [rows(D_MODEL)]
    for name, width in _SEG:
        out_shapes.append(jax.ShapeDtypeStruct((T, width), F32 if name == "gd" else BF16))
        out_specs.append(rows(width))
    return pl.pallas_call(
        _inproj_kernel,
        grid=(n_steps,),
        in_specs=[rows(D_MODEL),
                  pl.BlockSpec((tm, D_MODEL), lambda i: (jnp.minimum(i + 1, n_steps - 1), 0)),
                  pl.BlockSpec((1, D_MODEL), const), pl.BlockSpec((1, D_MODEL), const),
                  pl.BlockSpec((D_MODEL, W_ALL_COLS), const)],
        out_specs=out_specs,
        out_shape=out_shapes,
        scratch_shapes=[pltpu.VMEM((tm, D_MODEL), F32)],
        compiler_params=_params("arbitrary"),
        name="ln_inproj",
    )(xin, xin, g, b, w_all)


def _mla_prep_kernel(cq_ref, ckv_ref, kr2_ref, cos_ref, sin_ref, qn_ref, kvn_ref,
                     wq_ref, wq2_ref, wk_ref, wv_ref, nope_ref, one_ref,
                     q_ref, k_ref, v_ref):
    scale = (MLA_NOPE + MLA_ROPE) ** -0.5 * LOG2_E
    cosr = cos_ref[...]
    sinr = sin_ref[...]
    nq = _rms_norm(cq_ref[...].astype(F32), qn_ref[...]).astype(BF16)
    nkv = _rms_norm(ckv_ref[...].astype(F32), kvn_ref[...]).astype(BF16)
    qa = _dot(nq, wq_ref[...])
    qb = _dot(nq, wq2_ref[...])
    ka = _dot(nkv, wk_ref[...])
    va = _dot(nkv, wv_ref[...])
    kr2 = kr2_ref[...].astype(F32)
    k_rope = kr2[:, :HEAD_PAD] * cosr + kr2[:, HEAD_PAD:] * sinr
    q_cos = (cosr + nope_ref[...]) * scale
    q_sin = sinr * scale
    for h in range(MLA_HEADS):
        sl = slice(h * HEAD_PAD, (h + 1) * HEAD_PAD)
        q_ref[0, h] = (qa[:, sl] * q_cos + qb[:, sl] * q_sin).astype(BF16)
        k_ref[0, h] = (ka[:, sl] + k_rope).astype(BF16)
        v_ref[0, h] = (va[:, sl] + one_ref[...]).astype(BF16)


def _mla_prep(cq, ckv, kr2, cos_t, sin_t, qn, kvn, wq, wq2, wk, wv, nope_row, one_row, B, S,
              tm=512):
    nb = S // tm
    rows = lambda w: pl.BlockSpec((tm, w), lambda b, i: (b * nb + i, 0))
    const = lambda r, c: pl.BlockSpec((r, c), lambda b, i: (0, 0))
    hw = MLA_HEADS * HEAD_PAD
    out = pl.BlockSpec((1, MLA_HEADS, tm, HEAD_PAD), lambda b, i: (b, 0, i, 0))
    return pl.pallas_call(
        _mla_prep_kernel,
        grid=(B, nb),
        in_specs=[rows(MLA_Q_RANK), rows(MLA_KV_RANK), rows(2 * HEAD_PAD), rows(LANES),
                  rows(LANES), const(1, MLA_Q_RANK), const(1, MLA_KV_RANK),
                  const(MLA_Q_RANK, hw), const(MLA_Q_RANK, hw), const(MLA_KV_RANK, hw),
                  const(MLA_KV_RANK, hw), const(1, LANES), const(1, LANES)],
        out_specs=[out, out, out],
        out_shape=[jax.ShapeDtypeStruct((B, MLA_HEADS, S, HEAD_PAD), BF16)] * 3,
        compiler_params=_params("parallel", "parallel"),
        name="mla_prep",
    )(cq, ckv, kr2, cos_t, sin_t, qn, kvn, wq, wq2, wk, wv, nope_row, one_row)


def _flash_kernel(q_ref, k_ref, v_ref, wg_ref, wu_ref, wd_ref, o_ref, wgb_ref, wub_ref, wdb_ref,
                  m_scr, acc_scr):
    kv = pl.program_id(2)
    wgb_ref[...] = wg_ref[0].astype(BF16)
    wub_ref[...] = wu_ref[0].astype(BF16)
    wdb_ref[...] = wd_ref[0].astype(BF16)

    @pl.when(kv == 0)
    def _():
        m_scr[...] = jnp.full(m_scr.shape, -jnp.inf, F32)
        acc_scr[...] = jnp.zeros(acc_scr.shape, F32)

    tk = k_ref.shape[2]
    for h in range(MLA_HEADS):
        s = _dot_nt(q_ref[0, h], k_ref[0, h])
        m_prev = m_scr[h]
        m_new = jnp.maximum(m_prev, jnp.max(s, axis=1, keepdims=True))
        p = jnp.exp2(s - jnp.concatenate([m_new] * (tk // LANES), axis=1))
        alpha = jnp.exp2(m_prev - m_new)
        acc_scr[h] = alpha * acc_scr[h] + _dot(p.astype(BF16), v_ref[0, h])
        m_scr[h] = m_new

    @pl.when(kv == pl.num_programs(2) - 1)
    def _():
        lane = lax.broadcasted_iota(jnp.int32, (q_ref.shape[2], HEAD_PAD), 1)
        for hp in range(MLA_HEADS // 2):
            a0 = acc_scr[2 * hp]
            a1 = acc_scr[2 * hp + 1]
            o0 = a0 / a0[:, MLA_V:MLA_V + 1]
            o1 = a1 / a1[:, MLA_V:MLA_V + 1]
            pair = jnp.where(lane < MLA_V, o0, pltpu.roll(o1, MLA_V, axis=1))
            o_ref[0, :, hp * HEAD_PAD:(hp + 1) * HEAD_PAD] = pair.astype(BF16)


def _flash(q, k, v, expert_w, layer, tq=1024, tk=2048):
    B, H, S, _ = q.shape
    tq, tk = min(tq, S), min(tk, S)
    assert S % tq == 0 and S % tk == 0
    nq, nk = S // tq, S // tk
    n_steps = B * nq * nk
    qspec = pl.BlockSpec((1, H, tq, HEAD_PAD), lambda b, i, j: (b, 0, i, 0))
    kspec = pl.BlockSpec((1, H, tk, HEAD_PAD), lambda b, i, j: (b, 0, j, 0))
    w_in, w_specs, wb_specs, wb_shapes = [], [], [], []
    for w in expert_w:
        n_l, n_e, r, c = w.shape
        slab = n_e * r // n_steps
        assert n_e * r % n_steps == 0 and slab % 16 == 0
        w_in.append(w.reshape(n_l, n_steps, slab, c))
        w_specs.append(pl.BlockSpec((1, 1, slab, c),
                                    lambda b, i, j: (layer, (b * nq + i) * nk + j, 0, 0)))
        wb_specs.append(pl.BlockSpec((1, slab, c), lambda b, i, j: ((b * nq + i) * nk + j, 0, 0)))
        wb_shapes.append(jax.ShapeDtypeStruct((n_steps, slab, c), BF16))
    o, *wb = pl.pallas_call(
        _flash_kernel,
        grid=(B, nq, nk),
        in_specs=[qspec, kspec, kspec] + w_specs,
        out_specs=[pl.BlockSpec((1, tq, H * MLA_V), lambda b, i, j: (b, i, 0))] + wb_specs,
        out_shape=[jax.ShapeDtypeStruct((B, S, H * MLA_V), BF16)] + wb_shapes,
        scratch_shapes=[pltpu.VMEM((H, tq, LANES), F32), pltpu.VMEM((H, tq, HEAD_PAD), F32)],
        compiler_params=pltpu.CompilerParams(
            dimension_semantics=("parallel", "parallel", "arbitrary"),
            vmem_limit_bytes=FLASH_VMEM_LIMIT),
        name="mla_flash",
    )(q, k, v, *w_in)
    return o, [b16.reshape(w.shape[1:]) for b16, w in zip(wb, expert_w)]


def _log_sigmoid(x):
    return jnp.minimum(x, 0.0) - jnp.log(1.0 + jnp.exp(-jnp.abs(x)))


def _gla_scan(reverse, gq_ref, gk_ref, gv_ref, gd_ref, wd_ref, bd_ref, tri_ref, state_scr,
              o_scr):
    tm = gq_ref.shape[0]
    L = GLA_CHUNK
    n_chunks = tm // L

    @pl.when(pl.program_id(1) == 0)
    def _():
        state_scr[...] = jnp.zeros(state_scr.shape, F32)

    logits = _dot(gd_ref[...].astype(BF16), wd_ref[...]) + bd_ref[...]
    g = _log_sigmoid(logits) * (1.0 / GLA_GATE_NORM)
    g_hi, g_lo = _split(g)
    b = _dot(tri_ref[...], g_hi) + _dot(tri_ref[...], g_lo)
    edge = 0 if reverse else L - 1
    b_last = jnp.concatenate(
        [jnp.broadcast_to(b[c * L + edge:c * L + edge + 1, :], (L, GLA_W))
         for c in range(n_chunks)], axis=0)
    q_in = (gq_ref[...].astype(F32) * (GLA_DK ** -0.5) * jnp.exp(b)).astype(BF16)
    k_f = gk_ref[...].astype(F32)
    k_in = (k_f * jnp.exp(-b)).astype(BF16)
    k_st = (k_f * jnp.exp(b_last - b)).astype(BF16)
    dec = jnp.exp(b_last)
    v = gv_ref[...]
    lane = lax.broadcasted_iota(jnp.int32, (HEAD_PAD, 2 * L), 1)
    order = range(n_chunks - 1, -1, -1) if reverse else range(n_chunks)
    heads = [slice(h * HEAD_PAD, (h + 1) * HEAD_PAD) for h in range(GLA_HEADS)]
    kv = {}
    for h, hs in enumerate(heads):
        a = _dot_nt(q_in[:, hs], k_in[:, hs]).astype(BF16) * tri_ref[...]
        o_scr[:, hs] = _dot(a, v[:, hs])
        v_t = v[:, hs].astype(F32).T
        for c in range(n_chunks):
            pair = slice((c // 2) * 2 * L, (c // 2 + 1) * 2 * L)
            in_chunk = (lane >= L) if c % 2 else (lane < L)
            v_tc = jnp.where(in_chunk, v_t[:, pair], 0.0).astype(BF16)
            kv[c, h] = _dot(v_tc, k_st[pair, hs])
    entering = {}
    for h, hs in enumerate(heads):
        st = state_scr[h]
        for c in order:
            entering[c, h] = st.astype(BF16)
            st = dec[c * L:c * L + 1, hs] * st + kv[c, h]
        state_scr[h] = st
    for c in range(n_chunks):
        rs = slice(c * L, (c + 1) * L)
        for h, hs in enumerate(heads):
            o_scr[rs, hs] += _dot_nt(q_in[rs, hs], entering[c, h])


def _gla_fwd_kernel(gq_ref, gk_ref, gv_ref, gd_ref, wd_ref, bd_ref, tri_ref, o_ref, state_scr,
                    o_scr):
    _gla_scan(False, gq_ref, gk_ref, gv_ref, gd_ref, wd_ref, bd_ref, tri_ref, state_scr, o_scr)
    o_ref[...] = o_scr[...]


def _gla_bwd_kernel(gq_ref, gk_ref, gv_ref, gd_ref, wd_ref, bd_ref, tri_ref, of_ref, gr_ref,
                    ng_ref, out_ref, state_scr, o_scr):
    _gla_scan(True, gq_ref, gk_ref, gv_ref, gd_ref, wd_ref, bd_ref, tri_ref, state_scr, o_scr)
    r = gr_ref[...].astype(F32)
    gate = r * jax.nn.sigmoid(r)
    for h in range(GLA_HEADS):
        hs = slice(h * HEAD_PAD, (h + 1) * HEAD_PAD)
        o = _rms_norm(of_ref[:, hs] + o_scr[:, hs], ng_ref[...])
        out_ref[:, hs] = (o * gate[:, hs]).astype(BF16)


def _gla(gq, gk, gv, gd, gr, wd, bd, tri_f, tri_b, norm_g, B, S, tm=512):
    nb = S // tm
    T = B * S

    def call(reverse):
        blk = (lambda b, i: (b * nb + nb - 1 - i, 0)) if reverse else (lambda b, i: (b * nb + i, 0))
        rows = lambda w: pl.BlockSpec((tm, w), blk)
        const = lambda r, c: pl.BlockSpec((r, c), lambda b, i: (0, 0))
        in_specs = [rows(GLA_W), rows(GLA_W), rows(GLA_W), rows(LANES),
                    const(LANES, GLA_W), const(1, GLA_W), const(tm, tm)]
        scratch = [pltpu.VMEM((GLA_HEADS, GLA_DV, HEAD_PAD), F32), pltpu.VMEM((tm, GLA_W), F32)]
        d = 1 if reverse else 0
        args = [gq, gk, gv, gd, wd[d], bd[d], tri_b if reverse else tri_f]
        if reverse:
            in_specs += [rows(GLA_W), rows(GLA_W), const(1, GLA_DV)]
            args += [o_f, gr, norm_g]
        return pl.pallas_call(
            _gla_bwd_kernel if reverse else _gla_fwd_kernel,
            grid=(B, nb),
            in_specs=in_specs,
            out_specs=rows(GLA_W),
            out_shape=jax.ShapeDtypeStruct((T, GLA_W), BF16 if reverse else F32),
            scratch_shapes=scratch,
            compiler_params=_params("parallel", "arbitrary"),
            name="gla_bwd" if reverse else "gla_fwd",
        )(*args)

    o_f = call(False)
    return call(True)


def _merge_kernel(S, u_ref, up_ref, un_ref, om_ref, gla_ref, gl_ref, h_ref,
                  pw_ref, ps_ref, wa_ref, wb_ref, wc_ref, bg_ref, wo_ref, g_ref, b_ref,
                  rw_ref, ha_ref, afft_ref):
    tm = u_ref.shape[0]
    ext = jnp.concatenate([up_ref[...], u_ref[...], un_ref[...]], axis=0).astype(F32)
    tg = tm // MERGE_GROUPS
    for grp in range(MERGE_GROUPS):
        _merge_rows(S, pl.program_id(1) * tm + grp * tg, slice(grp * tg, (grp + 1) * tg),
                    ext[grp * tg:grp * tg + tg + 2 * POOL_HALO], om_ref, gla_ref, gl_ref, h_ref,
                    pw_ref, ps_ref, wa_ref, wb_ref, wc_ref, bg_ref, wo_ref, g_ref, b_ref,
                    rw_ref, ha_ref, afft_ref)


def _merge_rows(S, first_pos, rows, ext, om_ref, gla_ref, gl_ref, h_ref,
                pw_ref, ps_ref, wa_ref, wb_ref, wc_ref, bg_ref, wo_ref, g_ref, b_ref,
                rw_ref, ha_ref, afft_ref):
    n_ext = ext.shape[0]
    tg = n_ext - 2 * POOL_HALO
    pos_ext = first_pos - POOL_HALO + lax.broadcasted_iota(jnp.int32, (n_ext, LANES), 0)
    in_seq = (pos_ext >= 0) & (pos_ext < S)
    pos = first_pos + lax.broadcasted_iota(jnp.int32, (tg, LANES), 0)
    core = slice(POOL_HALO, POOL_HALO + tg)

    def rows_at(a, d):
        return pltpu.roll(a, (-d) % n_ext, axis=0)

    pooled = []
    for gi, w in enumerate(POOL_WINDOWS):
        hw = w // 2
        cs = slice(gi * LANES, (gi + 1) * LANES)
        x = jnp.where(in_seq, ext[:, cs], 0.0)
        win = rows_at(x, -1) + x
        reach = 1
        while reach < hw:
            win = rows_at(win, -reach) + rows_at(win, reach)
            reach *= 2
        cnt = (jnp.minimum(pos + hw, S) - jnp.maximum(pos - hw, 0)).astype(F32)
        pg = (win[core] / cnt - x[core]).astype(BF16)
        pooled.append(_dot(pg, pw_ref[gi]))
    pa = (jnp.concatenate(pooled, axis=1) * ps_ref[...]).astype(BF16)
    y_a = _dot(pa, wa_ref[...])
    y_b = _dot(om_ref[rows, :], wb_ref[...])
    y_c = _dot(gla_ref[rows, :], wc_ref[...])
    gates2 = 1.0 + jnp.tanh(0.5 * (gl_ref[rows, :].astype(F32) + bg_ref[...]))
    merged2 = (gates2[:, :D_MODEL] * y_a + gates2[:, D_MODEL:2 * D_MODEL] * y_b
               + gates2[:, 2 * D_MODEL:] * y_c)
    mix = _dot(merged2.astype(BF16), wo_ref[...])
    h1 = _layer_norm(DN_ALPHA * h_ref[rows, :] + mix, g_ref[...], b_ref[...])
    lane = lax.broadcasted_iota(jnp.int32, (tg, LANES), 1)
    logits = jnp.where(lane < N_EXPERTS, _dot3(h1, rw_ref[...]), -jnp.inf)
    e = jnp.exp(logits - jnp.max(logits, axis=1, keepdims=True))
    aff = e / jnp.sum(e, axis=1, keepdims=True)
    afft_ref[0, :, rows] = aff.T[:N_EXPERTS]
    ha_ref[rows, :D_MODEL] = DN_ALPHA * h1
    ha_ref[rows, D_MODEL:2 * D_MODEL] = h1
    ha_ref[rows, 2 * D_MODEL:] = aff


def _merge(u, o_mla, gla_o, gl, h, pw, ps, wa, wb, wc, bg, wo, g, b, rw, B, S, tm=512):
    nb = S // tm
    T = B * S
    hb = tm // POOL_HALO
    n_halo = T // POOL_HALO
    rows = lambda w: pl.BlockSpec((tm, w), lambda bb, i: (bb * nb + i, 0))
    prev = pl.BlockSpec((POOL_HALO, POOL_DIM),
                        lambda bb, i: (jnp.maximum((bb * nb + i) * hb - 1, 0), 0))
    nxt = pl.BlockSpec((POOL_HALO, POOL_DIM),
                       lambda bb, i: (jnp.minimum((bb * nb + i + 1) * hb, n_halo - 1), 0))
    c2 = lambda r, c: pl.BlockSpec((r, c), lambda bb, i: (0, 0))
    c3 = lambda a, r, c: pl.BlockSpec((a, r, c), lambda bb, i: (0, 0, 0))
    return pl.pallas_call(
        functools.partial(_merge_kernel, S),
        grid=(B, nb),
        in_specs=[rows(POOL_DIM), prev, nxt,
                  rows(MLA_HEADS * MLA_V), rows(GLA_W), rows(3 * D_MODEL), rows(D_MODEL),
                  c3(len(POOL_WINDOWS), LANES, LANES), c2(1, POOL_DIM), c2(POOL_DIM, D_MODEL),
                  c2(MLA_HEADS * MLA_V, D_MODEL), c2(GLA_W, D_MODEL), c2(1, 3 * D_MODEL),
                  c2(D_MODEL, D_MODEL), c2(1, D_MODEL), c2(1, D_MODEL), c2(D_MODEL, LANES)],
        out_specs=[rows(HA_W),
                   pl.BlockSpec((1, N_EXPERTS, tm), lambda bb, i: (bb, 0, i))],
        out_shape=[jax.ShapeDtypeStruct((T, HA_W), F32),
                   jax.ShapeDtypeStruct((B, N_EXPERTS, S), F32)],
        compiler_params=_params("parallel", "parallel"),
        name="merge",
    )(u, u, u, o_mla, gla_o, gl, h, pw, ps, wa, wb, wc, bg, wo, g, b, rw)


def _topk_kernel(C, aff_ref, tri_ref, idx_ref, p_scr):
    S = aff_ref.shape[2]
    n_chunks = S // LANES
    aff = aff_ref[0]

    def count(mask):
        return jnp.sum(jnp.where(mask, 1.0, 0.0), axis=1, keepdims=True)

    def as_float(bits):
        return lax.bitcast_convert_type(bits, F32)

    def refine(i, thr):
        cand = thr | jnp.left_shift(jnp.int32(1), 30 - i)
        return jnp.where(count(aff >= as_float(cand)) >= C, cand, thr)

    thr = lax.fori_loop(0, 31, refine, jnp.zeros((N_EXPERTS, 1), jnp.int32))
    above = aff >= as_float(thr + 1)
    tied = (aff >= as_float(thr)) & jnp.logical_not(above)
    need = C - count(above)
    tri = tri_ref[...]

    tied_f = tied.astype(F32)
    run = jnp.zeros((N_EXPERTS, 1), F32)
    sel_parts = []
    for c in range(n_chunks):
        cs = slice(c * LANES, (c + 1) * LANES)
        incl = _dot(tied_f[:, cs].astype(BF16), tri) + run
        run = incl[:, LANES - 1:LANES]
        sel_parts.append(jnp.where(above[:, cs] | (tied[:, cs] & (incl <= need)), 1.0, 0.0))

    p_scr[...] = jnp.zeros(p_scr.shape, F32)
    lane_e = lax.broadcasted_iota(jnp.int32, (N_EXPERTS, LANES), 1)
    chunk_end = jnp.full((N_EXPERTS, LANES), NEVER, F32)
    run = jnp.zeros((N_EXPERTS, 1), F32)
    for c in range(n_chunks):
        rel = _dot(sel_parts[c].astype(BF16), tri)
        p_scr[:, c, :] = rel
        run = run + rel[:, LANES - 1:LANES]
        chunk_end = jnp.where(lane_e == c, run, chunk_end)

    slot = lax.broadcasted_iota(jnp.int32, (C, LANES), 0).astype(F32)
    lane_c = lax.broadcasted_iota(jnp.int32, (C, LANES), 1).astype(F32)
    ones = jnp.ones((8, LANES), BF16)
    for e in range(N_EXPERTS):
        ce = chunk_end[e:e + 1, :]
        full = ce <= slot
        n_full = jnp.sum(jnp.where(full, 1.0, 0.0), axis=1, keepdims=True)
        base = jnp.max(jnp.where(full, ce, 0.0), axis=1, keepdims=True)
        pick = jnp.where(lane_c == n_full, 1.0, 0.0).astype(BF16)
        rel = _dot(pick, p_scr[e].astype(BF16))
        w = jnp.where(rel <= slot - base, 1.0, 0.0) + jnp.where(full, float(LANES), 0.0)
        tok = _dot_nt(ones, w.astype(BF16))[0:1, :]
        idx_ref[0, e:e + 1, :] = tok.astype(jnp.int32)


def _topk(afft, tri, C):
    B, E, S = afft.shape
    assert S // LANES <= LANES
    return pl.pallas_call(
        functools.partial(_topk_kernel, C),
        grid=(B,),
        in_specs=[pl.BlockSpec((1, E, S), lambda b: (b, 0, 0)),
                  pl.BlockSpec((LANES, LANES), lambda b: (0, 0))],
        out_specs=pl.BlockSpec((1, E, C), lambda b: (b, 0, 0)),
        out_shape=jax.ShapeDtypeStruct((B, E, C), jnp.int32),
        scratch_shapes=[pltpu.VMEM((E, LANES, LANES), F32)],
        compiler_params=_params("parallel"),
        name="expert_choice",
    )(afft, tri)


def _ffn_kernel(tc, n_steps, idx_ref, ha_in_ref, wg_ref, wu_ref, wd_ref, ha_ref,
                gbuf, sbuf, gsem, ssem):
    del ha_in_ref
    e = pl.program_id(0)
    s = (e * pl.num_programs(1) + pl.program_id(1)) * pl.num_programs(2) + pl.program_id(2)
    slot = s % 2

    def gather_copy(step, r, sl):
        row = idx_ref[step * tc + r]
        return pltpu.make_async_copy(ha_ref.at[pl.ds(row, 1), :],
                                     gbuf.at[sl, pl.ds(r, 1), :], gsem.at[sl])

    def scatter_copy(step, r, sl):
        row = idx_ref[step * tc + r]
        return pltpu.make_async_copy(sbuf.at[sl, pl.ds(r, 1), :],
                                     ha_ref.at[pl.ds(row, 1), pl.ds(0, D_MODEL)], ssem.at[sl])

    def start_rows(copy, step, sl):
        def body(r, carry):
            copy(step, r, sl).start()
            return carry
        lax.fori_loop(0, tc, body, 0, unroll=DMA_UNROLL)

    def wait_gather(sl):
        pltpu.make_async_copy(ha_ref.at[pl.ds(0, tc), :], gbuf.at[sl], gsem.at[sl]).wait()

    def wait_scatter(sl):
        pltpu.make_async_copy(sbuf.at[sl], ha_ref.at[pl.ds(0, tc), pl.ds(0, D_MODEL)],
                              ssem.at[sl]).wait()

    @pl.when(s == 0)
    def _():
        start_rows(gather_copy, s, slot)

    @pl.when(s >= 2)
    def _():
        wait_scatter(slot)

    @pl.when(s + 1 < n_steps)
    def _():
        start_rows(gather_copy, s + 1, 1 - slot)

    wait_gather(slot)

    rows = gbuf[slot]
    x = rows[:, D_MODEL:2 * D_MODEL].astype(BF16)
    lane = lax.broadcasted_iota(jnp.int32, (tc, LANES), 1)
    gate = jnp.sum(jnp.where(lane == e, rows[:, 2 * D_MODEL:], 0.0), axis=1, keepdims=True)
    y = jnp.zeros((tc, D_MODEL), F32)
    fh = D_EXPERT // 2
    for f in range(2):
        fs = slice(f * fh, (f + 1) * fh)
        hg = _dot(x, wg_ref[0, :, fs])
        hu = _dot(x, wu_ref[0, :, fs])
        hid = (hg * jax.nn.sigmoid(hg) * hu).astype(BF16)
        y += _dot(hid, wd_ref[0, fs, :])
    sbuf[slot] = rows[:, :D_MODEL] + gate * y
    start_rows(scatter_copy, s, slot)

    @pl.when(s == n_steps - 1)
    def _():
        if n_steps >= 2:
            wait_scatter(1 - slot)
        wait_scatter(slot)


def _ffn(ha, idx_steps, wg, wu, wd, B, C, tc=512):
    nblk = C // tc
    n_steps = N_EXPERTS * B * nblk
    wspec = lambda r, c: pl.BlockSpec((1, r, c), lambda e, b, j, idx: (e, 0, 0))
    return pl.pallas_call(
        functools.partial(_ffn_kernel, tc, n_steps),
        grid_spec=pltpu.PrefetchScalarGridSpec(
            num_scalar_prefetch=1,
            grid=(N_EXPERTS, B, nblk),
            in_specs=[pl.BlockSpec(memory_space=pl.ANY),
                      wspec(D_MODEL, D_EXPERT), wspec(D_MODEL, D_EXPERT),
                      wspec(D_EXPERT, D_MODEL)],
            out_specs=pl.BlockSpec(memory_space=pl.ANY),
            scratch_shapes=[pltpu.VMEM((2, tc, HA_W), F32), pltpu.VMEM((2, tc, D_MODEL), F32),
                            pltpu.SemaphoreType.DMA((2,)), pltpu.SemaphoreType.DMA((2,))]),
        out_shape=jax.ShapeDtypeStruct(ha.shape, F32),
        input_output_aliases={1: 0},
        compiler_params=_params("arbitrary", "arbitrary", "arbitrary"),
        name="expert_ffn",
    )(idx_steps, ha, wg, wu, wd)


def _ln_kernel(x_ref, g_ref, b_ref, o_ref):
    o_ref[...] = _layer_norm(x_ref[...], g_ref[...], b_ref[...])


def _ln_rows(xin, g, b, tm=512):
    T = xin.shape[0]
    const = lambda i: (0, 0)
    return pl.pallas_call(
        _ln_kernel,
        grid=(T // tm,),
        in_specs=[pl.BlockSpec((tm, D_MODEL), lambda i: (i, 0)),
                  pl.BlockSpec((1, D_MODEL), const), pl.BlockSpec((1, D_MODEL), const)],
        out_specs=pl.BlockSpec((tm, D_MODEL), lambda i: (i, 0)),
        out_shape=jax.ShapeDtypeStruct((T, D_MODEL), F32),
        compiler_params=_params("parallel"),
        name="final_ln",
    )(xin, g, b)


def _pad_heads(w, heads, width):
    lead = w.shape[:-1]
    w = w.reshape(lead + (heads, width))
    w = jnp.pad(w, [(0, 0)] * len(lead) + [(0, 0), (0, HEAD_PAD - width)])
    return w.reshape(lead + (heads * HEAD_PAD,))


def _rot_half(w):
    half = w.shape[-1] // 2
    return jnp.concatenate([-w[..., half:], w[..., :half]], axis=-1)


def _pack_w_in(w):
    o = 0
    parts = {}
    for name, width in (("pool", POOL_DIM), ("cq", MLA_Q_RANK), ("ckv", MLA_KV_RANK),
                        ("kr", MLA_ROPE), ("gq", GLA_HEADS * GLA_DK), ("gk", GLA_HEADS * GLA_DK),
                        ("gv", GLA_HEADS * GLA_DV), ("gr", GLA_HEADS * GLA_DV),
                        ("gd", 2 * GLA_GATE_RANK), ("gates", 3 * D_MODEL)):
        parts[name] = w[:, o:o + width]
        o += width
    rope_slot = lambda m: jnp.pad(m, ((0, 0), (MLA_NOPE, HEAD_PAD - MLA_NOPE - MLA_ROPE)))
    kr2 = jnp.concatenate([rope_slot(parts["kr"]), rope_slot(_rot_half(parts["kr"]))], axis=1)
    gd = jnp.pad(parts["gd"], ((0, 0), (0, LANES - 2 * GLA_GATE_RANK)))
    cols = [parts["pool"], parts["cq"], parts["ckv"], kr2,
            _pad_heads(parts["gq"], GLA_HEADS, GLA_DK), _pad_heads(parts["gk"], GLA_HEADS, GLA_DK),
            parts["gv"], parts["gr"], gd, parts["gates"]]
    return jnp.concatenate(cols, axis=1).astype(BF16)


def _pack_mla(w_uq, w_ukv):
    r = w_uq.shape[0]
    uq = w_uq.reshape(r, MLA_HEADS, MLA_NOPE + MLA_ROPE)
    nope, rope = uq[..., :MLA_NOPE], uq[..., MLA_NOPE:]
    tail = jnp.zeros((r, MLA_HEADS, HEAD_PAD - MLA_NOPE - MLA_ROPE), F32)
    wq = jnp.concatenate([nope, rope, tail], axis=-1).reshape(r, -1)
    wq2 = jnp.concatenate([jnp.zeros_like(nope), _rot_half(rope), tail], axis=-1).reshape(r, -1)
    rk = w_ukv.shape[0]
    ukv = w_ukv.reshape(rk, MLA_HEADS, MLA_NOPE + MLA_V)
    wk = _pad_heads(ukv[..., :MLA_NOPE].reshape(rk, -1), MLA_HEADS, MLA_NOPE)
    wv = _pad_heads(ukv[..., MLA_NOPE:].reshape(rk, -1), MLA_HEADS, MLA_V)
    return wq.astype(BF16), wq2.astype(BF16), wk.astype(BF16), wv.astype(BF16)


def _pack_gla_decay(w_dec, b_dec):
    wd = []
    for d in range(2):
        rows = _pad_heads(w_dec[d], GLA_HEADS, GLA_DK)
        wd.append(jnp.pad(rows, ((d * GLA_GATE_RANK, LANES - (d + 1) * GLA_GATE_RANK),
                                 (0, 0))).astype(BF16))
    bd = [_pad_heads(b_dec[d][None, :], GLA_HEADS, GLA_DK) for d in range(2)]
    return wd, bd


def _chunk_tri(tm, reverse):
    r = jnp.arange(tm)[:, None]
    c = jnp.arange(tm)[None, :]
    same = (r // GLA_CHUNK) == (c // GLA_CHUNK)
    return (same & ((c >= r) if reverse else (c <= r))).astype(BF16)


def kernel(x, positions, ln0_g, ln0_b, w_in, b_gate, pool_w, pool_scale, w_up_a, mla_q_norm,
           mla_w_uq, mla_kv_norm, mla_w_ukv, w_up_b, gla_w_dec, gla_b_dec, gla_norm, w_up_c,
           w_out, ln1_g, ln1_b, router_w, exp_w_gate, exp_w_up, exp_w_down, ln2_g, ln2_b):
    B, S, D = x.shape
    assert D == D_MODEL and S % 512 == 0 and B == 2
    T = B * S
    C = CAPACITY_FACTOR * S // N_EXPERTS
    tc = min(512, C)
    row = lambda v: v.reshape(1, -1).astype(F32)

    half = MLA_ROPE // 2
    freqs = ROPE_THETA ** (-jnp.arange(half, dtype=F32) / half)
    lanes = jnp.arange(LANES)
    in_rope = (lanes >= MLA_NOPE) & (lanes < MLA_NOPE + MLA_ROPE)
    freq_row = jnp.where(in_rope, freqs[(lanes - MLA_NOPE) % half], 0.0).reshape(1, LANES)
    rope_mask = in_rope.astype(F32).reshape(1, LANES)
    nope_row = (lanes < MLA_NOPE).astype(F32).reshape(1, LANES)
    one_row = (lanes == MLA_V).astype(F32).reshape(1, LANES)
    cos_t, sin_t = _rope_tables(positions.reshape(T, 1), freq_row, rope_mask)

    tri_f, tri_b = _chunk_tri(512, False), _chunk_tri(512, True)
    tri_lane = (jnp.arange(LANES)[:, None] <= jnp.arange(LANES)[None, :]).astype(BF16)

    stream, g_in, b_in = x.reshape(T, D), ln0_g, ln0_b
    for l in range(DEPTH):
        h, u, cq, ckv, kr2, gq, gk, gv, gr, gd, gl = _ln_inproj(
            stream, row(g_in), row(b_in), _pack_w_in(w_in[l]))
        wq, wq2, wk, wv = _pack_mla(mla_w_uq[l], mla_w_ukv[l])
        q, k, v = _mla_prep(cq, ckv, kr2, cos_t, sin_t, row(mla_q_norm[l]), row(mla_kv_norm[l]),
                            wq, wq2, wk, wv, nope_row, one_row, B, S)
        o_mla, (wg_b, wu_b, wd_b) = _flash(q, k, v, (exp_w_gate, exp_w_up, exp_w_down), l)
        o_mla = o_mla.reshape(T, MLA_HEADS * MLA_V)
        wd, bd = _pack_gla_decay(gla_w_dec[l], gla_b_dec[l])
        gla_o = _gla(gq, gk, gv, gd, gr, wd, bd, tri_f, tri_b, row(gla_norm[l]), B, S)
        rw = jnp.pad(router_w[l], ((0, 0), (0, LANES - N_EXPERTS)))
        ha, afft = _merge(u, o_mla, gla_o, gl, h, pool_w[l].astype(BF16), row(pool_scale[l]),
                          w_up_a[l].astype(BF16), w_up_b[l].astype(BF16),
                          w_up_c[l].astype(BF16), row(b_gate[l]),
                          (0.5 * w_out[l]).astype(BF16), row(ln1_g[l]), row(ln1_b[l]), rw, B, S)
        idx = _topk(afft, tri_lane, C)
        idx_steps = (idx + (jnp.arange(B, dtype=jnp.int32) * S)[:, None, None])
        idx_steps = idx_steps.transpose(1, 0, 2).reshape(-1)
        stream = _ffn(ha, idx_steps, wg_b, wu_b, wd_b, B, C, tc)
        g_in, b_in = ln2_g[l], ln2_b[l]
    out = _ln_rows(stream, row(g_in), row(b_in))
    return out.reshape(B, S, D)
```

```python
import functools

import jax
import jax.numpy as jnp
from jax import lax
from jax.experimental import pallas as pl
from jax.experimental.pallas import tpu as pltpu

F32 = jnp.float32
BF16 = jnp.bfloat16

LANES = 128
VMEM_LIMIT = 56 * 1024 * 1024
FLASH_VMEM_LIMIT = 60 * 1024 * 1024

D_MODEL = 1024
DEPTH = 2
POOL_WINDOWS = (2, 4, 8, 16)
POOL_DIM = 512
POOL_HALO = 16
MLA_HEADS = 8
MLA_NOPE = 64
MLA_ROPE = 32
MLA_V = 64
MLA_Q_RANK = 384
MLA_KV_RANK = 256
ROPE_THETA = 10000.0
GLA_HEADS = 4
GLA_DK = 64
GLA_DV = 128
GLA_GATE_RANK = 16
GLA_GATE_NORM = 16.0
GLA_CHUNK = 64
N_EXPERTS = 16
CAPACITY_FACTOR = 2
D_EXPERT = 2048
DN_ALPHA = (2 * DEPTH) ** 0.25
LN_EPS = 1e-5
RMS_EPS = 1e-6
LOG2_E = 1.4426950408889634
NEVER = 1e9

HEAD_PAD = LANES
GLA_W = GLA_HEADS * HEAD_PAD
HA_W = 2 * D_MODEL + LANES
DMA_UNROLL = 16
MERGE_GROUPS = 2
GLA_GROUP = 256

_SEG = (("pool", POOL_DIM), ("cq", MLA_Q_RANK), ("ckv", MLA_KV_RANK), ("kr2", 2 * HEAD_PAD),
        ("gq", GLA_W), ("gk", GLA_W), ("gv", GLA_W), ("gr", GLA_W), ("gd", LANES),
        ("gates", 3 * D_MODEL))
_SEG_OFF = {}
_off = 0
for _n, _w in _SEG:
    _SEG_OFF[_n] = (_off, _w)
    _off += _w
W_ALL_COLS = _off


def _params(*sem):
    return pltpu.CompilerParams(dimension_semantics=sem, vmem_limit_bytes=VMEM_LIMIT)


def _dot(a, b):
    return jnp.dot(a, b, preferred_element_type=F32)


def _dot_nt(a, b):
    return lax.dot_general(a, b, (((1,), (1,)), ((), ())), preferred_element_type=F32)


def _split(x):
    hi = x.astype(BF16)
    lo = (x - hi.astype(F32)).astype(BF16)
    return hi, lo


def _dot3(a, b, nt=False):
    d = _dot_nt if nt else _dot
    ah, al = _split(a)
    bh, bl = _split(b)
    return d(ah, bh) + d(ah, bl) + d(al, bh)


def _layer_norm(x, g, b):
    mu = jnp.mean(x, axis=-1, keepdims=True)
    xc = x - mu
    var = jnp.mean(xc * xc, axis=-1, keepdims=True)
    return xc * lax.rsqrt(var + LN_EPS) * g + b


def _rms_norm(x, g):
    ms = jnp.mean(x * x, axis=-1, keepdims=True)
    return x * lax.rsqrt(ms + RMS_EPS) * g


def _rope_kernel(pos_ref, freq_ref, mask_ref, cos_ref, sin_ref):
    ang = pos_ref[...].astype(F32) * freq_ref[...]
    cos_ref[...] = jnp.cos(ang) * mask_ref[...]
    sin_ref[...] = jnp.sin(ang) * mask_ref[...]


def _rope_tables(pos_col, freq_row, mask_row, tm=512):
    T = pos_col.shape[0]
    row = pl.BlockSpec((1, LANES), lambda i: (0, 0))
    out = pl.BlockSpec((tm, LANES), lambda i: (i, 0))
    return pl.pallas_call(
        _rope_kernel,
        grid=(T // tm,),
        in_specs=[pl.BlockSpec((tm, 1), lambda i: (i, 0)), row, row],
        out_specs=[out, out],
        out_shape=[jax.ShapeDtypeStruct((T, LANES), F32)] * 2,
        compiler_params=_params("parallel"),
        name="rope_tables",
    )(pos_col, freq_row, mask_row)


def _inproj_kernel(x_ref, xn_ref, g_ref, b_ref, w_ref, h_ref, *rest):
    out_refs, h_scr = rest[:-1], rest[-1]

    @pl.when(pl.program_id(0) == 0)
    def _():
        h_scr[...] = _layer_norm(x_ref[...], g_ref[...], b_ref[...])

    h = h_scr[...]
    h_ref[...] = h
    hb = h.astype(BF16)
    for (name, width), o_ref in zip(_SEG, out_refs):
        off = _SEG_OFF[name][0]
        o_ref[...] = _dot(hb, w_ref[:, off:off + width]).astype(o_ref.dtype)
    h_scr[...] = _layer_norm(xn_ref[...], g_ref[...], b_ref[...])


def _ln_inproj(xin, g, b, w_all, tm=512):
    T = xin.shape[0]
    n_steps = T // tm
    const = lambda i: (0, 0)
    rows = lambda w: pl.BlockSpec((tm, w), lambda i: (i, 0))
    out_shapes = [jax.ShapeDtypeStruct((T, D_MODEL), F32)]
    out_specs = [rows(D_MODEL)]
    for name, width in _SEG:
        out_shapes.append(jax.ShapeDtypeStruct((T, width), F32 if name == "gd" else BF16))
        out_specs.append(rows(width))
    return pl.pallas_call(
        _inproj_kernel,
        grid=(n_steps,),
        in_specs=[rows(D_MODEL),
                  pl.BlockSpec((tm, D_MODEL), lambda i: (jnp.minimum(i + 1, n_steps - 1), 0)),
                  pl.BlockSpec((1, D_MODEL), const), pl.BlockSpec((1, D_MODEL), const),
                  pl.BlockSpec((D_MODEL, W_ALL_COLS), const)],
        out_specs=out_specs,
        out_shape=out_shapes,
        scratch_shapes=[pltpu.VMEM((tm, D_MODEL), F32)],
        compiler_params=_params("arbitrary"),
        name="ln_inproj",
    )(xin, xin, g, b, w_all)


def _mla_prep_kernel(cq_ref, ckv_ref, kr2_ref, cos_ref, sin_ref, qn_ref, kvn_ref,
                     wq_ref, wq2_ref, wk_ref, wv_ref, nope_ref, one_ref,
                     q_ref, k_ref, v_ref):
    scale = (MLA_NOPE + MLA_ROPE) ** -0.5 * LOG2_E
    cosr = cos_ref[...]
    sinr = sin_ref[...]
    nq = _rms_norm(cq_ref[...].astype(F32), qn_ref[...]).astype(BF16)
    nkv = _rms_norm(ckv_ref[...].astype(F32), kvn_ref[...]).astype(BF16)
    qa = _dot(nq, wq_ref[...])
    qb = _dot(nq, wq2_ref[...])
    ka = _dot(nkv, wk_ref[...])
    va = _dot(nkv, wv_ref[...])
    kr2 = kr2_ref[...].astype(F32)
    k_rope = kr2[:, :HEAD_PAD] * cosr + kr2[:, HEAD_PAD:] * sinr
    q_cos = (cosr + nope_ref[...]) * scale
    q_sin = sinr * scale
    for h in range(MLA_HEADS):
        sl = slice(h * HEAD_PAD, (h + 1) * HEAD_PAD)
        q_ref[0, h] = (qa[:, sl] * q_cos + qb[:, sl] * q_sin).astype(BF16)
        k_ref[0, h] = (ka[:, sl] + k_rope).astype(BF16)
        v_ref[0, h] = (va[:, sl] + one_ref[...]).astype(BF16)


def _mla_prep(cq, ckv, kr2, cos_t, sin_t, qn, kvn, wq, wq2, wk, wv, nope_row, one_row, B, S,
              tm=512):
    nb = S // tm
    rows = lambda w: pl.BlockSpec((tm, w), lambda b, i: (b * nb + i, 0))
    const = lambda r, c: pl.BlockSpec((r, c), lambda b, i: (0, 0))
    hw = MLA_HEADS * HEAD_PAD
    out = pl.BlockSpec((1, MLA_HEADS, tm, HEAD_PAD), lambda b, i: (b, 0, i, 0))
    return pl.pallas_call(
        _mla_prep_kernel,
        grid=(B, nb),
        in_specs=[rows(MLA_Q_RANK), rows(MLA_KV_RANK), rows(2 * HEAD_PAD), rows(LANES),
                  rows(LANES), const(1, MLA_Q_RANK), const(1, MLA_KV_RANK),
                  const(MLA_Q_RANK, hw), const(MLA_Q_RANK, hw), const(MLA_KV_RANK, hw),
                  const(MLA_KV_RANK, hw), const(1, LANES), const(1, LANES)],
        out_specs=[out, out, out],
        out_shape=[jax.ShapeDtypeStruct((B, MLA_HEADS, S, HEAD_PAD), BF16)] * 3,
        compiler_params=_params("parallel", "parallel"),
        name="mla_prep",
    )(cq, ckv, kr2, cos_t, sin_t, qn, kvn, wq, wq2, wk, wv, nope_row, one_row)


def _flash_kernel(q_ref, k_ref, v_ref, wg_ref, wu_ref, wd_ref, o_ref, wgb_ref, wub_ref, wdb_ref,
                  m_scr, acc_scr):
    kv = pl.program_id(2)
    wgb_ref[...] = wg_ref[0].astype(BF16)
    wub_ref[...] = wu_ref[0].astype(BF16)
    wdb_ref[...] = wd_ref[0].astype(BF16)

    @pl.when(kv == 0)
    def _():
        m_scr[...] = jnp.full(m_scr.shape, -jnp.inf, F32)
        acc_scr[...] = jnp.zeros(acc_scr.shape, F32)

    tk = k_ref.shape[2]
    for h in range(MLA_HEADS):
        s = _dot_nt(q_ref[0, h], k_ref[0, h])
        m_prev = m_scr[h]
        m_new = jnp.maximum(m_prev, jnp.max(s, axis=1, keepdims=True))
        p = jnp.exp2(s - jnp.concatenate([m_new] * (tk // LANES), axis=1))
        alpha = jnp.exp2(m_prev - m_new)
        acc_scr[h] = alpha * acc_scr[h] + _dot(p.astype(BF16), v_ref[0, h])
        m_scr[h] = m_new

    @pl.when(kv == pl.num_programs(2) - 1)
    def _():
        lane = lax.broadcasted_iota(jnp.int32, (q_ref.shape[2], HEAD_PAD), 1)
        for hp in range(MLA_HEADS // 2):
            a0 = acc_scr[2 * hp]
            a1 = acc_scr[2 * hp + 1]
            o0 = a0 / a0[:, MLA_V:MLA_V + 1]
            o1 = a1 / a1[:, MLA_V:MLA_V + 1]
            pair = jnp.where(lane < MLA_V, o0, pltpu.roll(o1, MLA_V, axis=1))
            o_ref[0, :, hp * HEAD_PAD:(hp + 1) * HEAD_PAD] = pair.astype(BF16)


def _flash(q, k, v, expert_w, layer, tq=1024, tk=2048):
    B, H, S, _ = q.shape
    tq, tk = min(tq, S), min(tk, S)
    assert S % tq == 0 and S % tk == 0
    nq, nk = S // tq, S // tk
    n_steps = B * nq * nk
    qspec = pl.BlockSpec((1, H, tq, HEAD_PAD), lambda b, i, j: (b, 0, i, 0))
    kspec = pl.BlockSpec((1, H, tk, HEAD_PAD), lambda b, i, j: (b, 0, j, 0))
    w_in, w_specs, wb_specs, wb_shapes = [], [], [], []
    for w in expert_w:
        n_l, n_e, r, c = w.shape
        slab = n_e * r // n_steps
        assert n_e * r % n_steps == 0 and slab % 16 == 0
        w_in.append(w.reshape(n_l, n_steps, slab, c))
        w_specs.append(pl.BlockSpec((1, 1, slab, c),
                                    lambda b, i, j: (layer, (b * nq + i) * nk + j, 0, 0)))
        wb_specs.append(pl.BlockSpec((1, slab, c), lambda b, i, j: ((b * nq + i) * nk + j, 0, 0)))
        wb_shapes.append(jax.ShapeDtypeStruct((n_steps, slab, c), BF16))
    o, *wb = pl.pallas_call(
        _flash_kernel,
        grid=(B, nq, nk),
        in_specs=[qspec, kspec, kspec] + w_specs,
        out_specs=[pl.BlockSpec((1, tq, H * MLA_V), lambda b, i, j: (b, i, 0))] + wb_specs,
        out_shape=[jax.ShapeDtypeStruct((B, S, H * MLA_V), BF16)] + wb_shapes,
        scratch_shapes=[pltpu.VMEM((H, tq, LANES), F32), pltpu.VMEM((H, tq, HEAD_PAD), F32)],
        compiler_params=pltpu.CompilerParams(
            dimension_semantics=("parallel", "parallel", "arbitrary"),
            vmem_limit_bytes=FLASH_VMEM_LIMIT),
        name="mla_flash",
    )(q, k, v, *w_in)
    return o, [b16.reshape(w.shape[1:]) for b16, w in zip(wb, expert_w)]


def _log_sigmoid(x):
    return jnp.minimum(x, 0.0) - jnp.log(1.0 + jnp.exp(-jnp.abs(x)))


def _gla_scan(reverse, gq_ref, gk_ref, gv_ref, gd_ref, wd_ref, bd_ref, tri_ref, state_scr,
              o_scr):
    tm = gq_ref.shape[0]
    tg = tri_ref.shape[0]
    n_groups = tm // tg

    @pl.when(pl.program_id(1) == 0)
    def _():
        state_scr[...] = jnp.zeros(state_scr.shape, F32)

    states = [state_scr[h] for h in range(GLA_HEADS)]
    for grp in (range(n_groups - 1, -1, -1) if reverse else range(n_groups)):
        states = _gla_group(reverse, slice(grp * tg, (grp + 1) * tg), gq_ref, gk_ref, gv_ref,
                            gd_ref, wd_ref, bd_ref, tri_ref, states, o_scr)
    for h in range(GLA_HEADS):
        state_scr[h] = states[h]


def _gla_group(reverse, rows, gq_ref, gk_ref, gv_ref, gd_ref, wd_ref, bd_ref, tri_ref, states,
               o_scr):
    tg = tri_ref.shape[0]
    L = GLA_CHUNK
    n_chunks = tg // L
    logits = _dot(gd_ref[rows, :].astype(BF16), wd_ref[...]) + bd_ref[...]
    g = _log_sigmoid(logits) * (1.0 / GLA_GATE_NORM)
    g_hi, g_lo = _split(g)
    b = _dot(tri_ref[...], g_hi) + _dot(tri_ref[...], g_lo)
    edge = 0 if reverse else L - 1
    b_last = jnp.concatenate(
        [jnp.broadcast_to(b[c * L + edge:c * L + edge + 1, :], (L, GLA_W))
         for c in range(n_chunks)], axis=0)
    q_in = (gq_ref[rows, :].astype(F32) * (GLA_DK ** -0.5) * jnp.exp(b)).astype(BF16)
    k_f = gk_ref[rows, :].astype(F32)
    k_in = (k_f * jnp.exp(-b)).astype(BF16)
    k_st = (k_f * jnp.exp(b_last - b)).astype(BF16)
    dec = jnp.exp(b_last)
    v = gv_ref[rows, :]
    lane = lax.broadcasted_iota(jnp.int32, (HEAD_PAD, 2 * L), 1)
    order = range(n_chunks - 1, -1, -1) if reverse else range(n_chunks)
    heads = [slice(h * HEAD_PAD, (h + 1) * HEAD_PAD) for h in range(GLA_HEADS)]
    kv = {}
    o_intra = []
    for h, hs in enumerate(heads):
        a = _dot_nt(q_in[:, hs], k_in[:, hs]).astype(BF16) * tri_ref[...]
        o_intra.append(_dot(a, v[:, hs]))
        v_t = v[:, hs].astype(F32).T
        for c in range(n_chunks):
            pair = slice((c // 2) * 2 * L, (c // 2 + 1) * 2 * L)
            in_chunk = (lane >= L) if c % 2 else (lane < L)
            v_tc = jnp.where(in_chunk, v_t[:, pair], 0.0).astype(BF16)
            kv[c, h] = _dot(v_tc, k_st[pair, hs])
    entering = {}
    leaving = []
    for h, hs in enumerate(heads):
        st = states[h]
        for c in order:
            entering[c, h] = st.astype(BF16)
            st = dec[c * L:c * L + 1, hs] * st + kv[c, h]
        leaving.append(st)
    for h, hs in enumerate(heads):
        o_inter = jnp.concatenate(
            [_dot_nt(q_in[c * L:(c + 1) * L, hs], entering[c, h]) for c in range(n_chunks)],
            axis=0)
        o_scr[rows, hs] = o_intra[h] + o_inter
    return leaving


def _gla_fwd_kernel(gq_ref, gk_ref, gv_ref, gd_ref, wd_ref, bd_ref, tri_ref, o_ref, state_scr,
                    o_scr):
    _gla_scan(False, gq_ref, gk_ref, gv_ref, gd_ref, wd_ref, bd_ref, tri_ref, state_scr, o_scr)
    o_ref[...] = o_scr[...]


def _gla_bwd_kernel(gq_ref, gk_ref, gv_ref, gd_ref, wd_ref, bd_ref, tri_ref, of_ref, gr_ref,
                    ng_ref, out_ref, state_scr, o_scr):
    _gla_scan(True, gq_ref, gk_ref, gv_ref, gd_ref, wd_ref, bd_ref, tri_ref, state_scr, o_scr)
    r = gr_ref[...].astype(F32)
    gate = r * jax.nn.sigmoid(r)
    for h in range(GLA_HEADS):
        hs = slice(h * HEAD_PAD, (h + 1) * HEAD_PAD)
        o = _rms_norm(of_ref[:, hs] + o_scr[:, hs], ng_ref[...])
        out_ref[:, hs] = (o * gate[:, hs]).astype(BF16)


def _gla(gq, gk, gv, gd, gr, wd, bd, tri_f, tri_b, norm_g, B, S, tm=512):
    nb = S // tm
    T = B * S

    def call(reverse):
        blk = (lambda b, i: (b * nb + nb - 1 - i, 0)) if reverse else (lambda b, i: (b * nb + i, 0))
        rows = lambda w: pl.BlockSpec((tm, w), blk)
        const = lambda r, c: pl.BlockSpec((r, c), lambda b, i: (0, 0))
        in_specs = [rows(GLA_W), rows(GLA_W), rows(GLA_W), rows(LANES),
                    const(LANES, GLA_W), const(1, GLA_W), const(GLA_GROUP, GLA_GROUP)]
        scratch = [pltpu.VMEM((GLA_HEADS, GLA_DV, HEAD_PAD), F32), pltpu.VMEM((tm, GLA_W), F32)]
        d = 1 if reverse else 0
        args = [gq, gk, gv, gd, wd[d], bd[d], tri_b if reverse else tri_f]
        if reverse:
            in_specs += [rows(GLA_W), rows(GLA_W), const(1, GLA_DV)]
            args += [o_f, gr, norm_g]
        return pl.pallas_call(
            _gla_bwd_kernel if reverse else _gla_fwd_kernel,
            grid=(B, nb),
            in_specs=in_specs,
            out_specs=rows(GLA_W),
            out_shape=jax.ShapeDtypeStruct((T, GLA_W), BF16 if reverse else F32),
            scratch_shapes=scratch,
            compiler_params=_params("parallel", "arbitrary"),
            name="gla_bwd" if reverse else "gla_fwd",
        )(*args)

    o_f = call(False)
    return call(True)


def _merge_kernel(S, u_ref, up_ref, un_ref, om_ref, gla_ref, gl_ref, h_ref,
                  pw_ref, ps_ref, wa_ref, wb_ref, wc_ref, bg_ref, wo_ref, g_ref, b_ref,
                  rw_ref, ha_ref, afft_ref):
    tm = u_ref.shape[0]
    ext = jnp.concatenate([up_ref[...], u_ref[...], un_ref[...]], axis=0).astype(F32)
    tg = tm // MERGE_GROUPS
    for grp in range(MERGE_GROUPS):
        _merge_rows(S, pl.program_id(1) * tm + grp * tg, slice(grp * tg, (grp + 1) * tg),
                    ext[grp * tg:grp * tg + tg + 2 * POOL_HALO], om_ref, gla_ref, gl_ref, h_ref,
                    pw_ref, ps_ref, wa_ref, wb_ref, wc_ref, bg_ref, wo_ref, g_ref, b_ref,
                    rw_ref, ha_ref, afft_ref)


def _merge_rows(S, first_pos, rows, ext, om_ref, gla_ref, gl_ref, h_ref,
                pw_ref, ps_ref, wa_ref, wb_ref, wc_ref, bg_ref, wo_ref, g_ref, b_ref,
                rw_ref, ha_ref, afft_ref):
    n_ext = ext.shape[0]
    tg = n_ext - 2 * POOL_HALO
    pos_ext = first_pos - POOL_HALO + lax.broadcasted_iota(jnp.int32, (n_ext, LANES), 0)
    in_seq = (pos_ext >= 0) & (pos_ext < S)
    pos = first_pos + lax.broadcasted_iota(jnp.int32, (tg, LANES), 0)
    core = slice(POOL_HALO, POOL_HALO + tg)

    def rows_at(a, d):
        return pltpu.roll(a, (-d) % n_ext, axis=0)

    pooled = []
    for gi, w in enumerate(POOL_WINDOWS):
        hw = w // 2
        cs = slice(gi * LANES, (gi + 1) * LANES)
        x = jnp.where(in_seq, ext[:, cs], 0.0)
        win = rows_at(x, -1) + x
        reach = 1
        while reach < hw:
            win = rows_at(win, -reach) + rows_at(win, reach)
            reach *= 2
        cnt = (jnp.minimum(pos + hw, S) - jnp.maximum(pos - hw, 0)).astype(F32)
        pg = (win[core] / cnt - x[core]).astype(BF16)
        pooled.append(_dot(pg, pw_ref[gi]))
    pa = (jnp.concatenate(pooled, axis=1) * ps_ref[...]).astype(BF16)
    y_a = _dot(pa, wa_ref[...])
    y_b = _dot(om_ref[rows, :], wb_ref[...])
    y_c = _dot(gla_ref[rows, :], wc_ref[...])
    gates2 = 1.0 + jnp.tanh(0.5 * (gl_ref[rows, :].astype(F32) + bg_ref[...]))
    merged2 = (gates2[:, :D_MODEL] * y_a + gates2[:, D_MODEL:2 * D_MODEL] * y_b
               + gates2[:, 2 * D_MODEL:] * y_c)
    mix = _dot(merged2.astype(BF16), wo_ref[...])
    h1 = _layer_norm(DN_ALPHA * h_ref[rows, :] + mix, g_ref[...], b_ref[...])
    lane = lax.broadcasted_iota(jnp.int32, (tg, LANES), 1)
    logits = jnp.where(lane < N_EXPERTS, _dot3(h1, rw_ref[...]), -jnp.inf)
    e = jnp.exp(logits - jnp.max(logits, axis=1, keepdims=True))
    aff = e / jnp.sum(e, axis=1, keepdims=True)
    afft_ref[0, :, rows] = aff.T[:N_EXPERTS]
    ha_ref[rows, :D_MODEL] = DN_ALPHA * h1
    ha_ref[rows, D_MODEL:2 * D_MODEL] = h1
    ha_ref[rows, 2 * D_MODEL:] = aff


def _merge(u, o_mla, gla_o, gl, h, pw, ps, wa, wb, wc, bg, wo, g, b, rw, B, S, tm=512):
    nb = S // tm
    T = B * S
    hb = tm // POOL_HALO
    n_halo = T // POOL_HALO
    rows = lambda w: pl.BlockSpec((tm, w), lambda bb, i: (bb * nb + i, 0))
    prev = pl.BlockSpec((POOL_HALO, POOL_DIM),
                        lambda bb, i: (jnp.maximum((bb * nb + i) * hb - 1, 0), 0))
    nxt = pl.BlockSpec((POOL_HALO, POOL_DIM),
                       lambda bb, i: (jnp.minimum((bb * nb + i + 1) * hb, n_halo - 1), 0))
    c2 = lambda r, c: pl.BlockSpec((r, c), lambda bb, i: (0, 0))
    c3 = lambda a, r, c: pl.BlockSpec((a, r, c), lambda bb, i: (0, 0, 0))
    return pl.pallas_call(
        functools.partial(_merge_kernel, S),
        grid=(B, nb),
        in_specs=[rows(POOL_DIM), prev, nxt,
                  rows(MLA_HEADS * MLA_V), rows(GLA_W), rows(3 * D_MODEL), rows(D_MODEL),
                  c3(len(POOL_WINDOWS), LANES, LANES), c2(1, POOL_DIM), c2(POOL_DIM, D_MODEL),
                  c2(MLA_HEADS * MLA_V, D_MODEL), c2(GLA_W, D_MODEL), c2(1, 3 * D_MODEL),
                  c2(D_MODEL, D_MODEL), c2(1, D_MODEL), c2(1, D_MODEL), c2(D_MODEL, LANES)],
        out_specs=[rows(HA_W),
                   pl.BlockSpec((1, N_EXPERTS, tm), lambda bb, i: (bb, 0, i))],
        out_shape=[jax.ShapeDtypeStruct((T, HA_W), F32),
                   jax.ShapeDtypeStruct((B, N_EXPERTS, S), F32)],
        compiler_params=_params("parallel", "parallel"),
        name="merge",
    )(u, u, u, o_mla, gla_o, gl, h, pw, ps, wa, wb, wc, bg, wo, g, b, rw)


def _topk_kernel(C, aff_ref, tri_ref, idx_ref, p_scr):
    S = aff_ref.shape[2]
    n_chunks = S // LANES
    aff = aff_ref[0]

    def count(mask):
        return jnp.sum(jnp.where(mask, 1.0, 0.0), axis=1, keepdims=True)

    def as_float(bits):
        return lax.bitcast_convert_type(bits, F32)

    def refine(i, thr):
        cand = thr | jnp.left_shift(jnp.int32(1), 30 - i)
        return jnp.where(count(aff >= as_float(cand)) >= C, cand, thr)

    thr = lax.fori_loop(0, 31, refine, jnp.zeros((N_EXPERTS, 1), jnp.int32))
    above = aff >= as_float(thr + 1)
    tied = (aff >= as_float(thr)) & jnp.logical_not(above)
    need = C - count(above)
    tri = tri_ref[...]

    tied_f = tied.astype(F32)
    run = jnp.zeros((N_EXPERTS, 1), F32)
    sel_parts = []
    for c in range(n_chunks):
        cs = slice(c * LANES, (c + 1) * LANES)
        incl = _dot(tied_f[:, cs].astype(BF16), tri) + run
        run = incl[:, LANES - 1:LANES]
        sel_parts.append(jnp.where(above[:, cs] | (tied[:, cs] & (incl <= need)), 1.0, 0.0))

    p_scr[...] = jnp.zeros(p_scr.shape, F32)
    lane_e = lax.broadcasted_iota(jnp.int32, (N_EXPERTS, LANES), 1)
    chunk_end = jnp.full((N_EXPERTS, LANES), NEVER, F32)
    run = jnp.zeros((N_EXPERTS, 1), F32)
    for c in range(n_chunks):
        rel = _dot(sel_parts[c].astype(BF16), tri)
        p_scr[:, c, :] = rel
        run = run + rel[:, LANES - 1:LANES]
        chunk_end = jnp.where(lane_e == c, run, chunk_end)

    slot = lax.broadcasted_iota(jnp.int32, (C, LANES), 0).astype(F32)
    lane_c = lax.broadcasted_iota(jnp.int32, (C, LANES), 1).astype(F32)
    ones = jnp.ones((8, LANES), BF16)
    for e in range(N_EXPERTS):
        ce = chunk_end[e:e + 1, :]
        full = ce <= slot
        n_full = jnp.sum(jnp.where(full, 1.0, 0.0), axis=1, keepdims=True)
        base = jnp.max(jnp.where(full, ce, 0.0), axis=1, keepdims=True)
        pick = jnp.where(lane_c == n_full, 1.0, 0.0).astype(BF16)
        rel = _dot(pick, p_scr[e].astype(BF16))
        w = jnp.where(rel <= slot - base, 1.0, 0.0) + jnp.where(full, float(LANES), 0.0)
        tok = _dot_nt(ones, w.astype(BF16))[0:1, :]
        idx_ref[0, e:e + 1, :] = tok.astype(jnp.int32)


def _topk(afft, tri, C):
    B, E, S = afft.shape
    assert S // LANES <= LANES
    return pl.pallas_call(
        functools.partial(_topk_kernel, C),
        grid=(B,),
        in_specs=[pl.BlockSpec((1, E, S), lambda b: (b, 0, 0)),
                  pl.BlockSpec((LANES, LANES), lambda b: (0, 0))],
        out_specs=pl.BlockSpec((1, E, C), lambda b: (b, 0, 0)),
        out_shape=jax.ShapeDtypeStruct((B, E, C), jnp.int32),
        scratch_shapes=[pltpu.VMEM((E, LANES, LANES), F32)],
        compiler_params=_params("parallel"),
        name="expert_choice",
    )(afft, tri)


def _ffn_kernel(tc, n_steps, idx_ref, ha_in_ref, wg_ref, wu_ref, wd_ref, ha_ref,
                gbuf, sbuf, gsem, ssem):
    del ha_in_ref
    e = pl.program_id(0)
    s = (e * pl.num_programs(1) + pl.program_id(1)) * pl.num_programs(2) + pl.program_id(2)
    slot = s % 2

    def gather_copy(step, r, sl):
        row = idx_ref[step * tc + r]
        return pltpu.make_async_copy(ha_ref.at[pl.ds(row, 1), :],
                                     gbuf.at[sl, pl.ds(r, 1), :], gsem.at[sl])

    def scatter_copy(step, r, sl):
        row = idx_ref[step * tc + r]
        return pltpu.make_async_copy(sbuf.at[sl, pl.ds(r, 1), :],
                                     ha_ref.at[pl.ds(row, 1), pl.ds(0, D_MODEL)], ssem.at[sl])

    def start_rows(copy, step, sl):
        def body(r, carry):
            copy(step, r, sl).start()
            return carry
        lax.fori_loop(0, tc, body, 0, unroll=DMA_UNROLL)

    def wait_gather(sl):
        pltpu.make_async_copy(ha_ref.at[pl.ds(0, tc), :], gbuf.at[sl], gsem.at[sl]).wait()

    def wait_scatter(sl):
        pltpu.make_async_copy(sbuf.at[sl], ha_ref.at[pl.ds(0, tc), pl.ds(0, D_MODEL)],
                              ssem.at[sl]).wait()

    @pl.when(s == 0)
    def _():
        start_rows(gather_copy, s, slot)

    @pl.when(s >= 2)
    def _():
        wait_scatter(slot)

    @pl.when(s + 1 < n_steps)
    def _():
        start_rows(gather_copy, s + 1, 1 - slot)

    wait_gather(slot)

    rows = gbuf[slot]
    x = rows[:, D_MODEL:2 * D_MODEL].astype(BF16)
    lane = lax.broadcasted_iota(jnp.int32, (tc, LANES), 1)
    gate = jnp.sum(jnp.where(lane == e, rows[:, 2 * D_MODEL:], 0.0), axis=1, keepdims=True)
    y = jnp.zeros((tc, D_MODEL), F32)
    fh = D_EXPERT // 2
    for f in range(2):
        fs = slice(f * fh, (f + 1) * fh)
        hg = _dot(x, wg_ref[0, :, fs])
        hu = _dot(x, wu_ref[0, :, fs])
        hid = (hg * jax.nn.sigmoid(hg) * hu).astype(BF16)
        y += _dot(hid, wd_ref[0, fs, :])
    sbuf[slot] = rows[:, :D_MODEL] + gate * y
    start_rows(scatter_copy, s, slot)

    @pl.when(s == n_steps - 1)
    def _():
        if n_steps >= 2:
            wait_scatter(1 - slot)
        wait_scatter(slot)


def _ffn(ha, idx_steps, wg, wu, wd, B, C, tc=512):
    nblk = C // tc
    n_steps = N_EXPERTS * B * nblk
    wspec = lambda r, c: pl.BlockSpec((1, r, c), lambda e, b, j, idx: (e, 0, 0))
    return pl.pallas_call(
        functools.partial(_ffn_kernel, tc, n_steps),
        grid_spec=pltpu.PrefetchScalarGridSpec(
            num_scalar_prefetch=1,
            grid=(N_EXPERTS, B, nblk),
            in_specs=[pl.BlockSpec(memory_space=pl.ANY),
                      wspec(D_MODEL, D_EXPERT), wspec(D_MODEL, D_EXPERT),
                      wspec(D_EXPERT, D_MODEL)],
            out_specs=pl.BlockSpec(memory_space=pl.ANY),
            scratch_shapes=[pltpu.VMEM((2, tc, HA_W), F32), pltpu.VMEM((2, tc, D_MODEL), F32),
                            pltpu.SemaphoreType.DMA((2,)), pltpu.SemaphoreType.DMA((2,))]),
        out_shape=jax.ShapeDtypeStruct(ha.shape, F32),
        input_output_aliases={1: 0},
        compiler_params=_params("arbitrary", "arbitrary", "arbitrary"),
        name="expert_ffn",
    )(idx_steps, ha, wg, wu, wd)


def _ln_kernel(x_ref, g_ref, b_ref, o_ref):
    o_ref[...] = _layer_norm(x_ref[...], g_ref[...], b_ref[...])


def _ln_rows(xin, g, b, tm=512):
    T = xin.shape[0]
    const = lambda i: (0, 0)
    return pl.pallas_call(
        _ln_kernel,
        grid=(T // tm,),
        in_specs=[pl.BlockSpec((tm, D_MODEL), lambda i: (i, 0)),
                  pl.BlockSpec((1, D_MODEL), const), pl.BlockSpec((1, D_MODEL), const)],
        out_specs=pl.BlockSpec((tm, D_MODEL), lambda i: (i, 0)),
        out_shape=jax.ShapeDtypeStruct((T, D_MODEL), F32),
        compiler_params=_params("parallel"),
        name="final_ln",
    )(xin, g, b)


def _pad_heads(w, heads, width):
    lead = w.shape[:-1]
    w = w.reshape(lead + (heads, width))
    w = jnp.pad(w, [(0, 0)] * len(lead) + [(0, 0), (0, HEAD_PAD - width)])
    return w.reshape(lead + (heads * HEAD_PAD,))


def _rot_half(w):
    half = w.shape[-1] // 2
    return jnp.concatenate([-w[..., half:], w[..., :half]], axis=-1)


def _pack_w_in(w):
    o = 0
    parts = {}
    for name, width in (("pool", POOL_DIM), ("cq", MLA_Q_RANK), ("ckv", MLA_KV_RANK),
                        ("kr", MLA_ROPE), ("gq", GLA_HEADS * GLA_DK), ("gk", GLA_HEADS * GLA_DK),
                        ("gv", GLA_HEADS * GLA_DV), ("gr", GLA_HEADS * GLA_DV),
                        ("gd", 2 * GLA_GATE_RANK), ("gates", 3 * D_MODEL)):
        parts[name] = w[:, o:o + width]
        o += width
    rope_slot = lambda m: jnp.pad(m, ((0, 0), (MLA_NOPE, HEAD_PAD - MLA_NOPE - MLA_ROPE)))
    kr2 = jnp.concatenate([rope_slot(parts["kr"]), rope_slot(_rot_half(parts["kr"]))], axis=1)
    gd = jnp.pad(parts["gd"], ((0, 0), (0, LANES - 2 * GLA_GATE_RANK)))
    cols = [parts["pool"], parts["cq"], parts["ckv"], kr2,
            _pad_heads(parts["gq"], GLA_HEADS, GLA_DK), _pad_heads(parts["gk"], GLA_HEADS, GLA_DK),
            parts["gv"], parts["gr"], gd, parts["gates"]]
    return jnp.concatenate(cols, axis=1).astype(BF16)


def _pack_mla(w_uq, w_ukv):
    r = w_uq.shape[0]
    uq = w_uq.reshape(r, MLA_HEADS, MLA_NOPE + MLA_ROPE)
    nope, rope = uq[..., :MLA_NOPE], uq[..., MLA_NOPE:]
    tail = jnp.zeros((r, MLA_HEADS, HEAD_PAD - MLA_NOPE - MLA_ROPE), F32)
    wq = jnp.concatenate([nope, rope, tail], axis=-1).reshape(r, -1)
    wq2 = jnp.concatenate([jnp.zeros_like(nope), _rot_half(rope), tail], axis=-1).reshape(r, -1)
    rk = w_ukv.shape[0]
    ukv = w_ukv.reshape(rk, MLA_HEADS, MLA_NOPE + MLA_V)
    wk = _pad_heads(ukv[..., :MLA_NOPE].reshape(rk, -1), MLA_HEADS, MLA_NOPE)
    wv = _pad_heads(ukv[..., MLA_NOPE:].reshape(rk, -1), MLA_HEADS, MLA_V)
    return wq.astype(BF16), wq2.astype(BF16), wk.astype(BF16), wv.astype(BF16)


def _pack_gla_decay(w_dec, b_dec):
    wd = []
    for d in range(2):
        rows = _pad_heads(w_dec[d], GLA_HEADS, GLA_DK)
        wd.append(jnp.pad(rows, ((d * GLA_GATE_RANK, LANES - (d + 1) * GLA_GATE_RANK),
                                 (0, 0))).astype(BF16))
    bd = [_pad_heads(b_dec[d][None, :], GLA_HEADS, GLA_DK) for d in range(2)]
    return wd, bd


def _chunk_tri(tm, reverse):
    r = jnp.arange(tm)[:, None]
    c = jnp.arange(tm)[None, :]
    same = (r // GLA_CHUNK) == (c // GLA_CHUNK)
    return (same & ((c >= r) if reverse else (c <= r))).astype(BF16)


def kernel(x, positions, ln0_g, ln0_b, w_in, b_gate, pool_w, pool_scale, w_up_a, mla_q_norm,
           mla_w_uq, mla_kv_norm, mla_w_ukv, w_up_b, gla_w_dec, gla_b_dec, gla_norm, w_up_c,
           w_out, ln1_g, ln1_b, router_w, exp_w_gate, exp_w_up, exp_w_down, ln2_g, ln2_b):
    B, S, D = x.shape
    assert D == D_MODEL and S % 512 == 0 and B == 2
    T = B * S
    C = CAPACITY_FACTOR * S // N_EXPERTS
    tc = min(512, C)
    row = lambda v: v.reshape(1, -1).astype(F32)

    half = MLA_ROPE // 2
    freqs = ROPE_THETA ** (-jnp.arange(half, dtype=F32) / half)
    lanes = jnp.arange(LANES)
    in_rope = (lanes >= MLA_NOPE) & (lanes < MLA_NOPE + MLA_ROPE)
    freq_row = jnp.where(in_rope, freqs[(lanes - MLA_NOPE) % half], 0.0).reshape(1, LANES)
    rope_mask = in_rope.astype(F32).reshape(1, LANES)
    nope_row = (lanes < MLA_NOPE).astype(F32).reshape(1, LANES)
    one_row = (lanes == MLA_V).astype(F32).reshape(1, LANES)
    cos_t, sin_t = _rope_tables(positions.reshape(T, 1), freq_row, rope_mask)

    tri_f, tri_b = _chunk_tri(GLA_GROUP, False), _chunk_tri(GLA_GROUP, True)
    tri_lane = (jnp.arange(LANES)[:, None] <= jnp.arange(LANES)[None, :]).astype(BF16)

    stream, g_in, b_in = x.reshape(T, D), ln0_g, ln0_b
    for l in range(DEPTH):
        h, u, cq, ckv, kr2, gq, gk, gv, gr, gd, gl = _ln_inproj(
            stream, row(g_in), row(b_in), _pack_w_in(w_in[l]))
        wq, wq2, wk, wv = _pack_mla(mla_w_uq[l], mla_w_ukv[l])
        q, k, v = _mla_prep(cq, ckv, kr2, cos_t, sin_t, row(mla_q_norm[l]), row(mla_kv_norm[l]),
                            wq, wq2, wk, wv, nope_row, one_row, B, S)
        o_mla, (wg_b, wu_b, wd_b) = _flash(q, k, v, (exp_w_gate, exp_w_up, exp_w_down), l)
        o_mla = o_mla.reshape(T, MLA_HEADS * MLA_V)
        wd, bd = _pack_gla_decay(gla_w_dec[l], gla_b_dec[l])
        gla_o = _gla(gq, gk, gv, gd, gr, wd, bd, tri_f, tri_b, row(gla_norm[l]), B, S)
        rw = jnp.pad(router_w[l], ((0, 0), (0, LANES - N_EXPERTS)))
        ha, afft = _merge(u, o_mla, gla_o, gl, h, pool_w[l].astype(BF16), row(pool_scale[l]),
                          w_up_a[l].astype(BF16), w_up_b[l].astype(BF16),
                          w_up_c[l].astype(BF16), row(b_gate[l]),
                          (0.5 * w_out[l]).astype(BF16), row(ln1_g[l]), row(ln1_b[l]), rw, B, S)
        idx = _topk(afft, tri_lane, C)
        idx_steps = (idx + (jnp.arange(B, dtype=jnp.int32) * S)[:, None, None])
        idx_steps = idx_steps.transpose(1, 0, 2).reshape(-1)
        stream = _ffn(ha, idx_steps, wg_b, wu_b, wd_b, B, C, tc)
        g_in, b_in = ln2_g[l], ln2_b[l]
    out = _ln_rows(stream, row(g_in), row(b_in))
    return out.reshape(B, S, D)
```

```python
import functools

import jax
import jax.numpy as jnp
from jax import lax
from jax.experimental import pallas as pl
from jax.experimental.pallas import tpu as pltpu

F32 = jnp.float32
BF16 = jnp.bfloat16

LANES = 128
VMEM_LIMIT = 56 * 1024 * 1024
FLASH_VMEM_LIMIT = 60 * 1024 * 1024

D_MODEL = 1024
DEPTH = 2
POOL_WINDOWS = (2, 4, 8, 16)
POOL_DIM = 512
POOL_HALO = 16
MLA_HEADS = 8
MLA_NOPE = 64
MLA_ROPE = 32
MLA_V = 64
MLA_Q_RANK = 384
MLA_KV_RANK = 256
ROPE_THETA = 10000.0
GLA_HEADS = 4
GLA_DK = 64
GLA_DV = 128
GLA_GATE_RANK = 16
GLA_GATE_NORM = 16.0
GLA_CHUNK = 64
N_EXPERTS = 16
CAPACITY_FACTOR = 2
D_EXPERT = 2048
DN_ALPHA = (2 * DEPTH) ** 0.25
LN_EPS = 1e-5
RMS_EPS = 1e-6
LOG2_E = 1.4426950408889634
NEVER = 1e9

HEAD_PAD = LANES
GLA_W = GLA_HEADS * HEAD_PAD
HA_W = 2 * D_MODEL + LANES
DMA_UNROLL = 16
MERGE_GROUPS = 2
GLA_GROUP = 256

_SEG = (("pool", POOL_DIM), ("cq", MLA_Q_RANK), ("ckv", MLA_KV_RANK), ("kr2", 2 * HEAD_PAD),
        ("gq", GLA_W), ("gk", GLA_W), ("gv", GLA_W), ("gr", GLA_W), ("gd", LANES),
        ("gates", 3 * D_MODEL))
_SEG_OFF = {}
_off = 0
for _n, _w in _SEG:
    _SEG_OFF[_n] = (_off, _w)
    _off += _w
W_ALL_COLS = _off


def _params(*sem):
    return pltpu.CompilerParams(dimension_semantics=sem, vmem_limit_bytes=VMEM_LIMIT)


def _dot(a, b):
    return jnp.dot(a, b, preferred_element_type=F32)


def _dot_nt(a, b):
    return lax.dot_general(a, b, (((1,), (1,)), ((), ())), preferred_element_type=F32)


def _split(x):
    hi = x.astype(BF16)
    lo = (x - hi.astype(F32)).astype(BF16)
    return hi, lo


def _dot3(a, b, nt=False):
    d = _dot_nt if nt else _dot
    ah, al = _split(a)
    bh, bl = _split(b)
    return d(ah, bh) + d(ah, bl) + d(al, bh)


def _layer_norm(x, g, b):
    mu = jnp.mean(x, axis=-1, keepdims=True)
    xc = x - mu
    var = jnp.mean(xc * xc, axis=-1, keepdims=True)
    return xc * lax.rsqrt(var + LN_EPS) * g + b


def _rms_norm(x, g):
    ms = jnp.mean(x * x, axis=-1, keepdims=True)
    return x * lax.rsqrt(ms + RMS_EPS) * g


def _rope_kernel(pos_ref, freq_ref, mask_ref, cos_ref, sin_ref):
    ang = pos_ref[...].astype(F32) * freq_ref[...]
    cos_ref[...] = jnp.cos(ang) * mask_ref[...]
    sin_ref[...] = jnp.sin(ang) * mask_ref[...]


def _rope_tables(pos_col, freq_row, mask_row, tm=512):
    T = pos_col.shape[0]
    row = pl.BlockSpec((1, LANES), lambda i: (0, 0))
    out = pl.BlockSpec((tm, LANES), lambda i: (i, 0))
    return pl.pallas_call(
        _rope_kernel,
        grid=(T // tm,),
        in_specs=[pl.BlockSpec((tm, 1), lambda i: (i, 0)), row, row],
        out_specs=[out, out],
        out_shape=[jax.ShapeDtypeStruct((T, LANES), F32)] * 2,
        compiler_params=_params("parallel"),
        name="rope_tables",
    )(pos_col, freq_row, mask_row)


def _inproj_kernel(x_ref, xn_ref, g_ref, b_ref, w_ref, h_ref, *rest):
    out_refs, h_scr = rest[:-1], rest[-1]

    @pl.when(pl.program_id(0) == 0)
    def _():
        h_scr[...] = _layer_norm(x_ref[...], g_ref[...], b_ref[...])

    h = h_scr[...]
    h_ref[...] = h
    hb = h.astype(BF16)
    for (name, width), o_ref in zip(_SEG, out_refs):
        off = _SEG_OFF[name][0]
        o_ref[...] = _dot(hb, w_ref[:, off:off + width]).astype(o_ref.dtype)
    h_scr[...] = _layer_norm(xn_ref[...], g_ref[...], b_ref[...])


def _ln_inproj(xin, g, b, w_all, tm=512):
    T = xin.shape[0]
    n_steps = T // tm
    const = lambda i: (0, 0)
    rows = lambda w: pl.BlockSpec((tm, w), lambda i: (i, 0))
    out_shapes = [jax.ShapeDtypeStruct((T, D_MODEL), F32)]
    out_specs = [rows(D_MODEL)]
    for name, width in _SEG:
        out_shapes.append(jax.ShapeDtypeStruct((T, width), F32 if name == "gd" else BF16))
        out_specs.append(rows(width))
    return pl.pallas_call(
        _inproj_kernel,
        grid=(n_steps,),
        in_specs=[rows(D_MODEL),
                  pl.BlockSpec((tm, D_MODEL), lambda i: (jnp.minimum(i + 1, n_steps - 1), 0)),
                  pl.BlockSpec((1, D_MODEL), const), pl.BlockSpec((1, D_MODEL), const),
                  pl.BlockSpec((D_MODEL, W_ALL_COLS), const)],
        out_specs=out_specs,
        out_shape=out_shapes,
        scratch_shapes=[pltpu.VMEM((tm, D_MODEL), F32)],
        compiler_params=_params("arbitrary"),
        name="ln_inproj",
    )(xin, xin, g, b, w_all)


def _mla_prep_kernel(cq_ref, ckv_ref, kr2_ref, cos_ref, sin_ref, qn_ref, kvn_ref,
                     wq_ref, wq2_ref, wk_ref, wv_ref, nope_ref, one_ref,
                     q_ref, k_ref, v_ref):
    scale = (MLA_NOPE + MLA_ROPE) ** -0.5 * LOG2_E
    cosr = cos_ref[...]
    sinr = sin_ref[...]
    nq = _rms_norm(cq_ref[...].astype(F32), qn_ref[...]).astype(BF16)
    nkv = _rms_norm(ckv_ref[...].astype(F32), kvn_ref[...]).astype(BF16)
    qa = _dot(nq, wq_ref[...])
    qb = _dot(nq, wq2_ref[...])
    ka = _dot(nkv, wk_ref[...])
    va = _dot(nkv, wv_ref[...])
    kr2 = kr2_ref[...].astype(F32)
    k_rope = kr2[:, :HEAD_PAD] * cosr + kr2[:, HEAD_PAD:] * sinr
    q_cos = (cosr + nope_ref[...]) * scale
    q_sin = sinr * scale
    for h in range(MLA_HEADS):
        sl = slice(h * HEAD_PAD, (h + 1) * HEAD_PAD)
        q_ref[0, h] = (qa[:, sl] * q_cos + qb[:, sl] * q_sin).astype(BF16)
        k_ref[0, h] = (ka[:, sl] + k_rope).astype(BF16)
        v_ref[0, h] = (va[:, sl] + one_ref[...]).astype(BF16)


def _mla_prep(cq, ckv, kr2, cos_t, sin_t, qn, kvn, wq, wq2, wk, wv, nope_row, one_row, B, S,
              tm=512):
    nb = S // tm
    rows = lambda w: pl.BlockSpec((tm, w), lambda b, i: (b * nb + i, 0))
    const = lambda r, c: pl.BlockSpec((r, c), lambda b, i: (0, 0))
    hw = MLA_HEADS * HEAD_PAD
    out = pl.BlockSpec((1, MLA_HEADS, tm, HEAD_PAD), lambda b, i: (b, 0, i, 0))
    return pl.pallas_call(
        _mla_prep_kernel,
        grid=(B, nb),
        in_specs=[rows(MLA_Q_RANK), rows(MLA_KV_RANK), rows(2 * HEAD_PAD), rows(LANES),
                  rows(LANES), const(1, MLA_Q_RANK), const(1, MLA_KV_RANK),
                  const(MLA_Q_RANK, hw), const(MLA_Q_RANK, hw), const(MLA_KV_RANK, hw),
                  const(MLA_KV_RANK, hw), const(1, LANES), const(1, LANES)],
        out_specs=[out, out, out],
        out_shape=[jax.ShapeDtypeStruct((B, MLA_HEADS, S, HEAD_PAD), BF16)] * 3,
        compiler_params=_params("parallel", "parallel"),
        name="mla_prep",
    )(cq, ckv, kr2, cos_t, sin_t, qn, kvn, wq, wq2, wk, wv, nope_row, one_row)


def _flash_kernel(q_ref, k_ref, v_ref, wg_ref, wu_ref, wd_ref, o_ref, wgb_ref, wub_ref, wdb_ref,
                  m_scr, acc_scr):
    kv = pl.program_id(2)
    wgb_ref[...] = wg_ref[0].astype(BF16)
    wub_ref[...] = wu_ref[0].astype(BF16)
    wdb_ref[...] = wd_ref[0].astype(BF16)

    @pl.when(kv == 0)
    def _():
        m_scr[...] = jnp.full(m_scr.shape, -jnp.inf, F32)
        acc_scr[...] = jnp.zeros(acc_scr.shape, F32)

    tk = k_ref.shape[2]
    for h in range(MLA_HEADS):
        s = _dot_nt(q_ref[0, h], k_ref[0, h])
        m_prev = m_scr[h]
        m_new = jnp.maximum(m_prev, jnp.max(s, axis=1, keepdims=True))
        p = jnp.exp2(s - jnp.concatenate([m_new] * (tk // LANES), axis=1))
        alpha = jnp.exp2(m_prev - m_new)
        acc_scr[h] = alpha * acc_scr[h] + _dot(p.astype(BF16), v_ref[0, h])
        m_scr[h] = m_new

    @pl.when(kv == pl.num_programs(2) - 1)
    def _():
        lane = lax.broadcasted_iota(jnp.int32, (q_ref.shape[2], HEAD_PAD), 1)
        for hp in range(MLA_HEADS // 2):
            a0 = acc_scr[2 * hp]
            a1 = acc_scr[2 * hp + 1]
            o0 = a0 / a0[:, MLA_V:MLA_V + 1]
            o1 = a1 / a1[:, MLA_V:MLA_V + 1]
            pair = jnp.where(lane < MLA_V, o0, pltpu.roll(o1, MLA_V, axis=1))
            o_ref[0, :, hp * HEAD_PAD:(hp + 1) * HEAD_PAD] = pair.astype(BF16)


def _flash(q, k, v, expert_w, layer, tq=1024, tk=2048):
    B, H, S, _ = q.shape
    tq, tk = min(tq, S), min(tk, S)
    assert S % tq == 0 and S % tk == 0
    nq, nk = S // tq, S // tk
    n_steps = B * nq * nk
    qspec = pl.BlockSpec((1, H, tq, HEAD_PAD), lambda b, i, j: (b, 0, i, 0))
    kspec = pl.BlockSpec((1, H, tk, HEAD_PAD), lambda b, i, j: (b, 0, j, 0))
    w_in, w_specs, wb_specs, wb_shapes = [], [], [], []
    for w in expert_w:
        n_l, n_e, r, c = w.shape
        slab = n_e * r // n_steps
        assert n_e * r % n_steps == 0 and slab % 16 == 0
        w_in.append(w.reshape(n_l, n_steps, slab, c))
        w_specs.append(pl.BlockSpec((1, 1, slab, c),
                                    lambda b, i, j: (layer, (b * nq + i) * nk + j, 0, 0)))
        wb_specs.append(pl.BlockSpec((1, slab, c), lambda b, i, j: ((b * nq + i) * nk + j, 0, 0)))
        wb_shapes.append(jax.ShapeDtypeStruct((n_steps, slab, c), BF16))
    o, *wb = pl.pallas_call(
        _flash_kernel,
        grid=(B, nq, nk),
        in_specs=[qspec, kspec, kspec] + w_specs,
        out_specs=[pl.BlockSpec((1, tq, H * MLA_V), lambda b, i, j: (b, i, 0))] + wb_specs,
        out_shape=[jax.ShapeDtypeStruct((B, S, H * MLA_V), BF16)] + wb_shapes,
        scratch_shapes=[pltpu.VMEM((H, tq, LANES), F32), pltpu.VMEM((H, tq, HEAD_PAD), F32)],
        compiler_params=pltpu.CompilerParams(
            dimension_semantics=("parallel", "parallel", "arbitrary"),
            vmem_limit_bytes=FLASH_VMEM_LIMIT),
        name="mla_flash",
    )(q, k, v, *w_in)
    return o, [b16.reshape(w.shape[1:]) for b16, w in zip(wb, expert_w)]


def _log_sigmoid(x):
    return jnp.minimum(x, 0.0) - jnp.log(1.0 + jnp.exp(-jnp.abs(x)))


def _gla_scan(reverse, gq_ref, gk_ref, gv_ref, gd_ref, wd_ref, bd_ref, tri_ref, state_scr,
              emit):
    tm = gq_ref.shape[0]
    tg = tri_ref.shape[0]
    n_groups = tm // tg

    @pl.when(pl.program_id(1) == 0)
    def _():
        state_scr[...] = jnp.zeros(state_scr.shape, F32)

    states = [state_scr[h] for h in range(GLA_HEADS)]
    for grp in (range(n_groups - 1, -1, -1) if reverse else range(n_groups)):
        states = _gla_group(reverse, slice(grp * tg, (grp + 1) * tg), gq_ref, gk_ref, gv_ref,
                            gd_ref, wd_ref, bd_ref, tri_ref, states, emit)
    for h in range(GLA_HEADS):
        state_scr[h] = states[h]


def _gla_group(reverse, rows, gq_ref, gk_ref, gv_ref, gd_ref, wd_ref, bd_ref, tri_ref, states,
               emit):
    tg = tri_ref.shape[0]
    L = GLA_CHUNK
    n_chunks = tg // L
    logits = _dot(gd_ref[rows, :].astype(BF16), wd_ref[...]) + bd_ref[...]
    g = _log_sigmoid(logits) * (1.0 / GLA_GATE_NORM)
    g_hi, g_lo = _split(g)
    b = _dot(tri_ref[...], g_hi) + _dot(tri_ref[...], g_lo)
    edge = 0 if reverse else L - 1
    b_last = jnp.concatenate(
        [jnp.broadcast_to(b[c * L + edge:c * L + edge + 1, :], (L, GLA_W))
         for c in range(n_chunks)], axis=0)
    q_in = (gq_ref[rows, :].astype(F32) * (GLA_DK ** -0.5) * jnp.exp(b)).astype(BF16)
    k_f = gk_ref[rows, :].astype(F32)
    k_in = (k_f * jnp.exp(-b)).astype(BF16)
    k_st = (k_f * jnp.exp(b_last - b)).astype(BF16)
    dec = jnp.exp(b_last)
    v = gv_ref[rows, :]
    lane = lax.broadcasted_iota(jnp.int32, (HEAD_PAD, 2 * L), 1)
    order = range(n_chunks - 1, -1, -1) if reverse else range(n_chunks)
    heads = [slice(h * HEAD_PAD, (h + 1) * HEAD_PAD) for h in range(GLA_HEADS)]
    kv = {}
    o_intra = []
    for h, hs in enumerate(heads):
        a = _dot_nt(q_in[:, hs], k_in[:, hs]).astype(BF16) * tri_ref[...]
        o_intra.append(_dot(a, v[:, hs]))
        v_t = v[:, hs].astype(F32).T
        for c in range(n_chunks):
            pair = slice((c // 2) * 2 * L, (c // 2 + 1) * 2 * L)
            in_chunk = (lane >= L) if c % 2 else (lane < L)
            v_tc = jnp.where(in_chunk, v_t[:, pair], 0.0).astype(BF16)
            kv[c, h] = _dot(v_tc, k_st[pair, hs])
    entering = {}
    leaving = []
    for h, hs in enumerate(heads):
        st = states[h]
        for c in order:
            entering[c, h] = st.astype(BF16)
            st = dec[c * L:c * L + 1, hs] * st + kv[c, h]
        leaving.append(st)
    for h, hs in enumerate(heads):
        o_inter = jnp.concatenate(
            [_dot_nt(q_in[c * L:(c + 1) * L, hs], entering[c, h]) for c in range(n_chunks)],
            axis=0)
        emit(rows, hs, o_intra[h] + o_inter)
    return leaving


def _gla_fwd_kernel(gq_ref, gk_ref, gv_ref, gd_ref, wd_ref, bd_ref, tri_ref, o_ref, state_scr):
    def emit(rows, hs, tile):
        o_ref[rows, hs] = tile

    _gla_scan(False, gq_ref, gk_ref, gv_ref, gd_ref, wd_ref, bd_ref, tri_ref, state_scr, emit)


def _gla_bwd_kernel(gq_ref, gk_ref, gv_ref, gd_ref, wd_ref, bd_ref, tri_ref, of_ref, gr_ref,
                    ng_ref, out_ref, state_scr):
    def emit(rows, hs, tile):
        r = gr_ref[rows, hs].astype(F32)
        o = _rms_norm(of_ref[rows, hs] + tile, ng_ref[...])
        out_ref[rows, hs] = (o * (r * jax.nn.sigmoid(r))).astype(BF16)

    _gla_scan(True, gq_ref, gk_ref, gv_ref, gd_ref, wd_ref, bd_ref, tri_ref, state_scr, emit)


def _gla(gq, gk, gv, gd, gr, wd, bd, tri_f, tri_b, norm_g, B, S, tm=512):
    nb = S // tm
    T = B * S

    def call(reverse):
        blk = (lambda b, i: (b * nb + nb - 1 - i, 0)) if reverse else (lambda b, i: (b * nb + i, 0))
        rows = lambda w: pl.BlockSpec((tm, w), blk)
        const = lambda r, c: pl.BlockSpec((r, c), lambda b, i: (0, 0))
        in_specs = [rows(GLA_W), rows(GLA_W), rows(GLA_W), rows(LANES),
                    const(LANES, GLA_W), const(1, GLA_W), const(GLA_GROUP, GLA_GROUP)]
        scratch = [pltpu.VMEM((GLA_HEADS, GLA_DV, HEAD_PAD), F32)]
        d = 1 if reverse else 0
        args = [gq, gk, gv, gd, wd[d], bd[d], tri_b if reverse else tri_f]
        if reverse:
            in_specs += [rows(GLA_W), rows(GLA_W), const(1, GLA_DV)]
            args += [o_f, gr, norm_g]
        return pl.pallas_call(
            _gla_bwd_kernel if reverse else _gla_fwd_kernel,
            grid=(B, nb),
            in_specs=in_specs,
            out_specs=rows(GLA_W),
            out_shape=jax.ShapeDtypeStruct((T, GLA_W), BF16 if reverse else F32),
            scratch_shapes=scratch,
            compiler_params=_params("parallel", "arbitrary"),
            name="gla_bwd" if reverse else "gla_fwd",
        )(*args)

    o_f = call(False)
    return call(True)


def _merge_kernel(S, u_ref, up_ref, un_ref, om_ref, gla_ref, gl_ref, h_ref,
                  pw_ref, ps_ref, wa_ref, wb_ref, wc_ref, bg_ref, wo_ref, g_ref, b_ref,
                  rw_ref, ha_ref, afft_ref):
    tm = u_ref.shape[0]
    ext = jnp.concatenate([up_ref[...], u_ref[...], un_ref[...]], axis=0).astype(F32)
    tg = tm // MERGE_GROUPS
    for grp in range(MERGE_GROUPS):
        _merge_rows(S, pl.program_id(1) * tm + grp * tg, slice(grp * tg, (grp + 1) * tg),
                    ext[grp * tg:grp * tg + tg + 2 * POOL_HALO], om_ref, gla_ref, gl_ref, h_ref,
                    pw_ref, ps_ref, wa_ref, wb_ref, wc_ref, bg_ref, wo_ref, g_ref, b_ref,
                    rw_ref, ha_ref, afft_ref)


def _merge_rows(S, first_pos, rows, ext, om_ref, gla_ref, gl_ref, h_ref,
                pw_ref, ps_ref, wa_ref, wb_ref, wc_ref, bg_ref, wo_ref, g_ref, b_ref,
                rw_ref, ha_ref, afft_ref):
    n_ext = ext.shape[0]
    tg = n_ext - 2 * POOL_HALO
    pos_ext = first_pos - POOL_HALO + lax.broadcasted_iota(jnp.int32, (n_ext, LANES), 0)
    in_seq = (pos_ext >= 0) & (pos_ext < S)
    pos = first_pos + lax.broadcasted_iota(jnp.int32, (tg, LANES), 0)
    core = slice(POOL_HALO, POOL_HALO + tg)

    def rows_at(a, d):
        return pltpu.roll(a, (-d) % n_ext, axis=0)

    pooled = []
    for gi, w in enumerate(POOL_WINDOWS):
        hw = w // 2
        cs = slice(gi * LANES, (gi + 1) * LANES)
        x = jnp.where(in_seq, ext[:, cs], 0.0)
        win = rows_at(x, -1) + x
        reach = 1
        while reach < hw:
            win = rows_at(win, -reach) + rows_at(win, reach)
            reach *= 2
        cnt = (jnp.minimum(pos + hw, S) - jnp.maximum(pos - hw, 0)).astype(F32)
        pg = (win[core] / cnt - x[core]).astype(BF16)
        pooled.append(_dot(pg, pw_ref[gi]))
    pa = (jnp.concatenate(pooled, axis=1) * ps_ref[...]).astype(BF16)
    y_a = _dot(pa, wa_ref[...])
    y_b = _dot(om_ref[rows, :], wb_ref[...])
    y_c = _dot(gla_ref[rows, :], wc_ref[...])
    gates2 = 1.0 + jnp.tanh(0.5 * (gl_ref[rows, :].astype(F32) + bg_ref[...]))
    merged2 = (gates2[:, :D_MODEL] * y_a + gates2[:, D_MODEL:2 * D_MODEL] * y_b
               + gates2[:, 2 * D_MODEL:] * y_c)
    mix = _dot(merged2.astype(BF16), wo_ref[...])
    h1 = _layer_norm(DN_ALPHA * h_ref[rows, :] + mix, g_ref[...], b_ref[...])
    lane = lax.broadcasted_iota(jnp.int32, (tg, LANES), 1)
    logits = jnp.where(lane < N_EXPERTS, _dot3(h1, rw_ref[...]), -jnp.inf)
    e = jnp.exp(logits - jnp.max(logits, axis=1, keepdims=True))
    aff = e / jnp.sum(e, axis=1, keepdims=True)
    afft_ref[0, :, rows] = aff.T[:N_EXPERTS]
    ha_ref[rows, :D_MODEL] = DN_ALPHA * h1
    ha_ref[rows, D_MODEL:2 * D_MODEL] = h1
    ha_ref[rows, 2 * D_MODEL:] = aff


def _merge(u, o_mla, gla_o, gl, h, pw, ps, wa, wb, wc, bg, wo, g, b, rw, B, S, tm=512):
    nb = S // tm
    T = B * S
    hb = tm // POOL_HALO
    n_halo = T // POOL_HALO
    rows = lambda w: pl.BlockSpec((tm, w), lambda bb, i: (bb * nb + i, 0))
    prev = pl.BlockSpec((POOL_HALO, POOL_DIM),
                        lambda bb, i: (jnp.maximum((bb * nb + i) * hb - 1, 0), 0))
    nxt = pl.BlockSpec((POOL_HALO, POOL_DIM),
                       lambda bb, i: (jnp.minimum((bb * nb + i + 1) * hb, n_halo - 1), 0))
    c2 = lambda r, c: pl.BlockSpec((r, c), lambda bb, i: (0, 0))
    c3 = lambda a, r, c: pl.BlockSpec((a, r, c), lambda bb, i: (0, 0, 0))
    return pl.pallas_call(
        functools.partial(_merge_kernel, S),
        grid=(B, nb),
        in_specs=[rows(POOL_DIM), prev, nxt,
                  rows(MLA_HEADS * MLA_V), rows(GLA_W), rows(3 * D_MODEL), rows(D_MODEL),
                  c3(len(POOL_WINDOWS), LANES, LANES), c2(1, POOL_DIM), c2(POOL_DIM, D_MODEL),
                  c2(MLA_HEADS * MLA_V, D_MODEL), c2(GLA_W, D_MODEL), c2(1, 3 * D_MODEL),
                  c2(D_MODEL, D_MODEL), c2(1, D_MODEL), c2(1, D_MODEL), c2(D_MODEL, LANES)],
        out_specs=[rows(HA_W),
                   pl.BlockSpec((1, N_EXPERTS, tm), lambda bb, i: (bb, 0, i))],
        out_shape=[jax.ShapeDtypeStruct((T, HA_W), F32),
                   jax.ShapeDtypeStruct((B, N_EXPERTS, S), F32)],
        compiler_params=_params("parallel", "parallel"),
        name="merge",
    )(u, u, u, o_mla, gla_o, gl, h, pw, ps, wa, wb, wc, bg, wo, g, b, rw)


def _topk_kernel(C, aff_ref, tri_ref, idx_ref, p_scr):
    S = aff_ref.shape[2]
    n_chunks = S // LANES
    aff = aff_ref[0]

    def count(mask):
        return jnp.sum(jnp.where(mask, 1.0, 0.0), axis=1, keepdims=True)

    def as_float(bits):
        return lax.bitcast_convert_type(bits, F32)

    def refine(i, thr):
        cand = thr | jnp.left_shift(jnp.int32(1), 30 - i)
        return jnp.where(count(aff >= as_float(cand)) >= C, cand, thr)

    thr = lax.fori_loop(0, 31, refine, jnp.zeros((N_EXPERTS, 1), jnp.int32))
    above = aff >= as_float(thr + 1)
    tied = (aff >= as_float(thr)) & jnp.logical_not(above)
    need = C - count(above)
    tri = tri_ref[...]

    tied_f = tied.astype(F32)
    run = jnp.zeros((N_EXPERTS, 1), F32)
    sel_parts = []
    for c in range(n_chunks):
        cs = slice(c * LANES, (c + 1) * LANES)
        incl = _dot(tied_f[:, cs].astype(BF16), tri) + run
        run = incl[:, LANES - 1:LANES]
        sel_parts.append(jnp.where(above[:, cs] | (tied[:, cs] & (incl <= need)), 1.0, 0.0))

    p_scr[...] = jnp.zeros(p_scr.shape, F32)
    lane_e = lax.broadcasted_iota(jnp.int32, (N_EXPERTS, LANES), 1)
    chunk_end = jnp.full((N_EXPERTS, LANES), NEVER, F32)
    run = jnp.zeros((N_EXPERTS, 1), F32)
    for c in range(n_chunks):
        rel = _dot(sel_parts[c].astype(BF16), tri)
        p_scr[:, c, :] = rel
        run = run + rel[:, LANES - 1:LANES]
        chunk_end = jnp.where(lane_e == c, run, chunk_end)

    slot = lax.broadcasted_iota(jnp.int32, (C, LANES), 0).astype(F32)
    lane_c = lax.broadcasted_iota(jnp.int32, (C, LANES), 1).astype(F32)
    ones = jnp.ones((8, LANES), BF16)
    for e in range(N_EXPERTS):
        ce = chunk_end[e:e + 1, :]
        full = ce <= slot
        n_full = jnp.sum(jnp.where(full, 1.0, 0.0), axis=1, keepdims=True)
        base = jnp.max(jnp.where(full, ce, 0.0), axis=1, keepdims=True)
        pick = jnp.where(lane_c == n_full, 1.0, 0.0).astype(BF16)
        rel = _dot(pick, p_scr[e].astype(BF16))
        w = jnp.where(rel <= slot - base, 1.0, 0.0) + jnp.where(full, float(LANES), 0.0)
        tok = _dot_nt(ones, w.astype(BF16))[0:1, :]
        idx_ref[0, e:e + 1, :] = tok.astype(jnp.int32)


def _topk(afft, tri, C):
    B, E, S = afft.shape
    assert S // LANES <= LANES
    return pl.pallas_call(
        functools.partial(_topk_kernel, C),
        grid=(B,),
        in_specs=[pl.BlockSpec((1, E, S), lambda b: (b, 0, 0)),
                  pl.BlockSpec((LANES, LANES), lambda b: (0, 0))],
        out_specs=pl.BlockSpec((1, E, C), lambda b: (b, 0, 0)),
        out_shape=jax.ShapeDtypeStruct((B, E, C), jnp.int32),
        scratch_shapes=[pltpu.VMEM((E, LANES, LANES), F32)],
        compiler_params=_params("parallel"),
        name="expert_choice",
    )(afft, tri)


def _ffn_kernel(tc, n_steps, idx_ref, ha_in_ref, wg_ref, wu_ref, wd_ref, ha_ref,
                gbuf, sbuf, gsem, ssem):
    del ha_in_ref
    e = pl.program_id(0)
    s = (e * pl.num_programs(1) + pl.program_id(1)) * pl.num_programs(2) + pl.program_id(2)
    slot = s % 2

    def gather_copy(step, r, sl):
        row = idx_ref[step * tc + r]
        return pltpu.make_async_copy(ha_ref.at[pl.ds(row, 1), :],
                                     gbuf.at[sl, pl.ds(r, 1), :], gsem.at[sl])

    def scatter_copy(step, r, sl):
        row = idx_ref[step * tc + r]
        return pltpu.make_async_copy(sbuf.at[sl, pl.ds(r, 1), :],
                                     ha_ref.at[pl.ds(row, 1), pl.ds(0, D_MODEL)], ssem.at[sl])

    def start_rows(copy, step, sl):
        def body(r, carry):
            copy(step, r, sl).start()
            return carry
        lax.fori_loop(0, tc, body, 0, unroll=DMA_UNROLL)

    def wait_gather(sl):
        pltpu.make_async_copy(ha_ref.at[pl.ds(0, tc), :], gbuf.at[sl], gsem.at[sl]).wait()

    def wait_scatter(sl):
        pltpu.make_async_copy(sbuf.at[sl], ha_ref.at[pl.ds(0, tc), pl.ds(0, D_MODEL)],
                              ssem.at[sl]).wait()

    @pl.when(s == 0)
    def _():
        start_rows(gather_copy, s, slot)

    @pl.when(s >= 2)
    def _():
        wait_scatter(slot)

    @pl.when(s + 1 < n_steps)
    def _():
        start_rows(gather_copy, s + 1, 1 - slot)

    wait_gather(slot)

    rows = gbuf[slot]
    x = rows[:, D_MODEL:2 * D_MODEL].astype(BF16)
    lane = lax.broadcasted_iota(jnp.int32, (tc, LANES), 1)
    gate = jnp.sum(jnp.where(lane == e, rows[:, 2 * D_MODEL:], 0.0), axis=1, keepdims=True)
    y = jnp.zeros((tc, D_MODEL), F32)
    fh = D_EXPERT // 2
    for f in range(2):
        fs = slice(f * fh, (f + 1) * fh)
        hg = _dot(x, wg_ref[0, :, fs])
        hu = _dot(x, wu_ref[0, :, fs])
        hid = (hg * jax.nn.sigmoid(hg) * hu).astype(BF16)
        y += _dot(hid, wd_ref[0, fs, :])
    sbuf[slot] = rows[:, :D_MODEL] + gate * y
    start_rows(scatter_copy, s, slot)

    @pl.when(s == n_steps - 1)
    def _():
        if n_steps >= 2:
            wait_scatter(1 - slot)
        wait_scatter(slot)


def _ffn(ha, idx_steps, wg, wu, wd, B, C, tc=512):
    nblk = C // tc
    n_steps = N_EXPERTS * B * nblk
    wspec = lambda r, c: pl.BlockSpec((1, r, c), lambda e, b, j, idx: (e, 0, 0))
    return pl.pallas_call(
        functools.partial(_ffn_kernel, tc, n_steps),
        grid_spec=pltpu.PrefetchScalarGridSpec(
            num_scalar_prefetch=1,
            grid=(N_EXPERTS, B, nblk),
            in_specs=[pl.BlockSpec(memory_space=pl.ANY),
                      wspec(D_MODEL, D_EXPERT), wspec(D_MODEL, D_EXPERT),
                      wspec(D_EXPERT, D_MODEL)],
            out_specs=pl.BlockSpec(memory_space=pl.ANY),
            scratch_shapes=[pltpu.VMEM((2, tc, HA_W), F32), pltpu.VMEM((2, tc, D_MODEL), F32),
                            pltpu.SemaphoreType.DMA((2,)), pltpu.SemaphoreType.DMA((2,))]),
        out_shape=jax.ShapeDtypeStruct(ha.shape, F32),
        input_output_aliases={1: 0},
        compiler_params=_params("arbitrary", "arbitrary", "arbitrary"),
        name="expert_ffn",
    )(idx_steps, ha, wg, wu, wd)


def _ln_kernel(x_ref, g_ref, b_ref, o_ref):
    o_ref[...] = _layer_norm(x_ref[...], g_ref[...], b_ref[...])


def _ln_rows(xin, g, b, tm=512):
    T = xin.shape[0]
    const = lambda i: (0, 0)
    return pl.pallas_call(
        _ln_kernel,
        grid=(T // tm,),
        in_specs=[pl.BlockSpec((tm, D_MODEL), lambda i: (i, 0)),
                  pl.BlockSpec((1, D_MODEL), const), pl.BlockSpec((1, D_MODEL), const)],
        out_specs=pl.BlockSpec((tm, D_MODEL), lambda i: (i, 0)),
        out_shape=jax.ShapeDtypeStruct((T, D_MODEL), F32),
        compiler_params=_params("parallel"),
        name="final_ln",
    )(xin, g, b)


def _pad_heads(w, heads, width):
    lead = w.shape[:-1]
    w = w.reshape(lead + (heads, width))
    w = jnp.pad(w, [(0, 0)] * len(lead) + [(0, 0), (0, HEAD_PAD - width)])
    return w.reshape(lead + (heads * HEAD_PAD,))


def _rot_half(w):
    half = w.shape[-1] // 2
    return jnp.concatenate([-w[..., half:], w[..., :half]], axis=-1)


def _pack_w_in(w):
    o = 0
    parts = {}
    for name, width in (("pool", POOL_DIM), ("cq", MLA_Q_RANK), ("ckv", MLA_KV_RANK),
                        ("kr", MLA_ROPE), ("gq", GLA_HEADS * GLA_DK), ("gk", GLA_HEADS * GLA_DK),
                        ("gv", GLA_HEADS * GLA_DV), ("gr", GLA_HEADS * GLA_DV),
                        ("gd", 2 * GLA_GATE_RANK), ("gates", 3 * D_MODEL)):
        parts[name] = w[:, o:o + width]
        o += width
    rope_slot = lambda m: jnp.pad(m, ((0, 0), (MLA_NOPE, HEAD_PAD - MLA_NOPE - MLA_ROPE)))
    kr2 = jnp.concatenate([rope_slot(parts["kr"]), rope_slot(_rot_half(parts["kr"]))], axis=1)
    gd = jnp.pad(parts["gd"], ((0, 0), (0, LANES - 2 * GLA_GATE_RANK)))
    cols = [parts["pool"], parts["cq"], parts["ckv"], kr2,
            _pad_heads(parts["gq"], GLA_HEADS, GLA_DK), _pad_heads(parts["gk"], GLA_HEADS, GLA_DK),
            parts["gv"], parts["gr"], gd, parts["gates"]]
    return jnp.concatenate(cols, axis=1).astype(BF16)


def _pack_mla(w_uq, w_ukv):
    r = w_uq.shape[0]
    uq = w_uq.reshape(r, MLA_HEADS, MLA_NOPE + MLA_ROPE)
    nope, rope = uq[..., :MLA_NOPE], uq[..., MLA_NOPE:]
    tail = jnp.zeros((r, MLA_HEADS, HEAD_PAD - MLA_NOPE - MLA_ROPE), F32)
    wq = jnp.concatenate([nope, rope, tail], axis=-1).reshape(r, -1)
    wq2 = jnp.concatenate([jnp.zeros_like(nope), _rot_half(rope), tail], axis=-1).reshape(r, -1)
    rk = w_ukv.shape[0]
    ukv = w_ukv.reshape(rk, MLA_HEADS, MLA_NOPE + MLA_V)
    wk = _pad_heads(ukv[..., :MLA_NOPE].reshape(rk, -1), MLA_HEADS, MLA_NOPE)
    wv = _pad_heads(ukv[..., MLA_NOPE:].reshape(rk, -1), MLA_HEADS, MLA_V)
    return wq.astype(BF16), wq2.astype(BF16), wk.astype(BF16), wv.astype(BF16)


def _pack_gla_decay(w_dec, b_dec):
    wd = []
    for d in range(2):
        rows = _pad_heads(w_dec[d], GLA_HEADS, GLA_DK)
        wd.append(jnp.pad(rows, ((d * GLA_GATE_RANK, LANES - (d + 1) * GLA_GATE_RANK),
                                 (0, 0))).astype(BF16))
    bd = [_pad_heads(b_dec[d][None, :], GLA_HEADS, GLA_DK) for d in range(2)]
    return wd, bd


def _chunk_tri(tm, reverse):
    r = jnp.arange(tm)[:, None]
    c = jnp.arange(tm)[None, :]
    same = (r // GLA_CHUNK) == (c // GLA_CHUNK)
    return (same & ((c >= r) if reverse else (c <= r))).astype(BF16)


def kernel(x, positions, ln0_g, ln0_b, w_in, b_gate, pool_w, pool_scale, w_up_a, mla_q_norm,
           mla_w_uq, mla_kv_norm, mla_w_ukv, w_up_b, gla_w_dec, gla_b_dec, gla_norm, w_up_c,
           w_out, ln1_g, ln1_b, router_w, exp_w_gate, exp_w_up, exp_w_down, ln2_g, ln2_b):
    B, S, D = x.shape
    assert D == D_MODEL and S % 512 == 0 and B == 2
    T = B * S
    C = CAPACITY_FACTOR * S // N_EXPERTS
    tc = min(512, C)
    row = lambda v: v.reshape(1, -1).astype(F32)

    half = MLA_ROPE // 2
    freqs = ROPE_THETA ** (-jnp.arange(half, dtype=F32) / half)
    lanes = jnp.arange(LANES)
    in_rope = (lanes >= MLA_NOPE) & (lanes < MLA_NOPE + MLA_ROPE)
    freq_row = jnp.where(in_rope, freqs[(lanes - MLA_NOPE) % half], 0.0).reshape(1, LANES)
    rope_mask = in_rope.astype(F32).reshape(1, LANES)
    nope_row = (lanes < MLA_NOPE).astype(F32).reshape(1, LANES)
    one_row = (lanes == MLA_V).astype(F32).reshape(1, LANES)
    cos_t, sin_t = _rope_tables(positions.reshape(T, 1), freq_row, rope_mask)

    tri_f, tri_b = _chunk_tri(GLA_GROUP, False), _chunk_tri(GLA_GROUP, True)
    tri_lane = (jnp.arange(LANES)[:, None] <= jnp.arange(LANES)[None, :]).astype(BF16)

    stream, g_in, b_in = x.reshape(T, D), ln0_g, ln0_b
    for l in range(DEPTH):
        h, u, cq, ckv, kr2, gq, gk, gv, gr, gd, gl = _ln_inproj(
            stream, row(g_in), row(b_in), _pack_w_in(w_in[l]))
        wq, wq2, wk, wv = _pack_mla(mla_w_uq[l], mla_w_ukv[l])
        q, k, v = _mla_prep(cq, ckv, kr2, cos_t, sin_t, row(mla_q_norm[l]), row(mla_kv_norm[l]),
                            wq, wq2, wk, wv, nope_row, one_row, B, S)
        o_mla, (wg_b, wu_b, wd_b) = _flash(q, k, v, (exp_w_gate, exp_w_up, exp_w_down), l)
        o_mla = o_mla.reshape(T, MLA_HEADS * MLA_V)
        wd, bd = _pack_gla_decay(gla_w_dec[l], gla_b_dec[l])
        gla_o = _gla(gq, gk, gv, gd, gr, wd, bd, tri_f, tri_b, row(gla_norm[l]), B, S)
        rw = jnp.pad(router_w[l], ((0, 0), (0, LANES - N_EXPERTS)))
        ha, afft = _merge(u, o_mla, gla_o, gl, h, pool_w[l].astype(BF16), row(pool_scale[l]),
                          w_up_a[l].astype(BF16), w_up_b[l].astype(BF16),
                          w_up_c[l].astype(BF16), row(b_gate[l]),
                          (0.5 * w_out[l]).astype(BF16), row(ln1_g[l]), row(ln1_b[l]), rw, B, S)
        idx = _topk(afft, tri_lane, C)
        idx_steps = (idx + (jnp.arange(B, dtype=jnp.int32) * S)[:, None, None])
        idx_steps = idx_steps.transpose(1, 0, 2).reshape(-1)
        stream = _ffn(ha, idx_steps, wg_b, wu_b, wd_b, B, C, tc)
        g_in, b_in = ln2_g[l], ln2_b[l]
    out = _ln_rows(stream, row(g_in), row(b_in))
    return out.reshape(B, S, D)
```

```python
import functools

import jax
import jax.numpy as jnp
from jax import lax
from jax.experimental import pallas as pl
from jax.experimental.pallas import tpu as pltpu

F32 = jnp.float32
BF16 = jnp.bfloat16

LANES = 128
VMEM_LIMIT = 56 * 1024 * 1024
FLASH_VMEM_LIMIT = 60 * 1024 * 1024

D_MODEL = 1024
DEPTH = 2
POOL_WINDOWS = (2, 4, 8, 16)
POOL_DIM = 512
POOL_HALO = 16
MLA_HEADS = 8
MLA_NOPE = 64
MLA_ROPE = 32
MLA_V = 64
MLA_Q_RANK = 384
MLA_KV_RANK = 256
ROPE_THETA = 10000.0
GLA_HEADS = 4
GLA_DK = 64
GLA_DV = 128
GLA_GATE_RANK = 16
GLA_GATE_NORM = 16.0
GLA_CHUNK = 64
N_EXPERTS = 16
CAPACITY_FACTOR = 2
D_EXPERT = 2048
DN_ALPHA = (2 * DEPTH) ** 0.25
LN_EPS = 1e-5
RMS_EPS = 1e-6
LOG2_E = 1.4426950408889634
NEVER = 1e9

HEAD_PAD = LANES
GLA_W = GLA_HEADS * HEAD_PAD
HA_W = 2 * D_MODEL + LANES
DMA_UNROLL = 16
MERGE_GROUPS = 2
GLA_GROUP = 256

_SEG = (("pool", POOL_DIM), ("cq", MLA_Q_RANK), ("ckv", MLA_KV_RANK), ("kr2", 2 * HEAD_PAD),
        ("gq", GLA_W), ("gk", GLA_W), ("gv", GLA_W), ("gr", GLA_W), ("gd", LANES),
        ("gates", 3 * D_MODEL))
_SEG_OFF = {}
_off = 0
for _n, _w in _SEG:
    _SEG_OFF[_n] = (_off, _w)
    _off += _w
W_ALL_COLS = _off


def _params(*sem):
    return pltpu.CompilerParams(dimension_semantics=sem, vmem_limit_bytes=VMEM_LIMIT)


def _dot(a, b):
    return jnp.dot(a, b, preferred_element_type=F32)


def _dot_nt(a, b):
    return lax.dot_general(a, b, (((1,), (1,)), ((), ())), preferred_element_type=F32)


def _split(x):
    hi = x.astype(BF16)
    lo = (x - hi.astype(F32)).astype(BF16)
    return hi, lo


def _dot3(a, b, nt=False):
    d = _dot_nt if nt else _dot
    ah, al = _split(a)
    bh, bl = _split(b)
    return d(ah, bh) + d(ah, bl) + d(al, bh)


def _layer_norm(x, g, b):
    mu = jnp.mean(x, axis=-1, keepdims=True)
    xc = x - mu
    var = jnp.mean(xc * xc, axis=-1, keepdims=True)
    return xc * lax.rsqrt(var + LN_EPS) * g + b


def _rms_norm(x, g):
    ms = jnp.mean(x * x, axis=-1, keepdims=True)
    return x * lax.rsqrt(ms + RMS_EPS) * g


def _rope_kernel(pos_ref, freq_ref, mask_ref, cos_ref, sin_ref):
    ang = pos_ref[...].astype(F32) * freq_ref[...]
    cos_ref[...] = jnp.cos(ang) * mask_ref[...]
    sin_ref[...] = jnp.sin(ang) * mask_ref[...]


def _rope_tables(pos_col, freq_row, mask_row, tm=512):
    T = pos_col.shape[0]
    row = pl.BlockSpec((1, LANES), lambda i: (0, 0))
    out = pl.BlockSpec((tm, LANES), lambda i: (i, 0))
    return pl.pallas_call(
        _rope_kernel,
        grid=(T // tm,),
        in_specs=[pl.BlockSpec((tm, 1), lambda i: (i, 0)), row, row],
        out_specs=[out, out],
        out_shape=[jax.ShapeDtypeStruct((T, LANES), F32)] * 2,
        compiler_params=_params("parallel"),
        name="rope_tables",
    )(pos_col, freq_row, mask_row)


def _inproj_kernel(x_ref, xn_ref, g_ref, b_ref, w_ref, h_ref, *rest):
    out_refs, h_scr = rest[:-1], rest[-1]

    @pl.when(pl.program_id(0) == 0)
    def _():
        h_scr[...] = _layer_norm(x_ref[...], g_ref[...], b_ref[...])

    h = h_scr[...]
    h_ref[...] = h
    hb = h.astype(BF16)
    for (name, width), o_ref in zip(_SEG, out_refs):
        off = _SEG_OFF[name][0]
        o_ref[...] = _dot(hb, w_ref[:, off:off + width]).astype(o_ref.dtype)
    h_scr[...] = _layer_norm(xn_ref[...], g_ref[...], b_ref[...])


def _ln_inproj(xin, g, b, w_all, tm=512):
    T = xin.shape[0]
    n_steps = T // tm
    const = lambda i: (0, 0)
    rows = lambda w: pl.BlockSpec((tm, w), lambda i: (i, 0))
    out_shapes = [jax.ShapeDtypeStruct((T, D_MODEL), F32)]
    out_specs = [rows(D_MODEL)]
    for name, width in _SEG:
        out_shapes.append(jax.ShapeDtypeStruct((T, width), F32 if name == "gd" else BF16))
        out_specs.append(rows(width))
    return pl.pallas_call(
        _inproj_kernel,
        grid=(n_steps,),
        in_specs=[rows(D_MODEL),
                  pl.BlockSpec((tm, D_MODEL), lambda i: (jnp.minimum(i + 1, n_steps - 1), 0)),
                  pl.BlockSpec((1, D_MODEL), const), pl.BlockSpec((1, D_MODEL), const),
                  pl.BlockSpec((D_MODEL, W_ALL_COLS), const)],
        out_specs=out_specs,
        out_shape=out_shapes,
        scratch_shapes=[pltpu.VMEM((tm, D_MODEL), F32)],
        compiler_params=_params("arbitrary"),
        name="ln_inproj",
    )(xin, xin, g, b, w_all)


def _mla_prep_kernel(cq_ref, ckv_ref, kr2_ref, cos_ref, sin_ref, qn_ref, kvn_ref,
                     wq_ref, wq2_ref, wk_ref, wv_ref, nope_ref, one_ref,
                     q_ref, k_ref, v_ref):
    scale = (MLA_NOPE + MLA_ROPE) ** -0.5 * LOG2_E
    cosr = cos_ref[...]
    sinr = sin_ref[...]
    nq = _rms_norm(cq_ref[...].astype(F32), qn_ref[...]).astype(BF16)
    nkv = _rms_norm(ckv_ref[...].astype(F32), kvn_ref[...]).astype(BF16)
    qa = _dot(nq, wq_ref[...])
    qb = _dot(nq, wq2_ref[...])
    ka = _dot(nkv, wk_ref[...])
    va = _dot(nkv, wv_ref[...])
    kr2 = kr2_ref[...].astype(F32)
    k_rope = kr2[:, :HEAD_PAD] * cosr + kr2[:, HEAD_PAD:] * sinr
    q_cos = (cosr + nope_ref[...]) * scale
    q_sin = sinr * scale
    for h in range(MLA_HEADS):
        sl = slice(h * HEAD_PAD, (h + 1) * HEAD_PAD)
        q_ref[0, h] = (qa[:, sl] * q_cos + qb[:, sl] * q_sin).astype(BF16)
        k_ref[0, h] = (ka[:, sl] + k_rope).astype(BF16)
        v_ref[0, h] = (va[:, sl] + one_ref[...]).astype(BF16)


def _mla_prep(cq, ckv, kr2, cos_t, sin_t, qn, kvn, wq, wq2, wk, wv, nope_row, one_row, B, S,
              tm=512):
    nb = S // tm
    rows = lambda w: pl.BlockSpec((tm, w), lambda b, i: (b * nb + i, 0))
    const = lambda r, c: pl.BlockSpec((r, c), lambda b, i: (0, 0))
    hw = MLA_HEADS * HEAD_PAD
    out = pl.BlockSpec((1, MLA_HEADS, tm, HEAD_PAD), lambda b, i: (b, 0, i, 0))
    return pl.pallas_call(
        _mla_prep_kernel,
        grid=(B, nb),
        in_specs=[rows(MLA_Q_RANK), rows(MLA_KV_RANK), rows(2 * HEAD_PAD), rows(LANES),
                  rows(LANES), const(1, MLA_Q_RANK), const(1, MLA_KV_RANK),
                  const(MLA_Q_RANK, hw), const(MLA_Q_RANK, hw), const(MLA_KV_RANK, hw),
                  const(MLA_KV_RANK, hw), const(1, LANES), const(1, LANES)],
        out_specs=[out, out, out],
        out_shape=[jax.ShapeDtypeStruct((B, MLA_HEADS, S, HEAD_PAD), BF16)] * 3,
        compiler_params=_params("parallel", "parallel"),
        name="mla_prep",
    )(cq, ckv, kr2, cos_t, sin_t, qn, kvn, wq, wq2, wk, wv, nope_row, one_row)


def _flash_kernel(q_ref, k_ref, v_ref, wg_ref, wu_ref, wd_ref, o_ref, wgb_ref, wub_ref, wdb_ref,
                  m_scr, acc_scr):
    kv = pl.program_id(2)
    wgb_ref[...] = wg_ref[0].astype(BF16)
    wub_ref[...] = wu_ref[0].astype(BF16)
    wdb_ref[...] = wd_ref[0].astype(BF16)

    @pl.when(kv == 0)
    def _():
        m_scr[...] = jnp.full(m_scr.shape, -jnp.inf, F32)
        acc_scr[...] = jnp.zeros(acc_scr.shape, F32)

    tk = k_ref.shape[2]
    for h in range(MLA_HEADS):
        s = _dot_nt(q_ref[0, h], k_ref[0, h])
        m_prev = m_scr[h]
        m_new = jnp.maximum(m_prev, jnp.max(s, axis=1, keepdims=True))
        p = jnp.exp2(s - jnp.concatenate([m_new] * (tk // LANES), axis=1))
        alpha = jnp.exp2(m_prev - m_new)
        acc_scr[h] = alpha * acc_scr[h] + _dot(p.astype(BF16), v_ref[0, h])
        m_scr[h] = m_new

    @pl.when(kv == pl.num_programs(2) - 1)
    def _():
        lane = lax.broadcasted_iota(jnp.int32, (q_ref.shape[2], HEAD_PAD), 1)
        for hp in range(MLA_HEADS // 2):
            a0 = acc_scr[2 * hp]
            a1 = acc_scr[2 * hp + 1]
            o0 = a0 / a0[:, MLA_V:MLA_V + 1]
            o1 = a1 / a1[:, MLA_V:MLA_V + 1]
            pair = jnp.where(lane < MLA_V, o0, pltpu.roll(o1, MLA_V, axis=1))
            o_ref[0, :, hp * HEAD_PAD:(hp + 1) * HEAD_PAD] = pair.astype(BF16)


def _flash(q, k, v, expert_w, layer, tq=1024, tk=2048):
    B, H, S, _ = q.shape
    tq, tk = min(tq, S), min(tk, S)
    assert S % tq == 0 and S % tk == 0
    nq, nk = S // tq, S // tk
    n_steps = B * nq * nk
    qspec = pl.BlockSpec((1, H, tq, HEAD_PAD), lambda b, i, j: (b, 0, i, 0))
    kspec = pl.BlockSpec((1, H, tk, HEAD_PAD), lambda b, i, j: (b, 0, j, 0))
    w_in, w_specs, wb_specs, wb_shapes = [], [], [], []
    for w in expert_w:
        n_l, n_e, r, c = w.shape
        slab = n_e * r // n_steps
        assert n_e * r % n_steps == 0 and slab % 16 == 0
        w_in.append(w.reshape(n_l, n_steps, slab, c))
        w_specs.append(pl.BlockSpec((1, 1, slab, c),
                                    lambda b, i, j: (layer, (b * nq + i) * nk + j, 0, 0)))
        wb_specs.append(pl.BlockSpec((1, slab, c), lambda b, i, j: ((b * nq + i) * nk + j, 0, 0)))
        wb_shapes.append(jax.ShapeDtypeStruct((n_steps, slab, c), BF16))
    o, *wb = pl.pallas_call(
        _flash_kernel,
        grid=(B, nq, nk),
        in_specs=[qspec, kspec, kspec] + w_specs,
        out_specs=[pl.BlockSpec((1, tq, H * MLA_V), lambda b, i, j: (b, i, 0))] + wb_specs,
        out_shape=[jax.ShapeDtypeStruct((B, S, H * MLA_V), BF16)] + wb_shapes,
        scratch_shapes=[pltpu.VMEM((H, tq, LANES), F32), pltpu.VMEM((H, tq, HEAD_PAD), F32)],
        compiler_params=pltpu.CompilerParams(
            dimension_semantics=("parallel", "parallel", "arbitrary"),
            vmem_limit_bytes=FLASH_VMEM_LIMIT),
        name="mla_flash",
    )(q, k, v, *w_in)
    return o, [b16.reshape(w.shape[1:]) for b16, w in zip(wb, expert_w)]


def _log_sigmoid(x):
    return jnp.minimum(x, 0.0) - jnp.log(1.0 + jnp.exp(-jnp.abs(x)))


def _gla_scan(reverse, gq_ref, gk_ref, gv_ref, gd_ref, wd_ref, bd_ref, tri_ref, state_scr,
              emit):
    tm = gq_ref.shape[0]
    tg = tri_ref.shape[0]
    n_groups = tm // tg

    @pl.when(pl.program_id(1) == 0)
    def _():
        state_scr[...] = jnp.zeros(state_scr.shape, F32)

    states = [state_scr[h] for h in range(GLA_HEADS)]
    for grp in (range(n_groups - 1, -1, -1) if reverse else range(n_groups)):
        states = _gla_group(reverse, slice(grp * tg, (grp + 1) * tg), gq_ref, gk_ref, gv_ref,
                            gd_ref, wd_ref, bd_ref, tri_ref, states, emit)
    for h in range(GLA_HEADS):
        state_scr[h] = states[h]


def _gla_group(reverse, rows, gq_ref, gk_ref, gv_ref, gd_ref, wd_ref, bd_ref, tri_ref, states,
               emit):
    tg = tri_ref.shape[0]
    L = GLA_CHUNK
    n_chunks = tg // L
    logits = _dot(gd_ref[rows, :].astype(BF16), wd_ref[...]) + bd_ref[...]
    g = _log_sigmoid(logits) * (1.0 / GLA_GATE_NORM)
    g_hi, g_lo = _split(g)
    b = _dot(tri_ref[...], g_hi) + _dot(tri_ref[...], g_lo)
    edge = 0 if reverse else L - 1
    b_last = jnp.concatenate(
        [jnp.broadcast_to(b[c * L + edge:c * L + edge + 1, :], (L, GLA_W))
         for c in range(n_chunks)], axis=0)
    q_in = (gq_ref[rows, :].astype(F32) * (GLA_DK ** -0.5) * jnp.exp(b)).astype(BF16)
    k_f = gk_ref[rows, :].astype(F32)
    k_in = (k_f * jnp.exp(-b)).astype(BF16)
    k_st = (k_f * jnp.exp(b_last - b)).astype(BF16)
    dec = jnp.exp(b_last)
    v = gv_ref[rows, :]
    lane = lax.broadcasted_iota(jnp.int32, (HEAD_PAD, 2 * L), 1)
    order = range(n_chunks - 1, -1, -1) if reverse else range(n_chunks)
    heads = [slice(h * HEAD_PAD, (h + 1) * HEAD_PAD) for h in range(GLA_HEADS)]
    kv = {}
    o_intra = []
    for h, hs in enumerate(heads):
        a = _dot_nt(q_in[:, hs], k_in[:, hs]).astype(BF16) * tri_ref[...]
        o_intra.append(_dot(a, v[:, hs]))
        v_t = v[:, hs].astype(F32).T
        for c in range(n_chunks):
            pair = slice((c // 2) * 2 * L, (c // 2 + 1) * 2 * L)
            in_chunk = (lane >= L) if c % 2 else (lane < L)
            v_tc = jnp.where(in_chunk, v_t[:, pair], 0.0).astype(BF16)
            kv[c, h] = _dot(v_tc, k_st[pair, hs])
    entering = {}
    leaving = []
    for h, hs in enumerate(heads):
        st = states[h]
        for c in order:
            entering[c, h] = st.astype(BF16)
            st = dec[c * L:c * L + 1, hs] * st + kv[c, h]
        leaving.append(st)
    for h, hs in enumerate(heads):
        o_inter = jnp.concatenate(
            [_dot_nt(q_in[c * L:(c + 1) * L, hs], entering[c, h]) for c in range(n_chunks)],
            axis=0)
        emit(rows, hs, o_intra[h] + o_inter)
    return leaving


def _gla_fwd_kernel(gq_ref, gk_ref, gv_ref, gd_ref, wd_ref, bd_ref, tri_ref, o_ref, state_scr):
    def emit(rows, hs, tile):
        o_ref[rows, hs] = tile

    _gla_scan(False, gq_ref, gk_ref, gv_ref, gd_ref, wd_ref, bd_ref, tri_ref, state_scr, emit)


def _gla_bwd_kernel(gq_ref, gk_ref, gv_ref, gd_ref, wd_ref, bd_ref, tri_ref, of_ref, gr_ref,
                    ng_ref, out_ref, state_scr):
    def emit(rows, hs, tile):
        r = gr_ref[rows, hs].astype(F32)
        o = _rms_norm(of_ref[rows, hs] + tile, ng_ref[...])
        out_ref[rows, hs] = (o * (r * jax.nn.sigmoid(r))).astype(BF16)

    _gla_scan(True, gq_ref, gk_ref, gv_ref, gd_ref, wd_ref, bd_ref, tri_ref, state_scr, emit)


def _gla(gq, gk, gv, gd, gr, wd, bd, tri_f, tri_b, norm_g, B, S, tm=512):
    nb = S // tm
    T = B * S

    def call(reverse):
        blk = (lambda b, i: (b * nb + nb - 1 - i, 0)) if reverse else (lambda b, i: (b * nb + i, 0))
        rows = lambda w: pl.BlockSpec((tm, w), blk)
        const = lambda r, c: pl.BlockSpec((r, c), lambda b, i: (0, 0))
        in_specs = [rows(GLA_W), rows(GLA_W), rows(GLA_W), rows(LANES),
                    const(LANES, GLA_W), const(1, GLA_W), const(GLA_GROUP, GLA_GROUP)]
        scratch = [pltpu.VMEM((GLA_HEADS, GLA_DV, HEAD_PAD), F32)]
        d = 1 if reverse else 0
        args = [gq, gk, gv, gd, wd[d], bd[d], tri_b if reverse else tri_f]
        if reverse:
            in_specs += [rows(GLA_W), rows(GLA_W), const(1, GLA_DV)]
            args += [o_f, gr, norm_g]
        return pl.pallas_call(
            _gla_bwd_kernel if reverse else _gla_fwd_kernel,
            grid=(B, nb),
            in_specs=in_specs,
            out_specs=rows(GLA_W),
            out_shape=jax.ShapeDtypeStruct((T, GLA_W), BF16 if reverse else F32),
            scratch_shapes=scratch,
            compiler_params=_params("parallel", "arbitrary"),
            name="gla_bwd" if reverse else "gla_fwd",
        )(*args)

    o_f = call(False)
    return call(True)


def _merge_kernel(S, u_ref, up_ref, un_ref, om_ref, gla_ref, gl_ref, h_ref,
                  pw_ref, ps_ref, wa_ref, wb_ref, wc_ref, bg_ref, wo_ref, g_ref, b_ref,
                  rw_ref, ha_ref, afft_ref):
    tm = u_ref.shape[0]
    ext = jnp.concatenate([up_ref[...], u_ref[...], un_ref[...]], axis=0).astype(F32)
    tg = tm // MERGE_GROUPS
    for grp in range(MERGE_GROUPS):
        _merge_rows(S, pl.program_id(1) * tm + grp * tg, slice(grp * tg, (grp + 1) * tg),
                    ext[grp * tg:grp * tg + tg + 2 * POOL_HALO], om_ref, gla_ref, gl_ref, h_ref,
                    pw_ref, ps_ref, wa_ref, wb_ref, wc_ref, bg_ref, wo_ref, g_ref, b_ref,
                    rw_ref, ha_ref, afft_ref)


def _merge_rows(S, first_pos, rows, ext, om_ref, gla_ref, gl_ref, h_ref,
                pw_ref, ps_ref, wa_ref, wb_ref, wc_ref, bg_ref, wo_ref, g_ref, b_ref,
                rw_ref, ha_ref, afft_ref):
    n_ext = ext.shape[0]
    tg = n_ext - 2 * POOL_HALO
    pos_ext = first_pos - POOL_HALO + lax.broadcasted_iota(jnp.int32, (n_ext, LANES), 0)
    in_seq = (pos_ext >= 0) & (pos_ext < S)
    pos = first_pos + lax.broadcasted_iota(jnp.int32, (tg, LANES), 0)
    core = slice(POOL_HALO, POOL_HALO + tg)

    def rows_at(a, d):
        return pltpu.roll(a, (-d) % n_ext, axis=0)

    pooled = []
    for gi, w in enumerate(POOL_WINDOWS):
        hw = w // 2
        cs = slice(gi * LANES, (gi + 1) * LANES)
        x = jnp.where(in_seq, ext[:, cs], 0.0)
        win = rows_at(x, -1) + x
        reach = 1
        while reach < hw:
            win = rows_at(win, -reach) + rows_at(win, reach)
            reach *= 2
        cnt = (jnp.minimum(pos + hw, S) - jnp.maximum(pos - hw, 0)).astype(F32)
        pg = (win[core] / cnt - x[core]).astype(BF16)
        pooled.append(_dot(pg, pw_ref[gi]))
    pa = (jnp.concatenate(pooled, axis=1) * ps_ref[...]).astype(BF16)
    y_a = _dot(pa, wa_ref[...])
    y_b = _dot(om_ref[rows, :], wb_ref[...])
    y_c = _dot(gla_ref[rows, :], wc_ref[...])
    gates2 = 1.0 + jnp.tanh(0.5 * (gl_ref[rows, :].astype(F32) + bg_ref[...]))
    merged2 = (gates2[:, :D_MODEL] * y_a + gates2[:, D_MODEL:2 * D_MODEL] * y_b
               + gates2[:, 2 * D_MODEL:] * y_c)
    mix = _dot(merged2.astype(BF16), wo_ref[...])
    h1 = _layer_norm(DN_ALPHA * h_ref[rows, :] + mix, g_ref[...], b_ref[...])
    lane = lax.broadcasted_iota(jnp.int32, (tg, LANES), 1)
    logits = jnp.where(lane < N_EXPERTS, _dot3(h1, rw_ref[...]), -jnp.inf)
    e = jnp.exp(logits - jnp.max(logits, axis=1, keepdims=True))
    aff = e / jnp.sum(e, axis=1, keepdims=True)
    afft_ref[0, :, rows] = aff.T[:N_EXPERTS]
    ha_ref[rows, :D_MODEL] = DN_ALPHA * h1
    ha_ref[rows, D_MODEL:2 * D_MODEL] = h1
    ha_ref[rows, 2 * D_MODEL:] = aff


def _merge(u, o_mla, gla_o, gl, h, pw, ps, wa, wb, wc, bg, wo, g, b, rw, B, S, tm=512):
    nb = S // tm
    T = B * S
    hb = tm // POOL_HALO
    n_halo = T // POOL_HALO
    rows = lambda w: pl.BlockSpec((tm, w), lambda bb, i: (bb * nb + i, 0))
    prev = pl.BlockSpec((POOL_HALO, POOL_DIM),
                        lambda bb, i: (jnp.maximum((bb * nb + i) * hb - 1, 0), 0))
    nxt = pl.BlockSpec((POOL_HALO, POOL_DIM),
                       lambda bb, i: (jnp.minimum((bb * nb + i + 1) * hb, n_halo - 1), 0))
    c2 = lambda r, c: pl.BlockSpec((r, c), lambda bb, i: (0, 0))
    c3 = lambda a, r, c: pl.BlockSpec((a, r, c), lambda bb, i: (0, 0, 0))
    return pl.pallas_call(
        functools.partial(_merge_kernel, S),
        grid=(B, nb),
        in_specs=[rows(POOL_DIM), prev, nxt,
                  rows(MLA_HEADS * MLA_V), rows(GLA_W), rows(3 * D_MODEL), rows(D_MODEL),
                  c3(len(POOL_WINDOWS), LANES, LANES), c2(1, POOL_DIM), c2(POOL_DIM, D_MODEL),
                  c2(MLA_HEADS * MLA_V, D_MODEL), c2(GLA_W, D_MODEL), c2(1, 3 * D_MODEL),
                  c2(D_MODEL, D_MODEL), c2(1, D_MODEL), c2(1, D_MODEL), c2(D_MODEL, LANES)],
        out_specs=[rows(HA_W),
                   pl.BlockSpec((1, N_EXPERTS, tm), lambda bb, i: (bb, 0, i))],
        out_shape=[jax.ShapeDtypeStruct((T, HA_W), F32),
                   jax.ShapeDtypeStruct((B, N_EXPERTS, S), F32)],
        compiler_params=_params("parallel", "parallel"),
        name="merge",
    )(u, u, u, o_mla, gla_o, gl, h, pw, ps, wa, wb, wc, bg, wo, g, b, rw)


def _topk_kernel(C, aff_ref, tri_ref, idx_ref, p_scr):
    S = aff_ref.shape[2]
    n_chunks = S // LANES
    aff = aff_ref[0]

    def count(mask):
        return jnp.sum(jnp.where(mask, 1.0, 0.0), axis=1, keepdims=True)

    def as_float(bits):
        return lax.bitcast_convert_type(bits, F32)

    def refine(i, thr):
        cand = thr | jnp.left_shift(jnp.int32(1), 30 - i)
        return jnp.where(count(aff >= as_float(cand)) >= C, cand, thr)

    thr = lax.fori_loop(0, 31, refine, jnp.zeros((N_EXPERTS, 1), jnp.int32))
    above = aff >= as_float(thr + 1)
    tied = (aff >= as_float(thr)) & jnp.logical_not(above)
    need = C - count(above)
    tri = tri_ref[...]

    tied_f = tied.astype(F32)
    run = jnp.zeros((N_EXPERTS, 1), F32)
    sel_parts = []
    for c in range(n_chunks):
        cs = slice(c * LANES, (c + 1) * LANES)
        incl = _dot(tied_f[:, cs].astype(BF16), tri) + run
        run = incl[:, LANES - 1:LANES]
        sel_parts.append(jnp.where(above[:, cs] | (tied[:, cs] & (incl <= need)), 1.0, 0.0))

    p_scr[...] = jnp.zeros(p_scr.shape, F32)
    lane_e = lax.broadcasted_iota(jnp.int32, (N_EXPERTS, LANES), 1)
    chunk_end = jnp.full((N_EXPERTS, LANES), NEVER, F32)
    run = jnp.zeros((N_EXPERTS, 1), F32)
    for c in range(n_chunks):
        rel = _dot(sel_parts[c].astype(BF16), tri)
        p_scr[:, c, :] = rel
        run = run + rel[:, LANES - 1:LANES]
        chunk_end = jnp.where(lane_e == c, run, chunk_end)

    slot = lax.broadcasted_iota(jnp.int32, (C, LANES), 0).astype(F32)
    lane_c = lax.broadcasted_iota(jnp.int32, (C, LANES), 1).astype(F32)
    ones = jnp.ones((8, LANES), BF16)
    for e in range(N_EXPERTS):
        ce = chunk_end[e:e + 1, :]
        full = ce <= slot
        n_full = jnp.sum(jnp.where(full, 1.0, 0.0), axis=1, keepdims=True)
        base = jnp.max(jnp.where(full, ce, 0.0), axis=1, keepdims=True)
        pick = jnp.where(lane_c == n_full, 1.0, 0.0).astype(BF16)
        rel = _dot(pick, p_scr[e].astype(BF16))
        w = jnp.where(rel <= slot - base, 1.0, 0.0) + jnp.where(full, float(LANES), 0.0)
        tok = _dot_nt(ones, w.astype(BF16))[0:1, :]
        idx_ref[0, e:e + 1, :] = tok.astype(jnp.int32)


def _topk(afft, tri, C):
    B, E, S = afft.shape
    assert S // LANES <= LANES
    return pl.pallas_call(
        functools.partial(_topk_kernel, C),
        grid=(B,),
        in_specs=[pl.BlockSpec((1, E, S), lambda b: (b, 0, 0)),
                  pl.BlockSpec((LANES, LANES), lambda b: (0, 0))],
        out_specs=pl.BlockSpec((1, E, C), lambda b: (b, 0, 0)),
        out_shape=jax.ShapeDtypeStruct((B, E, C), jnp.int32),
        scratch_shapes=[pltpu.VMEM((E, LANES, LANES), F32)],
        compiler_params=_params("parallel"),
        name="expert_choice",
    )(afft, tri)


def _ffn_kernel(tc, n_steps, idx_ref, ha_in_ref, wg_ref, wu_ref, wd_ref, ha_ref,
                gbuf, sbuf, gsem, ssem):
    del ha_in_ref
    e = pl.program_id(0)
    s = (e * pl.num_programs(1) + pl.program_id(1)) * pl.num_programs(2) + pl.program_id(2)
    slot = s % 2

    def gather_copy(step, r, sl):
        row = idx_ref[step * tc + r]
        return pltpu.make_async_copy(ha_ref.at[pl.ds(row, 1), :],
                                     gbuf.at[sl, pl.ds(r, 1), :], gsem.at[sl])

    def scatter_copy(step, r, sl):
        row = idx_ref[step * tc + r]
        return pltpu.make_async_copy(sbuf.at[sl, pl.ds(r, 1), :],
                                     ha_ref.at[pl.ds(row, 1), pl.ds(0, D_MODEL)], ssem.at[sl])

    def start_rows(copy, step, sl):
        def body(r, carry):
            copy(step, r, sl).start()
            return carry
        lax.fori_loop(0, tc, body, 0, unroll=DMA_UNROLL)

    def wait_gather(sl):
        pltpu.make_async_copy(ha_ref.at[pl.ds(0, tc), :], gbuf.at[sl], gsem.at[sl]).wait()

    def wait_scatter(sl):
        pltpu.make_async_copy(sbuf.at[sl], ha_ref.at[pl.ds(0, tc), pl.ds(0, D_MODEL)],
                              ssem.at[sl]).wait()

    @pl.when(s == 0)
    def _():
        start_rows(gather_copy, s, slot)

    @pl.when(s >= 2)
    def _():
        wait_scatter(slot)

    @pl.when(s + 1 < n_steps)
    def _():
        start_rows(gather_copy, s + 1, 1 - slot)

    wait_gather(slot)

    rows = gbuf[slot]
    x = rows[:, D_MODEL:2 * D_MODEL].astype(BF16)
    lane = lax.broadcasted_iota(jnp.int32, (tc, LANES), 1)
    gate = jnp.sum(jnp.where(lane == e, rows[:, 2 * D_MODEL:], 0.0), axis=1, keepdims=True)
    y = jnp.zeros((tc, D_MODEL), F32)
    fh = D_EXPERT // 2
    for f in range(2):
        fs = slice(f * fh, (f + 1) * fh)
        hg = _dot(x, wg_ref[0, :, fs])
        hu = _dot(x, wu_ref[0, :, fs])
        hid = (hg * jax.nn.sigmoid(hg) * hu).astype(BF16)
        y += _dot(hid, wd_ref[0, fs, :])
    sbuf[slot] = rows[:, :D_MODEL] + gate * y
    start_rows(scatter_copy, s, slot)

    @pl.when(s == n_steps - 1)
    def _():
        if n_steps >= 2:
            wait_scatter(1 - slot)
        wait_scatter(slot)


def _ffn(ha, idx_steps, wg, wu, wd, B, C, tc=512):
    nblk = C // tc
    assert B == 2 and nblk >= 2 and C % tc == 0
    n_steps = N_EXPERTS * B * nblk
    wspec = lambda r, c: pl.BlockSpec((1, r, c), lambda e, b, j, idx: (e, 0, 0))
    return pl.pallas_call(
        functools.partial(_ffn_kernel, tc, n_steps),
        grid_spec=pltpu.PrefetchScalarGridSpec(
            num_scalar_prefetch=1,
            grid=(N_EXPERTS, B, nblk),
            in_specs=[pl.BlockSpec(memory_space=pl.ANY),
                      wspec(D_MODEL, D_EXPERT), wspec(D_MODEL, D_EXPERT),
                      wspec(D_EXPERT, D_MODEL)],
            out_specs=pl.BlockSpec(memory_space=pl.ANY),
            scratch_shapes=[pltpu.VMEM((2, tc, HA_W), F32), pltpu.VMEM((2, tc, D_MODEL), F32),
                            pltpu.SemaphoreType.DMA((2,)), pltpu.SemaphoreType.DMA((2,))]),
        out_shape=jax.ShapeDtypeStruct(ha.shape, F32),
        input_output_aliases={1: 0},
        compiler_params=_params("arbitrary", "arbitrary", "arbitrary"),
        name="expert_ffn",
    )(idx_steps, ha, wg, wu, wd)


def _ln_kernel(x_ref, g_ref, b_ref, o_ref):
    o_ref[...] = _layer_norm(x_ref[...], g_ref[...], b_ref[...])


def _ln_rows(xin, g, b, tm=512):
    T = xin.shape[0]
    const = lambda i: (0, 0)
    return pl.pallas_call(
        _ln_kernel,
        grid=(T // tm,),
        in_specs=[pl.BlockSpec((tm, D_MODEL), lambda i: (i, 0)),
                  pl.BlockSpec((1, D_MODEL), const), pl.BlockSpec((1, D_MODEL), const)],
        out_specs=pl.BlockSpec((tm, D_MODEL), lambda i: (i, 0)),
        out_shape=jax.ShapeDtypeStruct((T, D_MODEL), F32),
        compiler_params=_params("parallel"),
        name="final_ln",
    )(xin, g, b)


def _pad_heads(w, heads, width):
    lead = w.shape[:-1]
    w = w.reshape(lead + (heads, width))
    w = jnp.pad(w, [(0, 0)] * len(lead) + [(0, 0), (0, HEAD_PAD - width)])
    return w.reshape(lead + (heads * HEAD_PAD,))


def _rot_half(w):
    half = w.shape[-1] // 2
    return jnp.concatenate([-w[..., half:], w[..., :half]], axis=-1)


def _pack_w_in(w):
    o = 0
    parts = {}
    for name, width in (("pool", POOL_DIM), ("cq", MLA_Q_RANK), ("ckv", MLA_KV_RANK),
                        ("kr", MLA_ROPE), ("gq", GLA_HEADS * GLA_DK), ("gk", GLA_HEADS * GLA_DK),
                        ("gv", GLA_HEADS * GLA_DV), ("gr", GLA_HEADS * GLA_DV),
                        ("gd", 2 * GLA_GATE_RANK), ("gates", 3 * D_MODEL)):
        parts[name] = w[:, o:o + width]
        o += width
    rope_slot = lambda m: jnp.pad(m, ((0, 0), (MLA_NOPE, HEAD_PAD - MLA_NOPE - MLA_ROPE)))
    kr2 = jnp.concatenate([rope_slot(parts["kr"]), rope_slot(_rot_half(parts["kr"]))], axis=1)
    gd = jnp.pad(parts["gd"], ((0, 0), (0, LANES - 2 * GLA_GATE_RANK)))
    cols = [parts["pool"], parts["cq"], parts["ckv"], kr2,
            _pad_heads(parts["gq"], GLA_HEADS, GLA_DK), _pad_heads(parts["gk"], GLA_HEADS, GLA_DK),
            parts["gv"], parts["gr"], gd, parts["gates"]]
    return jnp.concatenate(cols, axis=1).astype(BF16)


def _pack_mla(w_uq, w_ukv):
    r = w_uq.shape[0]
    uq = w_uq.reshape(r, MLA_HEADS, MLA_NOPE + MLA_ROPE)
    nope, rope = uq[..., :MLA_NOPE], uq[..., MLA_NOPE:]
    tail = jnp.zeros((r, MLA_HEADS, HEAD_PAD - MLA_NOPE - MLA_ROPE), F32)
    wq = jnp.concatenate([nope, rope, tail], axis=-1).reshape(r, -1)
    wq2 = jnp.concatenate([jnp.zeros_like(nope), _rot_half(rope), tail], axis=-1).reshape(r, -1)
    rk = w_ukv.shape[0]
    ukv = w_ukv.reshape(rk, MLA_HEADS, MLA_NOPE + MLA_V)
    wk = _pad_heads(ukv[..., :MLA_NOPE].reshape(rk, -1), MLA_HEADS, MLA_NOPE)
    wv = _pad_heads(ukv[..., MLA_NOPE:].reshape(rk, -1), MLA_HEADS, MLA_V)
    return wq.astype(BF16), wq2.astype(BF16), wk.astype(BF16), wv.astype(BF16)


def _pack_gla_decay(w_dec, b_dec):
    wd = []
    for d in range(2):
        rows = _pad_heads(w_dec[d], GLA_HEADS, GLA_DK)
        wd.append(jnp.pad(rows, ((d * GLA_GATE_RANK, LANES - (d + 1) * GLA_GATE_RANK),
                                 (0, 0))).astype(BF16))
    bd = [_pad_heads(b_dec[d][None, :], GLA_HEADS, GLA_DK) for d in range(2)]
    return wd, bd


def _chunk_tri(tm, reverse):
    r = jnp.arange(tm)[:, None]
    c = jnp.arange(tm)[None, :]
    same = (r // GLA_CHUNK) == (c // GLA_CHUNK)
    return (same & ((c >= r) if reverse else (c <= r))).astype(BF16)


def kernel(x, positions, ln0_g, ln0_b, w_in, b_gate, pool_w, pool_scale, w_up_a, mla_q_norm,
           mla_w_uq, mla_kv_norm, mla_w_ukv, w_up_b, gla_w_dec, gla_b_dec, gla_norm, w_up_c,
           w_out, ln1_g, ln1_b, router_w, exp_w_gate, exp_w_up, exp_w_down, ln2_g, ln2_b):
    B, S, D = x.shape
    assert D == D_MODEL and S % 512 == 0 and B == 2
    T = B * S
    C = CAPACITY_FACTOR * S // N_EXPERTS
    tc = min(512, C // 2)
    row = lambda v: v.reshape(1, -1).astype(F32)

    half = MLA_ROPE // 2
    freqs = ROPE_THETA ** (-jnp.arange(half, dtype=F32) / half)
    lanes = jnp.arange(LANES)
    in_rope = (lanes >= MLA_NOPE) & (lanes < MLA_NOPE + MLA_ROPE)
    freq_row = jnp.where(in_rope, freqs[(lanes - MLA_NOPE) % half], 0.0).reshape(1, LANES)
    rope_mask = in_rope.astype(F32).reshape(1, LANES)
    nope_row = (lanes < MLA_NOPE).astype(F32).reshape(1, LANES)
    one_row = (lanes == MLA_V).astype(F32).reshape(1, LANES)
    cos_t, sin_t = _rope_tables(positions.reshape(T, 1), freq_row, rope_mask)

    tri_f, tri_b = _chunk_tri(GLA_GROUP, False), _chunk_tri(GLA_GROUP, True)
    tri_lane = (jnp.arange(LANES)[:, None] <= jnp.arange(LANES)[None, :]).astype(BF16)

    stream, g_in, b_in = x.reshape(T, D), ln0_g, ln0_b
    for l in range(DEPTH):
        h, u, cq, ckv, kr2, gq, gk, gv, gr, gd, gl = _ln_inproj(
            stream, row(g_in), row(b_in), _pack_w_in(w_in[l]))
        wq, wq2, wk, wv = _pack_mla(mla_w_uq[l], mla_w_ukv[l])
        q, k, v = _mla_prep(cq, ckv, kr2, cos_t, sin_t, row(mla_q_norm[l]), row(mla_kv_norm[l]),
                            wq, wq2, wk, wv, nope_row, one_row, B, S)
        o_mla, (wg_b, wu_b, wd_b) = _flash(q, k, v, (exp_w_gate, exp_w_up, exp_w_down), l)
        o_mla = o_mla.reshape(T, MLA_HEADS * MLA_V)
        wd, bd = _pack_gla_decay(gla_w_dec[l], gla_b_dec[l])
        gla_o = _gla(gq, gk, gv, gd, gr, wd, bd, tri_f, tri_b, row(gla_norm[l]), B, S)
        rw = jnp.pad(router_w[l], ((0, 0), (0, LANES - N_EXPERTS)))
        ha, afft = _merge(u, o_mla, gla_o, gl, h, pool_w[l].astype(BF16), row(pool_scale[l]),
                          w_up_a[l].astype(BF16), w_up_b[l].astype(BF16),
                          w_up_c[l].astype(BF16), row(b_gate[l]),
                          (0.5 * w_out[l]).astype(BF16), row(ln1_g[l]), row(ln1_b[l]), rw, B, S)
        idx = _topk(afft, tri_lane, C)
        idx_steps = (idx + (jnp.arange(B, dtype=jnp.int32) * S)[:, None, None])
        idx_steps = idx_steps.transpose(1, 0, 2).reshape(-1)
        stream = _ffn(ha, idx_steps, wg_b, wu_b, wd_b, B, C, tc)
        g_in, b_in = ln2_g[l], ln2_b[l]
    out = _ln_rows(stream, row(g_in), row(b_in))
    return out.reshape(B, S, D)
```

```python
import functools

import jax
import jax.numpy as jnp
from jax import lax
from jax.experimental import pallas as pl
from jax.experimental.pallas import tpu as pltpu

F32 = jnp.float32
BF16 = jnp.bfloat16

LANES = 128
VMEM_LIMIT = 56 * 1024 * 1024
FLASH_VMEM_LIMIT = 60 * 1024 * 1024

D_MODEL = 1024
DEPTH = 2
POOL_WINDOWS = (2, 4, 8, 16)
POOL_DIM = 512
POOL_HALO = 16
MLA_HEADS = 8
MLA_NOPE = 64
MLA_ROPE = 32
MLA_V = 64
MLA_Q_RANK = 384
MLA_KV_RANK = 256
ROPE_THETA = 10000.0
GLA_HEADS = 4
GLA_DK = 64
GLA_DV = 128
GLA_GATE_RANK = 16
GLA_GATE_NORM = 16.0
GLA_CHUNK = 64
N_EXPERTS = 16
CAPACITY_FACTOR = 2
D_EXPERT = 2048
DN_ALPHA = (2 * DEPTH) ** 0.25
LN_EPS = 1e-5
RMS_EPS = 1e-6
LOG2_E = 1.4426950408889634
NEVER = 1e9

HEAD_PAD = LANES
GLA_W = GLA_HEADS * HEAD_PAD
HA_W = 2 * D_MODEL + LANES
DMA_UNROLL = 16
MERGE_GROUPS = 2
GLA_GROUP = 256

_SEG = (("pool", POOL_DIM), ("cq", MLA_Q_RANK), ("ckv", MLA_KV_RANK), ("kr2", 2 * HEAD_PAD),
        ("gq", GLA_W), ("gk", GLA_W), ("gv", GLA_W), ("gr", GLA_W), ("gd", LANES),
        ("gates", 3 * D_MODEL))
_SEG_OFF = {}
_off = 0
for _n, _w in _SEG:
    _SEG_OFF[_n] = (_off, _w)
    _off += _w
W_ALL_COLS = _off


def _params(*sem):
    return pltpu.CompilerParams(dimension_semantics=sem, vmem_limit_bytes=VMEM_LIMIT)


def _dot(a, b):
    return jnp.dot(a, b, preferred_element_type=F32)


def _dot_nt(a, b):
    return lax.dot_general(a, b, (((1,), (1,)), ((), ())), preferred_element_type=F32)


def _split(x):
    hi = x.astype(BF16)
    lo = (x - hi.astype(F32)).astype(BF16)
    return hi, lo


def _dot3(a, b, nt=False):
    d = _dot_nt if nt else _dot
    ah, al = _split(a)
    bh, bl = _split(b)
    return d(ah, bh) + d(ah, bl) + d(al, bh)


def _layer_norm(x, g, b):
    mu = jnp.mean(x, axis=-1, keepdims=True)
    xc = x - mu
    var = jnp.mean(xc * xc, axis=-1, keepdims=True)
    return xc * lax.rsqrt(var + LN_EPS) * g + b


def _rms_norm(x, g):
    ms = jnp.mean(x * x, axis=-1, keepdims=True)
    return x * lax.rsqrt(ms + RMS_EPS) * g


def _rope_kernel(pos_ref, freq_ref, mask_ref, cos_ref, sin_ref):
    ang = pos_ref[...].astype(F32) * freq_ref[...]
    cos_ref[...] = jnp.cos(ang) * mask_ref[...]
    sin_ref[...] = jnp.sin(ang) * mask_ref[...]


def _rope_tables(pos_col, freq_row, mask_row, tm=512):
    T = pos_col.shape[0]
    row = pl.BlockSpec((1, LANES), lambda i: (0, 0))
    out = pl.BlockSpec((tm, LANES), lambda i: (i, 0))
    return pl.pallas_call(
        _rope_kernel,
        grid=(T // tm,),
        in_specs=[pl.BlockSpec((tm, 1), lambda i: (i, 0)), row, row],
        out_specs=[out, out],
        out_shape=[jax.ShapeDtypeStruct((T, LANES), F32)] * 2,
        compiler_params=_params("parallel"),
        name="rope_tables",
    )(pos_col, freq_row, mask_row)


def _inproj_kernel(x_ref, xn_ref, g_ref, b_ref, w_ref, h_ref, *rest):
    out_refs, h_scr = rest[:-1], rest[-1]

    @pl.when(pl.program_id(0) == 0)
    def _():
        h_scr[...] = _layer_norm(x_ref[...], g_ref[...], b_ref[...])

    h = h_scr[...]
    h_ref[...] = h
    hb = h.astype(BF16)
    for (name, width), o_ref in zip(_SEG, out_refs):
        off = _SEG_OFF[name][0]
        o_ref[...] = _dot(hb, w_ref[:, off:off + width]).astype(o_ref.dtype)
    h_scr[...] = _layer_norm(xn_ref[...], g_ref[...], b_ref[...])


def _ln_inproj(xin, g, b, w_all, tm=512):
    T = xin.shape[0]
    n_steps = T // tm
    const = lambda i: (0, 0)
    rows = lambda w: pl.BlockSpec((tm, w), lambda i: (i, 0))
    out_shapes = [jax.ShapeDtypeStruct((T, D_MODEL), F32)]
    out_specs = [rows(D_MODEL)]
    for name, width in _SEG:
        out_shapes.append(jax.ShapeDtypeStruct((T, width), F32 if name == "gd" else BF16))
        out_specs.append(rows(width))
    return pl.pallas_call(
        _inproj_kernel,
        grid=(n_steps,),
        in_specs=[rows(D_MODEL),
                  pl.BlockSpec((tm, D_MODEL), lambda i: (jnp.minimum(i + 1, n_steps - 1), 0)),
                  pl.BlockSpec((1, D_MODEL), const), pl.BlockSpec((1, D_MODEL), const),
                  pl.BlockSpec((D_MODEL, W_ALL_COLS), const)],
        out_specs=out_specs,
        out_shape=out_shapes,
        scratch_shapes=[pltpu.VMEM((tm, D_MODEL), F32)],
        compiler_params=_params("arbitrary"),
        name="ln_inproj",
    )(xin, xin, g, b, w_all)


def _mla_prep_kernel(cq_ref, ckv_ref, kr2_ref, cos_ref, sin_ref, qn_ref, kvn_ref,
                     wq_ref, wq2_ref, wk_ref, wv_ref, nope_ref, one_ref,
                     q_ref, k_ref, v_ref):
    scale = (MLA_NOPE + MLA_ROPE) ** -0.5 * LOG2_E
    cosr = cos_ref[...]
    sinr = sin_ref[...]
    nq = _rms_norm(cq_ref[...].astype(F32), qn_ref[...]).astype(BF16)
    nkv = _rms_norm(ckv_ref[...].astype(F32), kvn_ref[...]).astype(BF16)
    qa = _dot(nq, wq_ref[...])
    qb = _dot(nq, wq2_ref[...])
    ka = _dot(nkv, wk_ref[...])
    va = _dot(nkv, wv_ref[...])
    kr2 = kr2_ref[...].astype(F32)
    k_rope = kr2[:, :HEAD_PAD] * cosr + kr2[:, HEAD_PAD:] * sinr
    q_cos = (cosr + nope_ref[...]) * scale
    q_sin = sinr * scale
    for h in range(MLA_HEADS):
        sl = slice(h * HEAD_PAD, (h + 1) * HEAD_PAD)
        q_ref[0, h] = (qa[:, sl] * q_cos + qb[:, sl] * q_sin).astype(BF16)
        k_ref[0, h] = (ka[:, sl] + k_rope).astype(BF16)
        v_ref[0, h] = (va[:, sl] + one_ref[...]).astype(BF16)


def _mla_prep(cq, ckv, kr2, cos_t, sin_t, qn, kvn, wq, wq2, wk, wv, nope_row, one_row, B, S,
              tm=512):
    nb = S // tm
    rows = lambda w: pl.BlockSpec((tm, w), lambda b, i: (b * nb + i, 0))
    const = lambda r, c: pl.BlockSpec((r, c), lambda b, i: (0, 0))
    hw = MLA_HEADS * HEAD_PAD
    out = pl.BlockSpec((1, MLA_HEADS, tm, HEAD_PAD), lambda b, i: (b, 0, i, 0))
    return pl.pallas_call(
        _mla_prep_kernel,
        grid=(B, nb),
        in_specs=[rows(MLA_Q_RANK), rows(MLA_KV_RANK), rows(2 * HEAD_PAD), rows(LANES),
                  rows(LANES), const(1, MLA_Q_RANK), const(1, MLA_KV_RANK),
                  const(MLA_Q_RANK, hw), const(MLA_Q_RANK, hw), const(MLA_KV_RANK, hw),
                  const(MLA_KV_RANK, hw), const(1, LANES), const(1, LANES)],
        out_specs=[out, out, out],
        out_shape=[jax.ShapeDtypeStruct((B, MLA_HEADS, S, HEAD_PAD), BF16)] * 3,
        compiler_params=_params("parallel", "parallel"),
        name="mla_prep",
    )(cq, ckv, kr2, cos_t, sin_t, qn, kvn, wq, wq2, wk, wv, nope_row, one_row)


def _flash_kernel(q_ref, k_ref, v_ref, wg_ref, wu_ref, wd_ref, o_ref, wgb_ref, wub_ref, wdb_ref,
                  m_scr, acc_scr):
    kv = pl.program_id(2)
    wgb_ref[...] = wg_ref[0].astype(BF16)
    wub_ref[...] = wu_ref[0].astype(BF16)
    wdb_ref[...] = wd_ref[0].astype(BF16)

    @pl.when(kv == 0)
    def _():
        m_scr[...] = jnp.full(m_scr.shape, -jnp.inf, F32)
        acc_scr[...] = jnp.zeros(acc_scr.shape, F32)

    tk = k_ref.shape[2]
    for h in range(MLA_HEADS):
        s = _dot_nt(q_ref[0, h], k_ref[0, h])
        m_prev = m_scr[h]
        m_new = jnp.maximum(m_prev, jnp.max(s, axis=1, keepdims=True))
        p = jnp.exp2(s - jnp.concatenate([m_new] * (tk // LANES), axis=1))
        alpha = jnp.exp2(m_prev - m_new)
        acc_scr[h] = alpha * acc_scr[h] + _dot(p.astype(BF16), v_ref[0, h])
        m_scr[h] = m_new

    @pl.when(kv == pl.num_programs(2) - 1)
    def _():
        lane = lax.broadcasted_iota(jnp.int32, (q_ref.shape[2], HEAD_PAD), 1)
        for hp in range(MLA_HEADS // 2):
            a0 = acc_scr[2 * hp]
            a1 = acc_scr[2 * hp + 1]
            o0 = a0 / a0[:, MLA_V:MLA_V + 1]
            o1 = a1 / a1[:, MLA_V:MLA_V + 1]
            pair = jnp.where(lane < MLA_V, o0, pltpu.roll(o1, MLA_V, axis=1))
            o_ref[0, :, hp * HEAD_PAD:(hp + 1) * HEAD_PAD] = pair.astype(BF16)


def _flash(q, k, v, expert_w, layer, tq=1024, tk=2048):
    B, H, S, _ = q.shape
    tq, tk = min(tq, S), min(tk, S)
    assert S % tq == 0 and S % tk == 0
    nq, nk = S // tq, S // tk
    n_steps = B * nq * nk
    qspec = pl.BlockSpec((1, H, tq, HEAD_PAD), lambda b, i, j: (b, 0, i, 0))
    kspec = pl.BlockSpec((1, H, tk, HEAD_PAD), lambda b, i, j: (b, 0, j, 0))
    w_in, w_specs, wb_specs, wb_shapes = [], [], [], []
    for w in expert_w:
        n_l, n_e, r, c = w.shape
        slab = n_e * r // n_steps
        assert n_e * r % n_steps == 0 and slab % 16 == 0
        w_in.append(w.reshape(n_l, n_steps, slab, c))
        w_specs.append(pl.BlockSpec((1, 1, slab, c),
                                    lambda b, i, j: (layer, (b * nq + i) * nk + j, 0, 0)))
        wb_specs.append(pl.BlockSpec((1, slab, c), lambda b, i, j: ((b * nq + i) * nk + j, 0, 0)))
        wb_shapes.append(jax.ShapeDtypeStruct((n_steps, slab, c), BF16))
    o, *wb = pl.pallas_call(
        _flash_kernel,
        grid=(B, nq, nk),
        in_specs=[qspec, kspec, kspec] + w_specs,
        out_specs=[pl.BlockSpec((1, tq, H * MLA_V), lambda b, i, j: (b, i, 0))] + wb_specs,
        out_shape=[jax.ShapeDtypeStruct((B, S, H * MLA_V), BF16)] + wb_shapes,
        scratch_shapes=[pltpu.VMEM((H, tq, LANES), F32), pltpu.VMEM((H, tq, HEAD_PAD), F32)],
        compiler_params=pltpu.CompilerParams(
            dimension_semantics=("parallel", "parallel", "arbitrary"),
            vmem_limit_bytes=FLASH_VMEM_LIMIT),
        name="mla_flash",
    )(q, k, v, *w_in)
    return o, [b16.reshape(w.shape[1:]) for b16, w in zip(wb, expert_w)]


def _log_sigmoid(x):
    return jnp.minimum(x, 0.0) - jnp.log(1.0 + jnp.exp(-jnp.abs(x)))


def _gla_scan(reverse, gq_ref, gk_ref, gv_ref, gd_ref, wd_ref, bd_ref, tri_ref, state_scr,
              emit):
    tm = gq_ref.shape[0]
    tg = tri_ref.shape[0]
    n_groups = tm // tg

    @pl.when(pl.program_id(1) == 0)
    def _():
        state_scr[...] = jnp.zeros(state_scr.shape, F32)

    states = [state_scr[h] for h in range(GLA_HEADS)]
    for grp in (range(n_groups - 1, -1, -1) if reverse else range(n_groups)):
        states = _gla_group(reverse, slice(grp * tg, (grp + 1) * tg), gq_ref, gk_ref, gv_ref,
                            gd_ref, wd_ref, bd_ref, tri_ref, states, emit)
    for h in range(GLA_HEADS):
        state_scr[h] = states[h]


def _gla_group(reverse, rows, gq_ref, gk_ref, gv_ref, gd_ref, wd_ref, bd_ref, tri_ref, states,
               emit):
    tg = tri_ref.shape[0]
    L = GLA_CHUNK
    n_chunks = tg // L
    logits = _dot(gd_ref[rows, :].astype(BF16), wd_ref[...]) + bd_ref[...]
    g = _log_sigmoid(logits) * (1.0 / GLA_GATE_NORM)
    g_hi, g_lo = _split(g)
    b = _dot(tri_ref[...], g_hi) + _dot(tri_ref[...], g_lo)
    edge = 0 if reverse else L - 1
    b_last = jnp.concatenate(
        [jnp.broadcast_to(b[c * L + edge:c * L + edge + 1, :], (L, GLA_W))
         for c in range(n_chunks)], axis=0)
    q_in = (gq_ref[rows, :].astype(F32) * (GLA_DK ** -0.5) * jnp.exp(b)).astype(BF16)
    k_f = gk_ref[rows, :].astype(F32)
    k_in = (k_f * jnp.exp(-b)).astype(BF16)
    k_st = (k_f * jnp.exp(b_last - b)).astype(BF16)
    dec = jnp.exp(b_last)
    v = gv_ref[rows, :]
    lane = lax.broadcasted_iota(jnp.int32, (HEAD_PAD, 2 * L), 1)
    order = range(n_chunks - 1, -1, -1) if reverse else range(n_chunks)
    heads = [slice(h * HEAD_PAD, (h + 1) * HEAD_PAD) for h in range(GLA_HEADS)]
    kv = {}
    o_intra = []
    for h, hs in enumerate(heads):
        a = _dot_nt(q_in[:, hs], k_in[:, hs]).astype(BF16) * tri_ref[...]
        o_intra.append(_dot(a, v[:, hs]))
        v_t = v[:, hs].astype(F32).T
        for c in range(n_chunks):
            pair = slice((c // 2) * 2 * L, (c // 2 + 1) * 2 * L)
            in_chunk = (lane >= L) if c % 2 else (lane < L)
            v_tc = jnp.where(in_chunk, v_t[:, pair], 0.0).astype(BF16)
            kv[c, h] = _dot(v_tc, k_st[pair, hs])
    entering = {}
    leaving = []
    for h, hs in enumerate(heads):
        st = states[h]
        for c in order:
            entering[c, h] = st.astype(BF16)
            st = dec[c * L:c * L + 1, hs] * st + kv[c, h]
        leaving.append(st)
    for h, hs in enumerate(heads):
        o_inter = jnp.concatenate(
            [_dot_nt(q_in[c * L:(c + 1) * L, hs], entering[c, h]) for c in range(n_chunks)],
            axis=0)
        emit(rows, hs, o_intra[h] + o_inter)
    return leaving


def _gla_fwd_kernel(gq_ref, gk_ref, gv_ref, gd_ref, wd_ref, bd_ref, tri_ref, o_ref, state_scr):
    def emit(rows, hs, tile):
        o_ref[rows, hs] = tile

    _gla_scan(False, gq_ref, gk_ref, gv_ref, gd_ref, wd_ref, bd_ref, tri_ref, state_scr, emit)


def _gla_bwd_kernel(gq_ref, gk_ref, gv_ref, gd_ref, wd_ref, bd_ref, tri_ref, of_ref, gr_ref,
                    ng_ref, out_ref, state_scr):
    def emit(rows, hs, tile):
        r = gr_ref[rows, hs].astype(F32)
        o = _rms_norm(of_ref[rows, hs] + tile, ng_ref[...])
        out_ref[rows, hs] = (o * (r * jax.nn.sigmoid(r))).astype(BF16)

    _gla_scan(True, gq_ref, gk_ref, gv_ref, gd_ref, wd_ref, bd_ref, tri_ref, state_scr, emit)


def _gla(gq, gk, gv, gd, gr, wd, bd, tri_f, tri_b, norm_g, B, S, tm=512):
    nb = S // tm
    T = B * S

    def call(reverse):
        blk = (lambda b, i: (b * nb + nb - 1 - i, 0)) if reverse else (lambda b, i: (b * nb + i, 0))
        rows = lambda w: pl.BlockSpec((tm, w), blk)
        const = lambda r, c: pl.BlockSpec((r, c), lambda b, i: (0, 0))
        in_specs = [rows(GLA_W), rows(GLA_W), rows(GLA_W), rows(LANES),
                    const(LANES, GLA_W), const(1, GLA_W), const(GLA_GROUP, GLA_GROUP)]
        scratch = [pltpu.VMEM((GLA_HEADS, GLA_DV, HEAD_PAD), F32)]
        d = 1 if reverse else 0
        args = [gq, gk, gv, gd, wd[d], bd[d], tri_b if reverse else tri_f]
        if reverse:
            in_specs += [rows(GLA_W), rows(GLA_W), const(1, GLA_DV)]
            args += [o_f, gr, norm_g]
        return pl.pallas_call(
            _gla_bwd_kernel if reverse else _gla_fwd_kernel,
            grid=(B, nb),
            in_specs=in_specs,
            out_specs=rows(GLA_W),
            out_shape=jax.ShapeDtypeStruct((T, GLA_W), BF16 if reverse else F32),
            scratch_shapes=scratch,
            compiler_params=_params("parallel", "arbitrary"),
            name="gla_bwd" if reverse else "gla_fwd",
        )(*args)

    o_f = call(False)
    return call(True)


def _merge_kernel(S, u_ref, up_ref, un_ref, om_ref, gla_ref, gl_ref, h_ref,
                  pw_ref, ps_ref, wa_ref, wb_ref, wc_ref, bg_ref, wo_ref, g_ref, b_ref,
                  rw_ref, ha_ref, afft_ref):
    tm = u_ref.shape[0]
    ext = jnp.concatenate([up_ref[...], u_ref[...], un_ref[...]], axis=0).astype(F32)
    tg = tm // MERGE_GROUPS
    for grp in range(MERGE_GROUPS):
        _merge_rows(S, pl.program_id(1) * tm + grp * tg, slice(grp * tg, (grp + 1) * tg),
                    ext[grp * tg:grp * tg + tg + 2 * POOL_HALO], om_ref, gla_ref, gl_ref, h_ref,
                    pw_ref, ps_ref, wa_ref, wb_ref, wc_ref, bg_ref, wo_ref, g_ref, b_ref,
                    rw_ref, ha_ref, afft_ref)


def _merge_rows(S, first_pos, rows, ext, om_ref, gla_ref, gl_ref, h_ref,
                pw_ref, ps_ref, wa_ref, wb_ref, wc_ref, bg_ref, wo_ref, g_ref, b_ref,
                rw_ref, ha_ref, afft_ref):
    n_ext = ext.shape[0]
    tg = n_ext - 2 * POOL_HALO
    pos_ext = first_pos - POOL_HALO + lax.broadcasted_iota(jnp.int32, (n_ext, LANES), 0)
    in_seq = (pos_ext >= 0) & (pos_ext < S)
    pos = first_pos + lax.broadcasted_iota(jnp.int32, (tg, LANES), 0)
    core = slice(POOL_HALO, POOL_HALO + tg)

    def rows_at(a, d):
        return pltpu.roll(a, (-d) % n_ext, axis=0)

    pooled = []
    for gi, w in enumerate(POOL_WINDOWS):
        hw = w // 2
        cs = slice(gi * LANES, (gi + 1) * LANES)
        x = jnp.where(in_seq, ext[:, cs], 0.0)
        win = rows_at(x, -1) + x
        reach = 1
        while reach < hw:
            win = rows_at(win, -reach) + rows_at(win, reach)
            reach *= 2
        cnt = (jnp.minimum(pos + hw, S) - jnp.maximum(pos - hw, 0)).astype(F32)
        pg = (win[core] / cnt - x[core]).astype(BF16)
        pooled.append(_dot(pg, pw_ref[gi]))
    pa = (jnp.concatenate(pooled, axis=1) * ps_ref[...]).astype(BF16)
    y_a = _dot(pa, wa_ref[...])
    y_b = _dot(om_ref[rows, :], wb_ref[...])
    y_c = _dot(gla_ref[rows, :], wc_ref[...])
    gates2 = 1.0 + jnp.tanh(0.5 * (gl_ref[rows, :].astype(F32) + bg_ref[...]))
    merged2 = (gates2[:, :D_MODEL] * y_a + gates2[:, D_MODEL:2 * D_MODEL] * y_b
               + gates2[:, 2 * D_MODEL:] * y_c)
    mix = _dot(merged2.astype(BF16), wo_ref[...])
    h1 = _layer_norm(DN_ALPHA * h_ref[rows, :] + mix, g_ref[...], b_ref[...])
    lane = lax.broadcasted_iota(jnp.int32, (tg, LANES), 1)
    logits = jnp.where(lane < N_EXPERTS, _dot3(h1, rw_ref[...]), -jnp.inf)
    e = jnp.exp(logits - jnp.max(logits, axis=1, keepdims=True))
    aff = e / jnp.sum(e, axis=1, keepdims=True)
    afft_ref[0, :, rows] = aff.T[:N_EXPERTS]
    ha_ref[rows, :D_MODEL] = DN_ALPHA * h1
    ha_ref[rows, D_MODEL:2 * D_MODEL] = h1
    ha_ref[rows, 2 * D_MODEL:] = aff


def _merge(u, o_mla, gla_o, gl, h, pw, ps, wa, wb, wc, bg, wo, g, b, rw, B, S, tm=512):
    nb = S // tm
    T = B * S
    hb = tm // POOL_HALO
    n_halo = T // POOL_HALO
    rows = lambda w: pl.BlockSpec((tm, w), lambda bb, i: (bb * nb + i, 0))
    prev = pl.BlockSpec((POOL_HALO, POOL_DIM),
                        lambda bb, i: (jnp.maximum((bb * nb + i) * hb - 1, 0), 0))
    nxt = pl.BlockSpec((POOL_HALO, POOL_DIM),
                       lambda bb, i: (jnp.minimum((bb * nb + i + 1) * hb, n_halo - 1), 0))
    c2 = lambda r, c: pl.BlockSpec((r, c), lambda bb, i: (0, 0))
    c3 = lambda a, r, c: pl.BlockSpec((a, r, c), lambda bb, i: (0, 0, 0))
    return pl.pallas_call(
        functools.partial(_merge_kernel, S),
        grid=(B, nb),
        in_specs=[rows(POOL_DIM), prev, nxt,
                  rows(MLA_HEADS * MLA_V), rows(GLA_W), rows(3 * D_MODEL), rows(D_MODEL),
                  c3(len(POOL_WINDOWS), LANES, LANES), c2(1, POOL_DIM), c2(POOL_DIM, D_MODEL),
                  c2(MLA_HEADS * MLA_V, D_MODEL), c2(GLA_W, D_MODEL), c2(1, 3 * D_MODEL),
                  c2(D_MODEL, D_MODEL), c2(1, D_MODEL), c2(1, D_MODEL), c2(D_MODEL, LANES)],
        out_specs=[rows(HA_W),
                   pl.BlockSpec((1, N_EXPERTS, tm), lambda bb, i: (bb, 0, i))],
        out_shape=[jax.ShapeDtypeStruct((T, HA_W), F32),
                   jax.ShapeDtypeStruct((B, N_EXPERTS, S), F32)],
        compiler_params=_params("parallel", "parallel"),
        name="merge",
    )(u, u, u, o_mla, gla_o, gl, h, pw, ps, wa, wb, wc, bg, wo, g, b, rw)


def _topk_kernel(C, aff_ref, tri_ref, idx_ref, p_scr):
    S = aff_ref.shape[2]
    n_chunks = S // LANES
    aff = aff_ref[0]

    def count(mask):
        return jnp.sum(jnp.where(mask, 1.0, 0.0), axis=1, keepdims=True)

    def as_float(bits):
        return lax.bitcast_convert_type(bits, F32)

    def refine(i, thr):
        cand = thr | jnp.left_shift(jnp.int32(1), 30 - i)
        return jnp.where(count(aff >= as_float(cand)) >= C, cand, thr)

    thr = lax.fori_loop(0, 31, refine, jnp.zeros((N_EXPERTS, 1), jnp.int32))
    above = aff >= as_float(thr + 1)
    tied = (aff >= as_float(thr)) & jnp.logical_not(above)
    need = C - count(above)
    tri = tri_ref[...]

    tied_f = tied.astype(F32)
    run = jnp.zeros((N_EXPERTS, 1), F32)
    sel_parts = []
    for c in range(n_chunks):
        cs = slice(c * LANES, (c + 1) * LANES)
        incl = _dot(tied_f[:, cs].astype(BF16), tri) + run
        run = incl[:, LANES - 1:LANES]
        sel_parts.append(jnp.where(above[:, cs] | (tied[:, cs] & (incl <= need)), 1.0, 0.0))

    p_scr[...] = jnp.zeros(p_scr.shape, F32)
    lane_e = lax.broadcasted_iota(jnp.int32, (N_EXPERTS, LANES), 1)
    chunk_end = jnp.full((N_EXPERTS, LANES), NEVER, F32)
    run = jnp.zeros((N_EXPERTS, 1), F32)
    for c in range(n_chunks):
        rel = _dot(sel_parts[c].astype(BF16), tri)
        p_scr[:, c, :] = rel
        run = run + rel[:, LANES - 1:LANES]
        chunk_end = jnp.where(lane_e == c, run, chunk_end)

    slot = lax.broadcasted_iota(jnp.int32, (C, LANES), 0).astype(F32)
    lane_c = lax.broadcasted_iota(jnp.int32, (C, LANES), 1).astype(F32)
    ones = jnp.ones((8, LANES), BF16)
    for e in range(N_EXPERTS):
        ce = chunk_end[e:e + 1, :]
        full = ce <= slot
        n_full = jnp.sum(jnp.where(full, 1.0, 0.0), axis=1, keepdims=True)
        base = jnp.max(jnp.where(full, ce, 0.0), axis=1, keepdims=True)
        pick = jnp.where(lane_c == n_full, 1.0, 0.0).astype(BF16)
        rel = _dot(pick, p_scr[e].astype(BF16))
        w = jnp.where(rel <= slot - base, 1.0, 0.0) + jnp.where(full, float(LANES), 0.0)
        tok = _dot_nt(ones, w.astype(BF16))[0:1, :]
        idx_ref[0, e:e + 1, :] = tok.astype(jnp.int32)


def _topk(afft, tri, C):
    B, E, S = afft.shape
    assert S // LANES <= LANES
    return pl.pallas_call(
        functools.partial(_topk_kernel, C),
        grid=(B,),
        in_specs=[pl.BlockSpec((1, E, S), lambda b: (b, 0, 0)),
                  pl.BlockSpec((LANES, LANES), lambda b: (0, 0))],
        out_specs=pl.BlockSpec((1, E, C), lambda b: (b, 0, 0)),
        out_shape=jax.ShapeDtypeStruct((B, E, C), jnp.int32),
        scratch_shapes=[pltpu.VMEM((E, LANES, LANES), F32)],
        compiler_params=_params("parallel"),
        name="expert_choice",
    )(afft, tri)


def _ffn_kernel(tc, n_steps, idx_ref, ha_in_ref, wg_ref, wu_ref, wd_ref, ha_ref,
                gbuf, sbuf, gsem, ssem):
    del ha_in_ref
    e = pl.program_id(0)
    s = (e * pl.num_programs(1) + pl.program_id(1)) * pl.num_programs(2) + pl.program_id(2)
    slot = s % 2

    def gather_copy(step, r, sl):
        row = idx_ref[step * tc + r]
        return pltpu.make_async_copy(ha_ref.at[pl.ds(row, 1), :],
                                     gbuf.at[sl, pl.ds(r, 1), :], gsem.at[sl])

    def scatter_copy(step, r, sl):
        row = idx_ref[step * tc + r]
        return pltpu.make_async_copy(sbuf.at[sl, pl.ds(r, 1), :],
                                     ha_ref.at[pl.ds(row, 1), pl.ds(0, D_MODEL)], ssem.at[sl])

    def start_rows(copy, step, sl):
        def body(r, carry):
            copy(step, 2 * r, sl).start(priority=0)
            copy(step, 2 * r + 1, sl).start(priority=1)
            return carry
        lax.fori_loop(0, tc // 2, body, 0, unroll=DMA_UNROLL // 2)

    def wait_gather(sl):
        pltpu.make_async_copy(ha_ref.at[pl.ds(0, tc), :], gbuf.at[sl], gsem.at[sl]).wait()

    def wait_scatter(sl):
        pltpu.make_async_copy(sbuf.at[sl], ha_ref.at[pl.ds(0, tc), pl.ds(0, D_MODEL)],
                              ssem.at[sl]).wait()

    @pl.when(s == 0)
    def _():
        start_rows(gather_copy, s, slot)

    @pl.when(s >= 2)
    def _():
        wait_scatter(slot)

    @pl.when(s + 1 < n_steps)
    def _():
        start_rows(gather_copy, s + 1, 1 - slot)

    wait_gather(slot)

    rows = gbuf[slot]
    x = rows[:, D_MODEL:2 * D_MODEL].astype(BF16)
    lane = lax.broadcasted_iota(jnp.int32, (tc, LANES), 1)
    gate = jnp.sum(jnp.where(lane == e, rows[:, 2 * D_MODEL:], 0.0), axis=1, keepdims=True)
    y = jnp.zeros((tc, D_MODEL), F32)
    fh = D_EXPERT // 2
    for f in range(2):
        fs = slice(f * fh, (f + 1) * fh)
        hg = _dot(x, wg_ref[0, :, fs])
        hu = _dot(x, wu_ref[0, :, fs])
        hid = (hg * jax.nn.sigmoid(hg) * hu).astype(BF16)
        y += _dot(hid, wd_ref[0, fs, :])
    sbuf[slot] = rows[:, :D_MODEL] + gate * y
    start_rows(scatter_copy, s, slot)

    @pl.when(s == n_steps - 1)
    def _():
        if n_steps >= 2:
            wait_scatter(1 - slot)
        wait_scatter(slot)


def _ffn(ha, idx_steps, wg, wu, wd, B, C, tc=512):
    nblk = C // tc
    assert B == 2 and nblk >= 2 and C % tc == 0
    n_steps = N_EXPERTS * B * nblk
    wspec = lambda r, c: pl.BlockSpec((1, r, c), lambda e, b, j, idx: (e, 0, 0))
    return pl.pallas_call(
        functools.partial(_ffn_kernel, tc, n_steps),
        grid_spec=pltpu.PrefetchScalarGridSpec(
            num_scalar_prefetch=1,
            grid=(N_EXPERTS, B, nblk),
            in_specs=[pl.BlockSpec(memory_space=pl.ANY),
                      wspec(D_MODEL, D_EXPERT), wspec(D_MODEL, D_EXPERT),
                      wspec(D_EXPERT, D_MODEL)],
            out_specs=pl.BlockSpec(memory_space=pl.ANY),
            scratch_shapes=[pltpu.VMEM((2, tc, HA_W), F32), pltpu.VMEM((2, tc, D_MODEL), F32),
                            pltpu.SemaphoreType.DMA((2,)), pltpu.SemaphoreType.DMA((2,))]),
        out_shape=jax.ShapeDtypeStruct(ha.shape, F32),
        input_output_aliases={1: 0},
        compiler_params=_params("arbitrary", "arbitrary", "arbitrary"),
        name="expert_ffn",
    )(idx_steps, ha, wg, wu, wd)


def _ln_kernel(x_ref, g_ref, b_ref, o_ref):
    o_ref[...] = _layer_norm(x_ref[...], g_ref[...], b_ref[...])


def _ln_rows(xin, g, b, tm=512):
    T = xin.shape[0]
    const = lambda i: (0, 0)
    return pl.pallas_call(
        _ln_kernel,
        grid=(T // tm,),
        in_specs=[pl.BlockSpec((tm, D_MODEL), lambda i: (i, 0)),
                  pl.BlockSpec((1, D_MODEL), const), pl.BlockSpec((1, D_MODEL), const)],
        out_specs=pl.BlockSpec((tm, D_MODEL), lambda i: (i, 0)),
        out_shape=jax.ShapeDtypeStruct((T, D_MODEL), F32),
        compiler_params=_params("parallel"),
        name="final_ln",
    )(xin, g, b)


def _pad_heads(w, heads, width):
    lead = w.shape[:-1]
    w = w.reshape(lead + (heads, width))
    w = jnp.pad(w, [(0, 0)] * len(lead) + [(0, 0), (0, HEAD_PAD - width)])
    return w.reshape(lead + (heads * HEAD_PAD,))


def _rot_half(w):
    half = w.shape[-1] // 2
    return jnp.concatenate([-w[..., half:], w[..., :half]], axis=-1)


def _pack_w_in(w):
    o = 0
    parts = {}
    for name, width in (("pool", POOL_DIM), ("cq", MLA_Q_RANK), ("ckv", MLA_KV_RANK),
                        ("kr", MLA_ROPE), ("gq", GLA_HEADS * GLA_DK), ("gk", GLA_HEADS * GLA_DK),
                        ("gv", GLA_HEADS * GLA_DV), ("gr", GLA_HEADS * GLA_DV),
                        ("gd", 2 * GLA_GATE_RANK), ("gates", 3 * D_MODEL)):
        parts[name] = w[:, o:o + width]
        o += width
    rope_slot = lambda m: jnp.pad(m, ((0, 0), (MLA_NOPE, HEAD_PAD - MLA_NOPE - MLA_ROPE)))
    kr2 = jnp.concatenate([rope_slot(parts["kr"]), rope_slot(_rot_half(parts["kr"]))], axis=1)
    gd = jnp.pad(parts["gd"], ((0, 0), (0, LANES - 2 * GLA_GATE_RANK)))
    cols = [parts["pool"], parts["cq"], parts["ckv"], kr2,
            _pad_heads(parts["gq"], GLA_HEADS, GLA_DK), _pad_heads(parts["gk"], GLA_HEADS, GLA_DK),
            parts["gv"], parts["gr"], gd, parts["gates"]]
    return jnp.concatenate(cols, axis=1).astype(BF16)


def _pack_mla(w_uq, w_ukv):
    r = w_uq.shape[0]
    uq = w_uq.reshape(r, MLA_HEADS, MLA_NOPE + MLA_ROPE)
    nope, rope = uq[..., :MLA_NOPE], uq[..., MLA_NOPE:]
    tail = jnp.zeros((r, MLA_HEADS, HEAD_PAD - MLA_NOPE - MLA_ROPE), F32)
    wq = jnp.concatenate([nope, rope, tail], axis=-1).reshape(r, -1)
    wq2 = jnp.concatenate([jnp.zeros_like(nope), _rot_half(rope), tail], axis=-1).reshape(r, -1)
    rk = w_ukv.shape[0]
    ukv = w_ukv.reshape(rk, MLA_HEADS, MLA_NOPE + MLA_V)
    wk = _pad_heads(ukv[..., :MLA_NOPE].reshape(rk, -1), MLA_HEADS, MLA_NOPE)
    wv = _pad_heads(ukv[..., MLA_NOPE:].reshape(rk, -1), MLA_HEADS, MLA_V)
    return wq.astype(BF16), wq2.astype(BF16), wk.astype(BF16), wv.astype(BF16)


def _pack_gla_decay(w_dec, b_dec):
    wd = []
    for d in range(2):
        rows = _pad_heads(w_dec[d], GLA_HEADS, GLA_DK)
        wd.append(jnp.pad(rows, ((d * GLA_GATE_RANK, LANES - (d + 1) * GLA_GATE_RANK),
                                 (0, 0))).astype(BF16))
    bd = [_pad_heads(b_dec[d][None, :], GLA_HEADS, GLA_DK) for d in range(2)]
    return wd, bd


def _chunk_tri(tm, reverse):
    r = jnp.arange(tm)[:, None]
    c = jnp.arange(tm)[None, :]
    same = (r // GLA_CHUNK) == (c // GLA_CHUNK)
    return (same & ((c >= r) if reverse else (c <= r))).astype(BF16)


def kernel(x, positions, ln0_g, ln0_b, w_in, b_gate, pool_w, pool_scale, w_up_a, mla_q_norm,
           mla_w_uq, mla_kv_norm, mla_w_ukv, w_up_b, gla_w_dec, gla_b_dec, gla_norm, w_up_c,
           w_out, ln1_g, ln1_b, router_w, exp_w_gate, exp_w_up, exp_w_down, ln2_g, ln2_b):
    B, S, D = x.shape
    assert D == D_MODEL and S % 512 == 0 and B == 2
    T = B * S
    C = CAPACITY_FACTOR * S // N_EXPERTS
    tc = min(512, C // 2)
    row = lambda v: v.reshape(1, -1).astype(F32)

    half = MLA_ROPE // 2
    freqs = ROPE_THETA ** (-jnp.arange(half, dtype=F32) / half)
    lanes = jnp.arange(LANES)
    in_rope = (lanes >= MLA_NOPE) & (lanes < MLA_NOPE + MLA_ROPE)
    freq_row = jnp.where(in_rope, freqs[(lanes - MLA_NOPE) % half], 0.0).reshape(1, LANES)
    rope_mask = in_rope.astype(F32).reshape(1, LANES)
    nope_row = (lanes < MLA_NOPE).astype(F32).reshape(1, LANES)
    one_row = (lanes == MLA_V).astype(F32).reshape(1, LANES)
    cos_t, sin_t = _rope_tables(positions.reshape(T, 1), freq_row, rope_mask)

    tri_f, tri_b = _chunk_tri(GLA_GROUP, False), _chunk_tri(GLA_GROUP, True)
    tri_lane = (jnp.arange(LANES)[:, None] <= jnp.arange(LANES)[None, :]).astype(BF16)

    stream, g_in, b_in = x.reshape(T, D), ln0_g, ln0_b
    for l in range(DEPTH):
        h, u, cq, ckv, kr2, gq, gk, gv, gr, gd, gl = _ln_inproj(
            stream, row(g_in), row(b_in), _pack_w_in(w_in[l]))
        wq, wq2, wk, wv = _pack_mla(mla_w_uq[l], mla_w_ukv[l])
        q, k, v = _mla_prep(cq, ckv, kr2, cos_t, sin_t, row(mla_q_norm[l]), row(mla_kv_norm[l]),
                            wq, wq2, wk, wv, nope_row, one_row, B, S)
        o_mla, (wg_b, wu_b, wd_b) = _flash(q, k, v, (exp_w_gate, exp_w_up, exp_w_down), l)
        o_mla = o_mla.reshape(T, MLA_HEADS * MLA_V)
        wd, bd = _pack_gla_decay(gla_w_dec[l], gla_b_dec[l])
        gla_o = _gla(gq, gk, gv, gd, gr, wd, bd, tri_f, tri_b, row(gla_norm[l]), B, S)
        rw = jnp.pad(router_w[l], ((0, 0), (0, LANES - N_EXPERTS)))
        ha, afft = _merge(u, o_mla, gla_o, gl, h, pool_w[l].astype(BF16), row(pool_scale[l]),
                          w_up_a[l].astype(BF16), w_up_b[l].astype(BF16),
                          w_up_c[l].astype(BF16), row(b_gate[l]),
                          (0.5 * w_out[l]).astype(BF16), row(ln1_g[l]), row(ln1_b[l]), rw, B, S)
        idx = _topk(afft, tri_lane, C)
        idx_steps = (idx + (jnp.arange(B, dtype=jnp.int32) * S)[:, None, None])
        idx_steps = idx_steps.transpose(1, 0, 2).reshape(-1)
        stream = _ffn(ha, idx_steps, wg_b, wu_b, wd_b, B, C, tc)
        g_in, b_in = ln2_g[l], ln2_b[l]
    out = _ln_rows(stream, row(g_in), row(b_in))
    return out.reshape(B, S, D)
```
